```python
import math
import jax, jax.numpy as jnp
from jax import lax
import numpy as np


D_MODEL = 2048
BATCH = 4
SEQ = 4096
DEPTH = 1

GRID_W = 64
CTX_LEN = 256
EPS = 1e-6
NEG_INF = -1e30
ROPE_THETA = 10000.0

DIFF_DH = 64
DIFF_DV = 2 * DIFF_DH
DIFF_HEADS = D_MODEL // (2 * DIFF_DV)
DIFF_WIDTH = DIFF_HEADS * DIFF_DV
Q_BLOCK = 128

SWA_DH = 64
SWA_HEADS = D_MODEL // (2 * SWA_DH)
SWA_KV = 4
SWA_GROUP = SWA_HEADS // SWA_KV
SWA_WIDTH = SWA_HEADS * SWA_DH
WINDOW = 128
BAND_BLOCK = 128

MIX_WIDTH = DIFF_WIDTH + SWA_WIDTH
ROPE_PAIRS = 16

W_DQ = DIFF_HEADS * 2 * DIFF_DH
W_DK = DIFF_HEADS * 2 * DIFF_DH
W_DV = DIFF_WIDTH
W_SQ = SWA_HEADS * SWA_DH
W_SK = SWA_KV * SWA_DH
W_SV = SWA_KV * SWA_DH
PROJ_OFFSETS = (W_DQ, W_DQ + W_DK, W_DQ + W_DK + W_DV, W_DQ + W_DK + W_DV + W_SQ,
                W_DQ + W_DK + W_DV + W_SQ + W_SK)
PROJ_WIDTH = W_DQ + W_DK + W_DV + W_SQ + W_SK + W_SV

PEER_HEADS = 8
PEER_TOPK = 16
N_KEYS = 128
N_EXPERTS = N_KEYS * N_KEYS
PEER_DQ = 256
PEER_DHALF = PEER_DQ // 2
TOKEN_CHUNK = 128

kernel_name = 'hymba_diffattn_swa_peer_dit_block'


def rms_norm(x, g):
    xf = x.astype(jnp.float32)
    y = xf * lax.rsqrt(jnp.mean(xf * xf, axis=-1, keepdims=True) + EPS)
    return y.astype(x.dtype) * g


def modulate(h, shift, scale):
    return h * (1 + scale) + shift


def head_rms(x, gain):
    xf = x.astype(jnp.float32)
    y = xf * lax.rsqrt(jnp.mean(xf * xf, axis=-1, keepdims=True) + EPS)
    return y.astype(x.dtype).reshape(x.shape[:2] + (-1,)) * gain


def axial_rope_tables(rows, dtype):
    row = jnp.repeat(jnp.arange(rows, dtype=jnp.float32), GRID_W)
    col = jnp.tile(jnp.arange(GRID_W, dtype=jnp.float32), rows)
    freqs = ROPE_THETA ** (-jnp.arange(ROPE_PAIRS, dtype=jnp.float32) / ROPE_PAIRS)
    ang = jnp.stack([row[:, None] * freqs, col[:, None] * freqs], axis=1)
    return jnp.cos(ang).astype(dtype), jnp.sin(ang).astype(dtype)


def apply_rope(x, cos, sin):
    xr = x.reshape(x.shape[:-1] + (2, 2, ROPE_PAIRS))
    x1, x2 = xr[..., 0, :], xr[..., 1, :]
    shape = (cos.shape[0],) + (1,) * (x.ndim - 3) + (2, ROPE_PAIRS)
    cs, sn = cos.reshape(shape), sin.reshape(shape)
    out = jnp.stack([x1 * cs - x2 * sn, x2 * cs + x1 * sn], axis=-2)
    return out.reshape(x.shape)


def split_heads(p):
    B, T = p.shape[:2]
    dq, dk, dv, sq, sk, sv = jnp.split(p, list(PROJ_OFFSETS), axis=-1)
    return (dq.reshape(B, T, DIFF_HEADS, 2, DIFF_DH), dk.reshape(B, T, DIFF_HEADS, 2, DIFF_DH),
            dv.reshape(B, T, DIFF_HEADS, DIFF_DV), sq.reshape(B, T, SWA_HEADS, SWA_DH),
            sk.reshape(B, T, SWA_KV, SWA_DH), sv.reshape(B, T, SWA_KV, SWA_DH))


def diff_attend(q, k, v, lam):
    s = jnp.einsum('bqhid,bkhid->bhiqk', q, k).astype(jnp.float32) * (DIFF_DH ** -0.5)
    a = jax.nn.softmax(s, axis=-1)
    a = a[:, :, 0] - lam * a[:, :, 1]
    return jnp.einsum('bhqk,bkhe->bqhe', a.astype(v.dtype), v)


def diff_latent(q, k, v, kc, vc, lam):
    B, S = q.shape[:2]
    nq = S // Q_BLOCK
    k_all = jnp.concatenate([kc, k], axis=1)
    v_all = jnp.concatenate([vc, v], axis=1)
    qb = jnp.moveaxis(q.reshape((B, nq, Q_BLOCK) + q.shape[2:]), 1, 0)
    o = lax.map(lambda qblk: diff_attend(qblk, k_all, v_all, lam), qb)
    return jnp.moveaxis(o, 0, 1).reshape(B, S, DIFF_HEADS, DIFF_DV)


def swa_latent(q, k, v, kc, vc, sink):
    B, S = q.shape[:2]
    nb = S // BAND_BLOCK
    scale = SWA_DH ** -0.5
    qb = q.reshape(B, nb, BAND_BLOCK, SWA_KV, SWA_GROUP, SWA_DH)

    def pad_blocks(t):
        t = jnp.pad(t, ((0, 0), (BAND_BLOCK, BAND_BLOCK), (0, 0), (0, 0)))
        return t.reshape(B, nb + 2, BAND_BLOCK, SWA_KV, SWA_DH)

    def band(t):
        return jnp.concatenate([t[:, :-2], t[:, 1:-1], t[:, 2:]], axis=2)

    kb, vb = band(pad_blocks(k)), band(pad_blocks(v))
    qi = jnp.arange(BAND_BLOCK)[:, None]
    ki = jnp.arange(3 * BAND_BLOCK)[None, :]
    rel_ok = jnp.abs(qi + BAND_BLOCK - ki) <= WINDOW
    sink_l = sink.reshape(SWA_KV, SWA_GROUP).astype(jnp.float32)
    n_band = 3 * BAND_BLOCK

    def one_block(args):
        qn, kn, vn, n = args
        kpos = (n - 1) * BAND_BLOCK + ki
        ok = rel_ok & (kpos >= 0) & (kpos < S)
        s_band = jnp.einsum('bqhgd,bkhd->bhgqk', qn, kn).astype(jnp.float32) * scale
        s_band = jnp.where(ok, s_band, NEG_INF)
        s_ctx = jnp.einsum('bqhgd,bkhd->bhgqk', qn, kc).astype(jnp.float32) * scale
        s_sink = jnp.broadcast_to(sink_l[None, :, :, None, None], s_band.shape[:-1] + (1,))
        p = jax.nn.softmax(jnp.concatenate([s_sink, s_band, s_ctx], axis=-1), axis=-1)
        pb = p[..., 1:1 + n_band].astype(vn.dtype)
        pc = p[..., 1 + n_band:].astype(vc.dtype)
        return (jnp.einsum('bhgqk,bkhd->bqhgd', pb, vn)
                + jnp.einsum('bhgqk,bkhd->bqhgd', pc, vc))

    xs = (jnp.moveaxis(qb, 1, 0), jnp.moveaxis(kb, 1, 0), jnp.moveaxis(vb, 1, 0),
          jnp.arange(nb, dtype=jnp.int32))
    o = lax.map(one_block, xs)
    return jnp.moveaxis(o, 0, 1).reshape(B, S, SWA_HEADS, SWA_DH)


def swa_context(qc, kc, vc, sink):
    B, L = qc.shape[:2]
    qg = qc.reshape(B, L, SWA_KV, SWA_GROUP, SWA_DH)
    s = jnp.einsum('bqhgd,bkhd->bhgqk', qg, kc).astype(jnp.float32) * (SWA_DH ** -0.5)
    sink_l = sink.reshape(SWA_KV, SWA_GROUP).astype(jnp.float32)
    s_sink = jnp.broadcast_to(sink_l[None, :, :, None, None], s.shape[:-1] + (1,))
    p = jax.nn.softmax(jnp.concatenate([s_sink, s], axis=-1), axis=-1)[..., 1:]
    o = jnp.einsum('bhgqk,bkhd->bqhgd', p.astype(vc.dtype), vc)
    return o.reshape(B, L, SWA_HEADS, SWA_DH)


def token_mixers(h, hc, w_in, w_out, lam, lam_init, diff_norm, sink, swa_norm, cos, sin, with_ctx_out):
    dq, dk, dv, sq, sk, sv = split_heads(h @ w_in)
    dqc, dkc, dvc, sqc, skc, svc = split_heads(hc @ w_in)
    dq, dk = apply_rope(dq, cos, sin), apply_rope(dk, cos, sin)
    sq, sk = apply_rope(sq, cos, sin), apply_rope(sk, cos, sin)

    def merge(o_diff, o_swa):
        return jnp.concatenate([head_rms(o_diff, diff_norm) * (1.0 - lam_init),
                                head_rms(o_swa, swa_norm)], axis=-1) @ w_out

    out = merge(diff_latent(dq, dk, dv, dkc, dvc, lam), swa_latent(sq, sk, sv, skc, svc, sink))
    out_c = None
    if with_ctx_out:
        out_c = merge(diff_attend(dqc, dkc, dvc, lam), swa_context(sqc, skc, svc, sink))
    return out, out_c


def peer(h, w_q, sub_keys, u, v):
    B, T, D = h.shape
    q = (h @ w_q).reshape(B, T, PEER_HEADS, 2, PEER_DHALF)
    s = jnp.einsum('bshid,hikd->bshik', q, sub_keys).astype(jnp.float32)
    s1, i1 = lax.top_k(s[..., 0, :], PEER_TOPK)
    s2, i2 = lax.top_k(s[..., 1, :], PEER_TOPK)
    cand = (s1[..., :, None] + s2[..., None, :]).reshape(B, T, PEER_HEADS, PEER_TOPK * PEER_TOPK)
    cidx = (i1[..., :, None] * N_KEYS + i2[..., None, :]).reshape(B, T, PEER_HEADS, PEER_TOPK * PEER_TOPK)
    top_s, pos = lax.top_k(cand, PEER_TOPK)
    idx = jnp.take_along_axis(cidx, pos, axis=-1)
    g = jax.nn.softmax(top_s, axis=-1)
    n_chunks = (B * T) // TOKEN_CHUNK
    hf = h.reshape(n_chunks, TOKEN_CHUNK, D)
    idx = idx.reshape(n_chunks, TOKEN_CHUNK, PEER_HEADS * PEER_TOPK)
    g = g.reshape(n_chunks, TOKEN_CHUNK, PEER_HEADS * PEER_TOPK)

    def one_chunk(args):
        hc, ic, gc = args
        u_sel = u[ic]
        act = jax.nn.gelu(jnp.einsum('cd,ced->ce', hc, u_sel))
        wts = (gc * act.astype(jnp.float32)).astype(v.dtype)
        return jnp.einsum('ce,ced->cd', wts, v[ic])

    return lax.map(one_chunk, (hf, idx, g)).reshape(B, T, D)


def setup_inputs(seed: int = 0) -> dict:
    key = jax.random.key(seed)
    ks = jax.random.split(key, 22)

    def nrm(k, shape, s):
        return jax.random.normal(k, shape, jnp.float32) * s

    D = D_MODEL
    return {
        'x': nrm(ks[0], (BATCH, SEQ, D), 1.0),
        'c': nrm(ks[1], (BATCH, D), 1.0),
        'ctx': nrm(ks[2], (BATCH, CTX_LEN, D), 1.0),
        'c_ctx': nrm(ks[3], (D,), 1.0),
        'w_ada': nrm(ks[4], (DEPTH, D, 6 * D), D ** -0.5),
        'b_ada': nrm(ks[5], (DEPTH, 6 * D), 0.02),
        'norm_attn': 1.0 + nrm(ks[6], (DEPTH, D), 0.02),
        'w_in': nrm(ks[7], (DEPTH, D, PROJ_WIDTH), D ** -0.5),
        'diff_lambda_q1': nrm(ks[8], (DEPTH, DIFF_DH), 0.1),
        'diff_lambda_k1': nrm(ks[9], (DEPTH, DIFF_DH), 0.1),
        'diff_lambda_q2': nrm(ks[10], (DEPTH, DIFF_DH), 0.1),
        'diff_lambda_k2': nrm(ks[11], (DEPTH, DIFF_DH), 0.1),
        'diff_norm': 1.0 + nrm(ks[12], (DEPTH, DIFF_WIDTH), 0.02),
        'swa_sink': nrm(ks[13], (DEPTH, SWA_HEADS), 1.0),
        'swa_norm': 1.0 + nrm(ks[14], (DEPTH, SWA_WIDTH), 0.02),
        'w_out': nrm(ks[15], (DEPTH, MIX_WIDTH, D), MIX_WIDTH ** -0.5),
        'norm_ffn': 1.0 + nrm(ks[16], (DEPTH, D), 0.02),
        'peer_w_q': nrm(ks[17], (DEPTH, D, PEER_HEADS * PEER_DQ), D ** -0.5),
        'peer_sub_keys': nrm(ks[18], (DEPTH, PEER_HEADS, 2, N_KEYS, PEER_DHALF), PEER_DHALF ** -0.5),
        'peer_u': nrm(ks[19], (DEPTH, N_EXPERTS, D), D ** -0.5),
        'peer_v': nrm(ks[20], (DEPTH, N_EXPERTS, D), 1.0),
        'final_norm': 1.0 + nrm(ks[21], (D,), 0.02),
    }


def reference(x, c, ctx, c_ctx, w_ada, b_ada, norm_attn, w_in, diff_lambda_q1, diff_lambda_k1,
              diff_lambda_q2, diff_lambda_k2, diff_norm, swa_sink, swa_norm, w_out, norm_ffn,
              peer_w_q, peer_sub_keys, peer_u, peer_v, final_norm):
    B, S, D = x.shape
    ROWS = S // GRID_W
    cos, sin = axial_rope_tables(ROWS, x.dtype)
    xc = ctx
    for l in range(DEPTH):
        last = l == DEPTH - 1
        mod = jax.nn.silu(c) @ w_ada[l] + b_ada[l]
        mod_c = jax.nn.silu(c_ctx) @ w_ada[l] + b_ada[l]
        sh1, sc1, g1, sh2, sc2, g2 = [m[:, None, :] for m in jnp.split(mod, 6, axis=-1)]
        sh1c, sc1c, g1c, sh2c, sc2c, g2c = jnp.split(mod_c, 6, axis=-1)

        lam_init = 0.8 - 0.6 * math.exp(-0.3 * l)
        lam = (jnp.exp(jnp.sum(diff_lambda_q1[l].astype(jnp.float32) * diff_lambda_k1[l].astype(jnp.float32)))
               - jnp.exp(jnp.sum(diff_lambda_q2[l].astype(jnp.float32) * diff_lambda_k2[l].astype(jnp.float32)))
               + lam_init)

        h = modulate(rms_norm(x, norm_attn[l]), sh1, sc1)
        hc = modulate(rms_norm(xc, norm_attn[l]), sh1c, sc1c)
        a, a_c = token_mixers(h, hc, w_in[l], w_out[l], lam, lam_init, diff_norm[l], swa_sink[l],
                              swa_norm[l], cos, sin, not last)
        x = x + g1 * a
        h2 = modulate(rms_norm(x, norm_ffn[l]), sh2, sc2)
        x = x + g2 * peer(h2, peer_w_q[l], peer_sub_keys[l], peer_u[l], peer_v[l])
        if not last:
            xc = xc + g1c * a_c
            h2c = modulate(rms_norm(xc, norm_ffn[l]), sh2c, sc2c)
            xc = xc + g2c * peer(h2c, peer_w_q[l], peer_sub_keys[l], peer_u[l], peer_v[l])
    return rms_norm(x, final_norm)
```

```python
import functools
import math

import jax
import jax.numpy as jnp
from jax import lax
from jax.experimental import pallas as pl
from jax.experimental.pallas import tpu as pltpu

F32 = jnp.float32
BF16 = jnp.bfloat16
I32 = jnp.int32

EPS = 1e-6
NEG_INF = -1e30
ROPE_THETA = 10000.0
GRID_W = 64
ROPE_PAIRS = 16

LANES = 128
DH = 64
DIFF_HEADS = 8
SWA_HEADS = 16
SWA_KV = 4
SWA_GROUP = SWA_HEADS // SWA_KV
BAND = 128
LAM_INIT = 0.8 - 0.6 * math.exp(-0.3 * 0)

PEER_HEADS = 8
PEER_TOPK = 16
N_KEYS = 128
PEER_SEL = PEER_HEADS * PEER_TOPK
SLAB = 16

VMEM_LIMIT = 56 * 1024 * 1024


def _cparams(sem):
    return pltpu.CompilerParams(dimension_semantics=sem, vmem_limit_bytes=VMEM_LIMIT)


def _adaln_kernel(c_ref, w_ref, b_ref, o_ref):
    c = c_ref[...]
    s = c * (1.0 / (1.0 + jnp.exp(-c)))
    o_ref[...] = jnp.dot(s, w_ref[...], preferred_element_type=F32,
                         precision=lax.Precision.HIGHEST) + b_ref[...]


def _adaln(cc, w, b):
    rows, d = cc.shape
    n = w.shape[1]
    tn = 1024
    return pl.pallas_call(
        _adaln_kernel,
        grid=(n // tn,),
        in_specs=[pl.BlockSpec((rows, d), lambda j: (0, 0)),
                  pl.BlockSpec((d, tn), lambda j: (0, j)),
                  pl.BlockSpec((1, tn), lambda j: (0, j))],
        out_specs=pl.BlockSpec((rows, tn), lambda j: (0, j)),
        out_shape=jax.ShapeDtypeStruct((rows, n), F32),
        compiler_params=_cparams(("arbitrary",)),
        name="adaln",
    )(cc, w, b.reshape(1, n))


def _modnorm(x, gain, shift, scale):
    y = x * lax.rsqrt(jnp.mean(x * x, axis=-1, keepdims=True) + EPS)
    return (y * gain) * (1.0 + scale) + shift


def _swap16(p):
    lane = lax.broadcasted_iota(I32, p.shape, 1)
    up = pltpu.roll(p, LANES - 16, 1)
    dn = pltpu.roll(p, 16, 1)
    return jnp.where((lane & 31) < 16, up, dn)


def _inproj_kernel(x_ref, sh_ref, sc_ref, g_ref, w_ref, cos_ref, sin_ref, o_ref, *,
                   rope_chunks, qscale_chunks):
    h = _modnorm(x_ref[0], g_ref[...], sh_ref[0], sc_ref[0]).astype(BF16)
    p = jnp.dot(h, w_ref[...], preferred_element_type=F32)
    n_chunks = p.shape[1] // LANES
    if rope_chunks:
        cs = cos_ref[...]
        sn = sin_ref[...]
    for j in range(n_chunks):
        pj = p[:, j * LANES:(j + 1) * LANES]
        if j in rope_chunks:
            pj = pj * cs + _swap16(pj) * sn
        if j in qscale_chunks:
            pj = pj * (DH ** -0.5)
        o_ref[0, :, j * LANES:(j + 1) * LANES] = pj.astype(BF16)


def _inproj(x, shift, scale, gain, w, cos_t, sin_t, rope, tm):
    b, s, d = x.shape
    n = w.shape[1]
    rope_chunks = frozenset(list(range(0, 16)) + list(range(24, 34))) if rope else frozenset()
    qscale_chunks = frozenset(list(range(0, 8)) + list(range(24, 32)))
    kern = functools.partial(_inproj_kernel, rope_chunks=rope_chunks, qscale_chunks=qscale_chunks)
    return pl.pallas_call(
        kern,
        grid=(b, s // tm),
        in_specs=[pl.BlockSpec((1, tm, d), lambda i, j: (i, j, 0)),
                  pl.BlockSpec((1, 1, d), lambda i, j: (i, 0, 0)),
                  pl.BlockSpec((1, 1, d), lambda i, j: (i, 0, 0)),
                  pl.BlockSpec((1, d), lambda i, j: (0, 0)),
                  pl.BlockSpec((d, n), lambda i, j: (0, 0), pipeline_mode=pl.Buffered(1)),
                  pl.BlockSpec((tm, LANES), lambda i, j: (j, 0)),
                  pl.BlockSpec((tm, LANES), lambda i, j: (j, 0))],
        out_specs=pl.BlockSpec((1, tm, n), lambda i, j: (i, j, 0)),
        out_shape=jax.ShapeDtypeStruct((b, s, n), BF16),
        compiler_params=_cparams(("arbitrary", "arbitrary")),
        name="in_proj",
    )(x, shift, scale, gain, w, cos_t, sin_t)


def _rope_tables(s):
    rows = s // GRID_W
    row = jnp.repeat(jnp.arange(rows, dtype=F32), GRID_W)
    col = jnp.tile(jnp.arange(GRID_W, dtype=F32), rows)
    freqs = ROPE_THETA ** (-jnp.arange(ROPE_PAIRS, dtype=F32) / ROPE_PAIRS)
    ar = row[:, None] * freqs
    ac = col[:, None] * freqs
    cos64 = jnp.concatenate([jnp.cos(ar), jnp.cos(ar), jnp.cos(ac), jnp.cos(ac)], axis=1)
    sin64 = jnp.concatenate([-jnp.sin(ar), jnp.sin(ar), -jnp.sin(ac), jnp.sin(ac)], axis=1)
    return jnp.tile(cos64, (1, 2)), jnp.tile(sin64, (1, 2))


def _head_rms(o, gain):
    return o * lax.rsqrt(jnp.mean(o * o, axis=-1, keepdims=True) + EPS) * gain


def _diff_attn_kernel(lq1_ref, lk1_ref, lq2_ref, lk2_ref, q_ref, k_ref, v_ref, kc_ref, vc_ref,
                      gain_ref, o_ref, m_ref, l_ref, acc_ref, *, tk):
    tq = q_ref.shape[1]
    s_len = k_ref.shape[1]
    q = q_ref[0]
    lane = lax.broadcasted_iota(I32, q.shape, 1)
    zero = jnp.zeros_like(q)
    qs = jnp.concatenate([jnp.where(lane < DH, q, zero), jnp.where(lane >= DH, q, zero)], axis=0)

    m_ref[...] = jnp.full(m_ref.shape, NEG_INF, F32)
    l_ref[...] = jnp.zeros(l_ref.shape, F32)
    acc_ref[...] = jnp.zeros(acc_ref.shape, F32)

    def step(kblk, vblk):
        s = lax.dot_general(qs, kblk, (((1,), (1,)), ((), ())), preferred_element_type=F32)
        m_old = m_ref[...]
        m_new = jnp.maximum(m_old, jnp.max(s, axis=-1, keepdims=True))
        alpha = jnp.exp(m_old - m_new)
        p = jnp.exp(s - m_new)
        l_ref[...] = alpha * l_ref[...] + jnp.sum(p, axis=-1, keepdims=True)
        acc_ref[...] = alpha * acc_ref[...] + jnp.dot(p.astype(BF16), vblk,
                                                      preferred_element_type=F32)
        m_ref[...] = m_new

    step(kc_ref[0], vc_ref[0])

    def body(j, carry):
        off = pl.multiple_of(j * tk, tk)
        step(k_ref[0, pl.ds(off, tk), :], v_ref[0, pl.ds(off, tk), :])
        return carry

    lax.fori_loop(0, s_len // tk, body, 0)

    lam = (jnp.exp(jnp.sum(lq1_ref[...] * lk1_ref[...], keepdims=True))
           - jnp.exp(jnp.sum(lq2_ref[...] * lk2_ref[...], keepdims=True)) + LAM_INIT)
    o = acc_ref[...] / l_ref[...]
    out = o[:tq] - lam * o[tq:]
    o_ref[0] = (_head_rms(out, gain_ref[...]) * (1.0 - LAM_INIT)).astype(BF16)


def _diff_attn(p, pc, lams, gain, tq, tk):
    b, s, _ = p.shape
    lc = pc.shape[1]
    lam_spec = pl.BlockSpec((1, DH), lambda i, h, j: (0, 0))
    return pl.pallas_call(
        functools.partial(_diff_attn_kernel, tk=tk),
        grid=(b, DIFF_HEADS, s // tq),
        in_specs=[lam_spec, lam_spec, lam_spec, lam_spec,
                  pl.BlockSpec((1, tq, LANES), lambda i, h, j: (i, j, h)),
                  pl.BlockSpec((1, s, LANES), lambda i, h, j: (i, 0, 8 + h)),
                  pl.BlockSpec((1, s, LANES), lambda i, h, j: (i, 0, 16 + h)),
                  pl.BlockSpec((1, lc, LANES), lambda i, h, j: (i, 0, 8 + h)),
                  pl.BlockSpec((1, lc, LANES), lambda i, h, j: (i, 0, 16 + h)),
                  pl.BlockSpec((1, LANES), lambda i, h, j: (0, h))],
        out_specs=pl.BlockSpec((1, tq, LANES), lambda i, h, j: (i, j, h)),
        out_shape=jax.ShapeDtypeStruct((b, s, DIFF_HEADS * LANES), BF16),
        scratch_shapes=[pltpu.VMEM((2 * tq, 1), F32), pltpu.VMEM((2 * tq, 1), F32),
                        pltpu.VMEM((2 * tq, LANES), F32)],
        compiler_params=_cparams(("arbitrary", "arbitrary", "arbitrary")),
        name="diff_attn",
    )(*lams, p, p, p, pc, pc, gain)


def _swa_attn_kernel(sink_ref, q_ref, k_ref, v_ref, kc_ref, vc_ref, gain_ref, o_ref,
                     kcat_ref, vcat_ref):
    n = pl.program_id(1)
    nb = pl.num_programs(1)
    s_len = k_ref.shape[1]
    lc = kc_ref.shape[1]
    n_band = 3 * BAND

    @pl.when(n == 0)
    def _():
        kcat_ref[n_band:, :] = kc_ref[0]
        vcat_ref[n_band:, :] = vc_ref[0]

    prev = pl.multiple_of(jnp.maximum(n - 1, 0) * BAND, BAND)
    cur = pl.multiple_of(n * BAND, BAND)
    nxt = pl.multiple_of(jnp.minimum(n + 1, nb - 1) * BAND, BAND)
    for t, off in enumerate((prev, cur, nxt)):
        kcat_ref[t * BAND:(t + 1) * BAND, :] = k_ref[0, pl.ds(off, BAND), :]
        vcat_ref[t * BAND:(t + 1) * BAND, :] = v_ref[0, pl.ds(off, BAND), :]

    rows = SWA_GROUP * BAND
    cols = n_band + lc
    qi = lax.broadcasted_iota(I32, (rows, cols), 0) & (BAND - 1)
    ki = lax.broadcasted_iota(I32, (rows, cols), 1)
    kpos = (n - 1) * BAND + ki
    in_band = jnp.where(jnp.abs(qi + BAND - ki) <= BAND,
                        jnp.where(kpos >= 0, jnp.where(kpos < s_len, 1, 0), 0), 0)
    ok = jnp.where(ki >= n_band, 1, in_band) > 0

    q = q_ref[0]
    for g in range(SWA_KV):
        kg = kcat_ref[:, g * DH:(g + 1) * DH]
        vg = vcat_ref[:, g * DH:(g + 1) * DH]
        qg = jnp.concatenate(
            [q[:, (g * SWA_GROUP + j) * DH:(g * SWA_GROUP + j + 1) * DH] for j in range(SWA_GROUP)],
            axis=0)
        s = lax.dot_general(qg, kg, (((1,), (1,)), ((), ())), preferred_element_type=F32)
        s = jnp.where(ok, s, NEG_INF)
        sink = jnp.concatenate(
            [jnp.full((BAND, 1), sink_ref[g * SWA_GROUP + j], F32) for j in range(SWA_GROUP)],
            axis=0)
        m = jnp.maximum(jnp.max(s, axis=-1, keepdims=True), sink)
        e = jnp.exp(s - m)
        denom = jnp.sum(e, axis=-1, keepdims=True) + jnp.exp(sink - m)
        o = jnp.dot(e.astype(BF16), vg, preferred_element_type=F32) / denom
        o = o * lax.rsqrt(jnp.mean(o * o, axis=-1, keepdims=True) + EPS)
        for j in range(SWA_GROUP):
            c0 = (g * SWA_GROUP + j) * DH
            o_ref[0, :, c0:c0 + DH] = (o[j * BAND:(j + 1) * BAND] * gain_ref[:, c0:c0 + DH]).astype(BF16)


def _swa_attn(p, pc, sink, gain):
    b, s, _ = p.shape
    lc = pc.shape[1]
    kvw = SWA_KV * DH
    return pl.pallas_call(
        _swa_attn_kernel,
        grid=(b, s // BAND),
        in_specs=[pl.BlockSpec(memory_space=pltpu.SMEM),
                  pl.BlockSpec((1, BAND, SWA_HEADS * DH), lambda i, j: (i, j, 3)),
                  pl.BlockSpec((1, s, kvw), lambda i, j: (i, 0, 16)),
                  pl.BlockSpec((1, s, kvw), lambda i, j: (i, 0, 17)),
                  pl.BlockSpec((1, lc, kvw), lambda i, j: (i, 0, 16)),
                  pl.BlockSpec((1, lc, kvw), lambda i, j: (i, 0, 17)),
                  pl.BlockSpec((1, SWA_HEADS * DH), lambda i, j: (0, 0))],
        out_specs=pl.BlockSpec((1, BAND, SWA_HEADS * DH), lambda i, j: (i, j, 0)),
        out_shape=jax.ShapeDtypeStruct((b, s, SWA_HEADS * DH), BF16),
        scratch_shapes=[pltpu.VMEM((3 * BAND + lc, kvw), BF16),
                        pltpu.VMEM((3 * BAND + lc, kvw), BF16)],
        compiler_params=_cparams(("arbitrary", "arbitrary")),
        name="swa_attn",
    )(sink, p, p, p, pc, pc, gain)


def _outproj_kernel(ad_ref, as_ref, wd_ref, ws_ref, x_ref, g1_ref, sh_ref, sc_ref, gn_ref,
                    x1_ref, h2_ref):
    a = (jnp.dot(ad_ref[0], wd_ref[...], preferred_element_type=F32)
         + jnp.dot(as_ref[0], ws_ref[...], preferred_element_type=F32))
    x1 = x_ref[0] + g1_ref[0] * a
    x1_ref[0] = x1
    h2_ref[0] = _modnorm(x1, gn_ref[...], sh_ref[0], sc_ref[0]).astype(BF16)


def _outproj(a_diff, a_swa, w_d, w_s, x, g1, sh2, sc2, gain, tm):
    b, s, d = x.shape
    wd = a_diff.shape[2]
    vec = pl.BlockSpec((1, 1, d), lambda i, j: (i, 0, 0))
    return pl.pallas_call(
        _outproj_kernel,
        grid=(b, s // tm),
        in_specs=[pl.BlockSpec((1, tm, wd), lambda i, j: (i, j, 0)),
                  pl.BlockSpec((1, tm, wd), lambda i, j: (i, j, 0)),
                  pl.BlockSpec((wd, d), lambda i, j: (0, 0)),
                  pl.BlockSpec((wd, d), lambda i, j: (0, 0)),
                  pl.BlockSpec((1, tm, d), lambda i, j: (i, j, 0)),
                  vec, vec, vec,
                  pl.BlockSpec((1, d), lambda i, j: (0, 0))],
        out_specs=[pl.BlockSpec((1, tm, d), lambda i, j: (i, j, 0)),
                   pl.BlockSpec((1, tm, d), lambda i, j: (i, j, 0))],
        out_shape=[jax.ShapeDtypeStruct((b, s, d), F32), jax.ShapeDtypeStruct((b, s, d), BF16)],
        compiler_params=_cparams(("arbitrary", "arbitrary")),
        name="out_proj",
    )(a_diff, a_swa, w_d, w_s, x, g1, sh2, sc2, gain)


def _extract_topk(s, payload, k):
    r = s.shape[0]
    row = lax.broadcasted_iota(I32, s.shape, 0).astype(F32)
    vals, pays = [], []
    for _ in range(k):
        m = jnp.max(s, axis=0, keepdims=True)
        first = jnp.min(jnp.where(s == m, row, float(r)), axis=0, keepdims=True)
        hit = row == first
        vals.append(m)
        pays.append(jnp.sum(jnp.where(hit, payload, 0.0), axis=0, keepdims=True))
        s = jnp.where(hit, -jnp.inf, s)
    return jnp.concatenate(vals, axis=0), jnp.concatenate(pays, axis=0)


def _peer_route_kernel(h_ref, wq_ref, keys_ref, idx_ref, gate_ref):
    q = jnp.dot(h_ref[...], wq_ref[...], preferred_element_type=F32).astype(BF16)
    tt = q.shape[0]
    key_id = lax.broadcasted_iota(I32, (N_KEYS, tt), 0).astype(F32)
    for h in range(PEER_HEADS):
        halves = []
        for i in range(2):
            c0 = (h * 2 + i) * N_KEYS
            s = lax.dot_general(keys_ref[h, i], q[:, c0:c0 + N_KEYS], (((1,), (1,)), ((), ())),
                                preferred_element_type=F32)
            halves.append(_extract_topk(s, key_id, PEER_TOPK))
        (s1, i1), (s2, i2) = halves
        cand = jnp.concatenate([s1[a:a + 1] + s2 for a in range(PEER_TOPK)], axis=0)
        cidx = jnp.concatenate([i1[a:a + 1] * N_KEYS + i2 for a in range(PEER_TOPK)], axis=0)
        top_s, top_i = _extract_topk(cand, cidx, PEER_TOPK)
        e = jnp.exp(top_s - jnp.max(top_s, axis=0, keepdims=True))
        r0 = h * PEER_TOPK
        idx_ref[r0:r0 + PEER_TOPK, :] = top_i.astype(I32)
        gate_ref[r0:r0 + PEER_TOPK, :] = e / jnp.sum(e, axis=0, keepdims=True)


def _peer_route(h2, wq, keys, tt):
    t, d = h2.shape
    nq = wq.shape[1]
    return pl.pallas_call(
        _peer_route_kernel,
        grid=(t // tt,),
        in_specs=[pl.BlockSpec((tt, d), lambda i: (i, 0)),
                  pl.BlockSpec((d, nq), lambda i: (0, 0)),
                  pl.BlockSpec(keys.shape, lambda i: (0, 0, 0, 0))],
        out_specs=[pl.BlockSpec((PEER_SEL, tt), lambda i: (0, i)),
                   pl.BlockSpec((PEER_SEL, tt), lambda i: (0, i))],
        out_shape=[jax.ShapeDtypeStruct((PEER_SEL, t), I32), jax.ShapeDtypeStruct((PEER_SEL, t), F32)],
        compiler_params=_cparams(("arbitrary",)),
        name="peer_route",
    )(h2, wq, keys)


def _split_bf16(x):
    hi = x.astype(BF16)
    return hi, (x - hi.astype(F32)).astype(BF16)


def _peer_expert_kernel(idx_cur_ref, idx_nxt_ref, uv_ref, h_ref, gate_ref, grp_ref, grpt_ref,
                        o_ref, buf_ref, sem_ref, *, tb):
    i = pl.program_id(0)
    nsteps = pl.num_programs(0)
    slot = i & 1

    def row_copy(idx_ref, dst_slot, t, j):
        return pltpu.make_async_copy(uv_ref.at[idx_ref[t * PEER_SEL + j]],
                                     buf_ref.at[dst_slot, t, j], sem_ref.at[dst_slot, t])

    def start_block(idx_ref, dst_slot):
        for t in range(tb):
            for j in range(PEER_SEL):
                row_copy(idx_ref, dst_slot, t, j).start()

    @pl.when(i == 0)
    def _():
        start_block(idx_cur_ref, 0)

    @pl.when(i + 1 < nsteps)
    def _():
        start_block(idx_nxt_ref, 1 - slot)

    diag = (lax.broadcasted_iota(I32, (SLAB, PEER_SEL * SLAB), 1) & (SLAB - 1)) == \
        lax.broadcasted_iota(I32, (SLAB, PEER_SEL * SLAB), 0)

    zs = []
    for t in range(tb):
        for j in range(PEER_SEL):
            row_copy(idx_cur_ref, slot, t, j).wait()
        u2d = buf_ref[slot, t, :, 0:SLAB, :].reshape(PEER_SEL * SLAB, LANES)
        y = lax.dot_general(h_ref[t], u2d, (((1,), (1,)), ((), ())), preferred_element_type=F32)
        zs.append(jnp.sum(jnp.where(diag, y, 0.0), axis=0, keepdims=True))
    z = jnp.concatenate(zs, axis=0)
    z_hi, z_lo = _split_bf16(z)
    act = (jnp.dot(z_hi, grp_ref[...], preferred_element_type=F32)
           + jnp.dot(z_lo, grp_ref[...], preferred_element_type=F32))
    gelu = 0.5 * act * (1.0 + jnp.tanh(math.sqrt(2.0 / math.pi) * (act + 0.044715 * act * act * act)))
    w = gate_ref[...] * gelu
    w_hi, w_lo = _split_bf16(w)
    w_rep = (jnp.dot(w_hi, grpt_ref[...], preferred_element_type=F32)
             + jnp.dot(w_lo, grpt_ref[...], preferred_element_type=F32))
    for t in range(tb):
        wt = jnp.where(diag, jnp.broadcast_to(w_rep[t:t + 1], diag.shape), 0.0)
        wt_hi, wt_lo = _split_bf16(wt)
        v2d = buf_ref[slot, t, :, SLAB:2 * SLAB, :].reshape(PEER_SEL * SLAB, LANES)
        r = jnp.dot(jnp.concatenate([wt_hi, wt_lo], axis=0), v2d, preferred_element_type=F32)
        o_ref[t] = r[:SLAB] + r[SLAB:]


def _peer_expert(idx_flat, uv, h_slab, gate, tb):
    t = h_slab.shape[0]
    nsteps = t // tb
    cols = lax.broadcasted_iota(I32, (PEER_SEL * SLAB, PEER_SEL), 0) // SLAB
    grp = (cols == lax.broadcasted_iota(I32, (PEER_SEL * SLAB, PEER_SEL), 1)).astype(BF16)
    blk = tb * PEER_SEL
    return pl.pallas_call(
        functools.partial(_peer_expert_kernel, tb=tb),
        grid=(nsteps,),
        in_specs=[pl.BlockSpec((blk,), lambda i: (i,), memory_space=pltpu.SMEM),
                  pl.BlockSpec((blk,), lambda i: (jnp.minimum(i + 1, nsteps - 1),),
                               memory_space=pltpu.SMEM),
                  pl.BlockSpec(memory_space=pl.ANY),
                  pl.BlockSpec((tb, SLAB, LANES), lambda i: (i, 0, 0)),
                  pl.BlockSpec((tb, PEER_SEL), lambda i: (i, 0)),
                  pl.BlockSpec(grp.shape, lambda i: (0, 0)),
                  pl.BlockSpec(grp.T.shape, lambda i: (0, 0))],
        out_specs=pl.BlockSpec((tb, SLAB, LANES), lambda i: (i, 0, 0)),
        out_shape=jax.ShapeDtypeStruct((t, SLAB, LANES), F32),
        scratch_shapes=[pltpu.VMEM((2, tb, PEER_SEL, 2 * SLAB, LANES), BF16),
                        pltpu.SemaphoreType.DMA((2, tb))],
        compiler_params=_cparams(("arbitrary",)),
        name="peer_expert",
    )(idx_flat, idx_flat, uv, h_slab, gate, grp, grp.T)


def _final_kernel(x1_ref, pe_ref, g2_ref, gn_ref, o_ref):
    x2 = x1_ref[0] + g2_ref[0] * pe_ref[0]
    o_ref[0] = x2 * lax.rsqrt(jnp.mean(x2 * x2, axis=-1, keepdims=True) + EPS) * gn_ref[...]


def _final(x1, pe, g2, gain, tm):
    b, s, d = x1.shape
    tok = pl.BlockSpec((1, tm, d), lambda i, j: (i, j, 0))
    return pl.pallas_call(
        _final_kernel,
        grid=(b, s // tm),
        in_specs=[tok, tok, pl.BlockSpec((1, 1, d), lambda i, j: (i, 0, 0)),
                  pl.BlockSpec((1, d), lambda i, j: (0, 0))],
        out_specs=tok,
        out_shape=jax.ShapeDtypeStruct((b, s, d), F32),
        compiler_params=_cparams(("arbitrary", "arbitrary")),
        name="final_norm",
    )(x1, pe, g2, gain)


def kernel(x, c, ctx, c_ctx, w_ada, b_ada, norm_attn, w_in, diff_lambda_q1, diff_lambda_k1,
           diff_lambda_q2, diff_lambda_k2, diff_norm, swa_sink, swa_norm, w_out, norm_ffn,
           peer_w_q, peer_sub_keys, peer_u, peer_v, final_norm):
    b, s, d = x.shape
    assert w_ada.shape[0] == 1, "single layer only"
    t = b * s

    cc = jnp.zeros((8, d), F32).at[:b].set(c).at[b].set(c_ctx)
    mod = _adaln(cc, w_ada[0], b_ada[0])
    sh1, sc1, g1, sh2, sc2, g2 = [m[:, None, :] for m in jnp.split(mod, 6, axis=-1)]

    w_in_b = w_in[0].astype(BF16)
    gain_attn = norm_attn[0].reshape(1, d)
    cos_t, sin_t = _rope_tables(s)
    tm = min(512, s)
    p = _inproj(x, sh1[:b], sc1[:b], gain_attn, w_in_b, cos_t, sin_t, True, tm)
    lc = ctx.shape[1]
    ones = jnp.ones((lc, LANES), F32)
    ctx_sh = jnp.broadcast_to(sh1[b:b + 1], (b, 1, d))
    ctx_sc = jnp.broadcast_to(sc1[b:b + 1], (b, 1, d))
    pc = _inproj(ctx, ctx_sh, ctx_sc, gain_attn, w_in_b, ones, ones, False, lc)

    lams = [v[0].reshape(1, DH).astype(F32) for v in
            (diff_lambda_q1, diff_lambda_k1, diff_lambda_q2, diff_lambda_k2)]
    a_diff = _diff_attn(p, pc, lams, diff_norm[0].reshape(1, -1), min(256, s), min(512, s))
    a_swa = _swa_attn(p, pc, swa_sink[0].astype(F32), swa_norm[0].reshape(1, -1))

    w_out_b = w_out[0].astype(BF16)
    dw = a_diff.shape[2]
    x1, h2 = _outproj(a_diff, a_swa, w_out_b[:dw], w_out_b[dw:], x, g1[:b], sh2[:b], sc2[:b],
                      norm_ffn[0].reshape(1, d), tm)

    idx_t, gate_t = _peer_route(h2.reshape(t, d), peer_w_q[0].astype(BF16),
                                peer_sub_keys[0].astype(BF16), min(256, t))
    uv = jnp.concatenate([peer_u[0].astype(BF16).reshape(-1, SLAB, LANES),
                          peer_v[0].astype(BF16).reshape(-1, SLAB, LANES)], axis=1)
    pe = _peer_expert(idx_t.T.reshape(-1), uv, h2.reshape(t, SLAB, LANES), gate_t.T, 16)

    return _final(x1, pe.reshape(b, s, d), g2[:b], final_norm.reshape(1, d), tm)
```

```python
import functools
import math

import jax
import jax.numpy as jnp
from jax import lax
from jax.experimental import pallas as pl
from jax.experimental.pallas import tpu as pltpu

F32 = jnp.float32
BF16 = jnp.bfloat16
I32 = jnp.int32

EPS = 1e-6
NEG_INF = -1e30
ROPE_THETA = 10000.0
GRID_W = 64
ROPE_PAIRS = 16

LANES = 128
DH = 64
DIFF_HEADS = 8
SWA_HEADS = 16
SWA_KV = 4
SWA_GROUP = SWA_HEADS // SWA_KV
BAND = 128
LAM_INIT = 0.8 - 0.6 * math.exp(-0.3 * 0)
LOG2E = math.log2(math.e)

PEER_HEADS = 8
PEER_TOPK = 16
N_KEYS = 128
PEER_SEL = PEER_HEADS * PEER_TOPK
SLAB = 16

VMEM_LIMIT = 56 * 1024 * 1024


def _cparams(sem):
    return pltpu.CompilerParams(dimension_semantics=sem, vmem_limit_bytes=VMEM_LIMIT)


def _adaln_kernel(c_ref, w_ref, b_ref, o_ref):
    c = c_ref[...]
    s = c * (1.0 / (1.0 + jnp.exp(-c)))
    o_ref[...] = jnp.dot(s, w_ref[...], preferred_element_type=F32,
                         precision=lax.Precision.HIGHEST) + b_ref[...]


def _adaln(cc, w, b):
    rows, d = cc.shape
    n = w.shape[1]
    tn = 1024
    return pl.pallas_call(
        _adaln_kernel,
        grid=(n // tn,),
        in_specs=[pl.BlockSpec((rows, d), lambda j: (0, 0)),
                  pl.BlockSpec((d, tn), lambda j: (0, j)),
                  pl.BlockSpec((1, tn), lambda j: (0, j))],
        out_specs=pl.BlockSpec((rows, tn), lambda j: (0, j)),
        out_shape=jax.ShapeDtypeStruct((rows, n), F32),
        compiler_params=_cparams(("arbitrary",)),
        name="adaln",
    )(cc, w, b.reshape(1, n))


def _modnorm(x, gain, shift, scale):
    y = x * lax.rsqrt(jnp.mean(x * x, axis=-1, keepdims=True) + EPS)
    return (y * gain) * (1.0 + scale) + shift


def _swap16(p):
    lane = lax.broadcasted_iota(I32, p.shape, 1)
    up = pltpu.roll(p, LANES - 16, 1)
    dn = pltpu.roll(p, 16, 1)
    return jnp.where((lane & 31) < 16, up, dn)


def _inproj_kernel(x_ref, sh_ref, sc_ref, g_ref, w_ref, wvt_ref, cos_ref, sin_ref, o_ref, vt_ref, *,
                   rope_chunks, chunk_scale):
    h = _modnorm(x_ref[0], g_ref[...], sh_ref[0], sc_ref[0]).astype(BF16)
    p = jnp.dot(h, w_ref[...], preferred_element_type=F32)
    n_chunks = p.shape[1] // LANES
    if rope_chunks:
        cs = cos_ref[...]
        sn = sin_ref[...]
    for j in range(n_chunks):
        pj = p[:, j * LANES:(j + 1) * LANES]
        if j in rope_chunks:
            pj = pj * cs + _swap16(pj) * sn
        if j in chunk_scale:
            pj = pj * chunk_scale[j]
        o_ref[0, :, j * LANES:(j + 1) * LANES] = pj.astype(BF16)
    vt_ref[0] = lax.dot_general(wvt_ref[...], h, (((1,), (1,)), ((), ())),
                                preferred_element_type=F32).astype(BF16)


def _inproj(x, shift, scale, gain, w, w_vt, cos_t, sin_t, rope, tm):
    b, s, d = x.shape
    n = w.shape[1]
    nv = w_vt.shape[0]
    rope_chunks = frozenset(list(range(0, 16)) + list(range(24, 34))) if rope else frozenset()
    chunk_scale = {j: DH ** -0.5 * LOG2E for j in range(0, 8)}
    chunk_scale.update({j: DH ** -0.5 for j in range(24, 32)})
    kern = functools.partial(_inproj_kernel, rope_chunks=rope_chunks, chunk_scale=chunk_scale)
    return pl.pallas_call(
        kern,
        grid=(b, s // tm),
        in_specs=[pl.BlockSpec((1, tm, d), lambda i, j: (i, j, 0)),
                  pl.BlockSpec((1, 1, d), lambda i, j: (i, 0, 0)),
                  pl.BlockSpec((1, 1, d), lambda i, j: (i, 0, 0)),
                  pl.BlockSpec((1, d), lambda i, j: (0, 0)),
                  pl.BlockSpec((d, n), lambda i, j: (0, 0), pipeline_mode=pl.Buffered(1)),
                  pl.BlockSpec((nv, d), lambda i, j: (0, 0), pipeline_mode=pl.Buffered(1)),
                  pl.BlockSpec((tm, LANES), lambda i, j: (j, 0)),
                  pl.BlockSpec((tm, LANES), lambda i, j: (j, 0))],
        out_specs=[pl.BlockSpec((1, tm, n), lambda i, j: (i, j, 0)),
                   pl.BlockSpec((1, nv, tm), lambda i, j: (i, 0, j))],
        out_shape=[jax.ShapeDtypeStruct((b, s, n), BF16), jax.ShapeDtypeStruct((b, nv, s), BF16)],
        compiler_params=_cparams(("arbitrary", "arbitrary")),
        name="in_proj",
    )(x, shift, scale, gain, w, w_vt, cos_t, sin_t)


def _rope_tables(s):
    rows = s // GRID_W
    row = jnp.repeat(jnp.arange(rows, dtype=F32), GRID_W)
    col = jnp.tile(jnp.arange(GRID_W, dtype=F32), rows)
    freqs = ROPE_THETA ** (-jnp.arange(ROPE_PAIRS, dtype=F32) / ROPE_PAIRS)
    ar = row[:, None] * freqs
    ac = col[:, None] * freqs
    cos64 = jnp.concatenate([jnp.cos(ar), jnp.cos(ar), jnp.cos(ac), jnp.cos(ac)], axis=1)
    sin64 = jnp.concatenate([-jnp.sin(ar), jnp.sin(ar), -jnp.sin(ac), jnp.sin(ac)], axis=1)
    return jnp.tile(cos64, (1, 2)), jnp.tile(sin64, (1, 2))


def _head_rms(o, gain):
    return o * lax.rsqrt(jnp.mean(o * o, axis=-1, keepdims=True) + EPS) * gain


def _diff_attn_kernel(lq1_ref, lk1_ref, lq2_ref, lk2_ref, q_ref, k_ref, vt_ref, kc_ref, vtc_ref,
                      gain_ref, o_ref, *, tk):
    tq = q_ref.shape[1]
    s_len = k_ref.shape[1]
    q = q_ref[0]
    lane = lax.broadcasted_iota(I32, q.shape, 1)
    zero = jnp.zeros_like(q)
    q_maps = (jnp.where(lane < DH, q, zero), jnp.where(lane >= DH, q, zero))
    state = [(jnp.full((1, tq), NEG_INF, F32), jnp.zeros((1, tq), F32),
              jnp.zeros((LANES, tq), F32)) for _ in q_maps]

    def scores(kblk):
        return [lax.dot_general(kblk, qm, (((1,), (1,)), ((), ())), preferred_element_type=F32)
                for qm in q_maps]

    def update(s, vtblk, m, l, acc):
        m_new = jnp.maximum(m, jnp.max(s, axis=0, keepdims=True))
        alpha = jnp.exp2(m - m_new)
        p = jnp.exp2(s - m_new)
        l = alpha * l + jnp.sum(p, axis=0, keepdims=True)
        acc = alpha * acc + jnp.dot(vtblk, p.astype(BF16), preferred_element_type=F32)
        return m_new, l, acc

    k_blocks = [lambda: kc_ref[0]]
    vt_blocks = [lambda: vtc_ref[0]]
    for j in range(s_len // tk):
        k_blocks.append(lambda j=j: k_ref[0, j * tk:(j + 1) * tk, :])
        vt_blocks.append(lambda j=j: vt_ref[0, :, j * tk:(j + 1) * tk])
    s_cur = scores(k_blocks[0]())
    for j in range(len(k_blocks)):
        s_nxt = scores(k_blocks[j + 1]()) if j + 1 < len(k_blocks) else None
        vtblk = vt_blocks[j]()
        state = [update(s, vtblk, *st) for s, st in zip(s_cur, state)]
        s_cur = s_nxt

    lam = (jnp.exp(jnp.sum(lq1_ref[...] * lk1_ref[...], keepdims=True))
           - jnp.exp(jnp.sum(lq2_ref[...] * lk2_ref[...], keepdims=True)) + LAM_INIT)
    (_, l1, acc1), (_, l2, acc2) = state
    out = (acc1 / l1 - lam * (acc2 / l2)).T
    o_ref[0] = (_head_rms(out, gain_ref[...]) * (1.0 - LAM_INIT)).astype(BF16)


def _diff_attn(p, vt, pc, vtc, lams, gain, tq, tk):
    b, s, _ = p.shape
    lc = pc.shape[1]
    lam_spec = pl.BlockSpec((1, DH), lambda i, h, j: (0, 0))
    return pl.pallas_call(
        functools.partial(_diff_attn_kernel, tk=tk),
        grid=(b, DIFF_HEADS, s // tq),
        in_specs=[lam_spec, lam_spec, lam_spec, lam_spec,
                  pl.BlockSpec((1, tq, LANES), lambda i, h, j: (i, j, h)),
                  pl.BlockSpec((1, s, LANES), lambda i, h, j: (i, 0, 8 + h)),
                  pl.BlockSpec((1, LANES, s), lambda i, h, j: (i, h, 0)),
                  pl.BlockSpec((1, lc, LANES), lambda i, h, j: (i, 0, 8 + h)),
                  pl.BlockSpec((1, LANES, lc), lambda i, h, j: (i, h, 0)),
                  pl.BlockSpec((1, LANES), lambda i, h, j: (0, h))],
        out_specs=pl.BlockSpec((1, tq, LANES), lambda i, h, j: (i, j, h)),
        out_shape=jax.ShapeDtypeStruct((b, s, DIFF_HEADS * LANES), BF16),
        compiler_params=_cparams(("arbitrary", "arbitrary", "arbitrary")),
        name="diff_attn",
    )(*lams, p, p, vt, pc, vtc, gain)


def _swa_attn_kernel(sink_ref, q_ref, k_ref, v_ref, kc_ref, vc_ref, gain_ref, o_ref,
                     kcat_ref, vcat_ref):
    n = pl.program_id(1)
    nb = pl.num_programs(1)
    s_len = k_ref.shape[1]
    lc = kc_ref.shape[1]
    n_band = 3 * BAND

    @pl.when(n == 0)
    def _():
        kcat_ref[n_band:, :] = kc_ref[0]
        vcat_ref[n_band:, :] = vc_ref[0]

    prev = pl.multiple_of(jnp.maximum(n - 1, 0) * BAND, BAND)
    cur = pl.multiple_of(n * BAND, BAND)
    nxt = pl.multiple_of(jnp.minimum(n + 1, nb - 1) * BAND, BAND)
    for t, off in enumerate((prev, cur, nxt)):
        kcat_ref[t * BAND:(t + 1) * BAND, :] = k_ref[0, pl.ds(off, BAND), :]
        vcat_ref[t * BAND:(t + 1) * BAND, :] = v_ref[0, pl.ds(off, BAND), :]

    rows = SWA_GROUP * BAND
    cols = n_band + lc
    qi = lax.broadcasted_iota(I32, (rows, cols), 0) & (BAND - 1)
    ki = lax.broadcasted_iota(I32, (rows, cols), 1)
    kpos = (n - 1) * BAND + ki
    in_band = jnp.where(jnp.abs(qi + BAND - ki) <= BAND,
                        jnp.where(kpos >= 0, jnp.where(kpos < s_len, 1, 0), 0), 0)
    ok = jnp.where(ki >= n_band, 1, in_band) > 0

    q = q_ref[0]
    for g in range(SWA_KV):
        kg = kcat_ref[:, g * DH:(g + 1) * DH]
        vg = vcat_ref[:, g * DH:(g + 1) * DH]
        qg = jnp.concatenate(
            [q[:, (g * SWA_GROUP + j) * DH:(g * SWA_GROUP + j + 1) * DH] for j in range(SWA_GROUP)],
            axis=0)
        s = lax.dot_general(qg, kg, (((1,), (1,)), ((), ())), preferred_element_type=F32)
        s = jnp.where(ok, s, NEG_INF)
        sink = jnp.concatenate(
            [jnp.full((BAND, 1), sink_ref[g * SWA_GROUP + j], F32) for j in range(SWA_GROUP)],
            axis=0)
        m = jnp.maximum(jnp.max(s, axis=-1, keepdims=True), sink)
        e = jnp.exp(s - m)
        denom = jnp.sum(e, axis=-1, keepdims=True) + jnp.exp(sink - m)
        o = jnp.dot(e.astype(BF16), vg, preferred_element_type=F32) / denom
        o = o * lax.rsqrt(jnp.mean(o * o, axis=-1, keepdims=True) + EPS)
        for j in range(SWA_GROUP):
            c0 = (g * SWA_GROUP + j) * DH
            o_ref[0, :, c0:c0 + DH] = (o[j * BAND:(j + 1) * BAND] * gain_ref[:, c0:c0 + DH]).astype(BF16)


def _swa_attn(p, pc, sink, gain):
    b, s, _ = p.shape
    lc = pc.shape[1]
    kvw = SWA_KV * DH
    return pl.pallas_call(
        _swa_attn_kernel,
        grid=(b, s // BAND),
        in_specs=[pl.BlockSpec(memory_space=pltpu.SMEM),
                  pl.BlockSpec((1, BAND, SWA_HEADS * DH), lambda i, j: (i, j, 3)),
                  pl.BlockSpec((1, s, kvw), lambda i, j: (i, 0, 16)),
                  pl.BlockSpec((1, s, kvw), lambda i, j: (i, 0, 17)),
                  pl.BlockSpec((1, lc, kvw), lambda i, j: (i, 0, 16)),
                  pl.BlockSpec((1, lc, kvw), lambda i, j: (i, 0, 17)),
                  pl.BlockSpec((1, SWA_HEADS * DH), lambda i, j: (0, 0))],
        out_specs=pl.BlockSpec((1, BAND, SWA_HEADS * DH), lambda i, j: (i, j, 0)),
        out_shape=jax.ShapeDtypeStruct((b, s, SWA_HEADS * DH), BF16),
        scratch_shapes=[pltpu.VMEM((3 * BAND + lc, kvw), BF16),
                        pltpu.VMEM((3 * BAND + lc, kvw), BF16)],
        compiler_params=_cparams(("arbitrary", "arbitrary")),
        name="swa_attn",
    )(sink, p, p, p, pc, pc, gain)


def _outproj_kernel(ad_ref, as_ref, wd_ref, ws_ref, x_ref, g1_ref, sh_ref, sc_ref, gn_ref,
                    x1_ref, h2_ref):
    a = (jnp.dot(ad_ref[0], wd_ref[...], preferred_element_type=F32)
         + jnp.dot(as_ref[0], ws_ref[...], preferred_element_type=F32))
    x1 = x_ref[0] + g1_ref[0] * a
    x1_ref[0] = x1
    h2_ref[0] = _modnorm(x1, gn_ref[...], sh_ref[0], sc_ref[0]).astype(BF16)


def _outproj(a_diff, a_swa, w_d, w_s, x, g1, sh2, sc2, gain, tm):
    b, s, d = x.shape
    wd = a_diff.shape[2]
    vec = pl.BlockSpec((1, 1, d), lambda i, j: (i, 0, 0))
    return pl.pallas_call(
        _outproj_kernel,
        grid=(b, s // tm),
        in_specs=[pl.BlockSpec((1, tm, wd), lambda i, j: (i, j, 0)),
                  pl.BlockSpec((1, tm, wd), lambda i, j: (i, j, 0)),
                  pl.BlockSpec((wd, d), lambda i, j: (0, 0)),
                  pl.BlockSpec((wd, d), lambda i, j: (0, 0)),
                  pl.BlockSpec((1, tm, d), lambda i, j: (i, j, 0)),
                  vec, vec, vec,
                  pl.BlockSpec((1, d), lambda i, j: (0, 0))],
        out_specs=[pl.BlockSpec((1, tm, d), lambda i, j: (i, j, 0)),
                   pl.BlockSpec((1, tm, d), lambda i, j: (i, j, 0))],
        out_shape=[jax.ShapeDtypeStruct((b, s, d), F32), jax.ShapeDtypeStruct((b, s, d), BF16)],
        compiler_params=_cparams(("arbitrary", "arbitrary")),
        name="out_proj",
    )(a_diff, a_swa, w_d, w_s, x, g1, sh2, sc2, gain)


def _extract_topk(s, payload, k):
    r = s.shape[0]
    row = lax.broadcasted_iota(I32, s.shape, 0).astype(F32)
    vals, pays = [], []
    for _ in range(k):
        m = jnp.max(s, axis=0, keepdims=True)
        first = jnp.min(jnp.where(s == m, row, float(r)), axis=0, keepdims=True)
        hit = row == first
        vals.append(m)
        pays.append(jnp.sum(jnp.where(hit, payload, 0.0), axis=0, keepdims=True))
        s = jnp.where(hit, -jnp.inf, s)
    return jnp.concatenate(vals, axis=0), jnp.concatenate(pays, axis=0)


def _peer_route_kernel(h_ref, wq_ref, keys_ref, idx_ref, gate_ref):
    q = jnp.dot(h_ref[...], wq_ref[...], preferred_element_type=F32).astype(BF16)
    tt = q.shape[0]
    key_id = lax.broadcasted_iota(I32, (N_KEYS, tt), 0).astype(F32)
    for h in range(PEER_HEADS):
        halves = []
        for i in range(2):
            c0 = (h * 2 + i) * N_KEYS
            s = lax.dot_general(keys_ref[h, i], q[:, c0:c0 + N_KEYS], (((1,), (1,)), ((), ())),
                                preferred_element_type=F32)
            halves.append(_extract_topk(s, key_id, PEER_TOPK))
        (s1, i1), (s2, i2) = halves
        cand = jnp.concatenate([s1[a:a + 1] + s2 for a in range(PEER_TOPK)], axis=0)
        cidx = jnp.concatenate([i1[a:a + 1] * N_KEYS + i2 for a in range(PEER_TOPK)], axis=0)
        top_s, top_i = _extract_topk(cand, cidx, PEER_TOPK)
        e = jnp.exp(top_s - jnp.max(top_s, axis=0, keepdims=True))
        r0 = h * PEER_TOPK
        idx_ref[r0:r0 + PEER_TOPK, :] = top_i.astype(I32)
        gate_ref[r0:r0 + PEER_TOPK, :] = e / jnp.sum(e, axis=0, keepdims=True)


def _peer_route(h2, wq, keys, tt):
    t, d = h2.shape
    nq = wq.shape[1]
    return pl.pallas_call(
        _peer_route_kernel,
        grid=(t // tt,),
        in_specs=[pl.BlockSpec((tt, d), lambda i: (i, 0)),
                  pl.BlockSpec((d, nq), lambda i: (0, 0)),
                  pl.BlockSpec(keys.shape, lambda i: (0, 0, 0, 0))],
        out_specs=[pl.BlockSpec((PEER_SEL, tt), lambda i: (0, i)),
                   pl.BlockSpec((PEER_SEL, tt), lambda i: (0, i))],
        out_shape=[jax.ShapeDtypeStruct((PEER_SEL, t), I32), jax.ShapeDtypeStruct((PEER_SEL, t), F32)],
        compiler_params=_cparams(("arbitrary",)),
        name="peer_route",
    )(h2, wq, keys)


def _split_bf16(x):
    hi = x.astype(BF16)
    return hi, (x - hi.astype(F32)).astype(BF16)


def _peer_expert_kernel(idx_cur_ref, idx_nxt_ref, uv_ref, h_ref, gate_ref, grp_ref, grpt_ref,
                        o_ref, buf_ref, sem_ref, *, tb, tg):
    i = pl.program_id(0)
    nsteps = pl.num_programs(0)

    def row_copy(idx_ref, slot, t, j):
        return pltpu.make_async_copy(uv_ref.at[idx_ref[t * PEER_SEL + j]],
                                     buf_ref.at[slot, t, j], sem_ref.at[slot, t])

    def wait_tokens(slot, toks):
        for t in toks:
            for j in range(PEER_SEL):
                row_copy(idx_cur_ref, slot, t, j).wait()

    @pl.when(i == 0)
    def _():
        for t in range(tb):
            for j in range(PEER_SEL):
                row_copy(idx_cur_ref, 0, t, j).start(priority=j % 2)

    diag = (lax.broadcasted_iota(I32, (SLAB, PEER_SEL * SLAB), 1) & (SLAB - 1)) == \
        lax.broadcasted_iota(I32, (SLAB, PEER_SEL * SLAB), 0)

    def consume(slot):
        for g in range(tb // tg):
            toks = range(g * tg, (g + 1) * tg)
            wait_tokens(slot, toks)
            zs = []
            half = PEER_SEL // 2
            for t in toks:
                for j in range(half):
                    row_copy(idx_nxt_ref, 1 - slot, t, j).start(priority=j % 2)
                u2d = buf_ref[slot, t, :, 0:SLAB, :].reshape(PEER_SEL * SLAB, LANES)
                y = lax.dot_general(h_ref[t], u2d, (((1,), (1,)), ((), ())),
                                    preferred_element_type=F32)
                zs.append(jnp.sum(jnp.where(diag, y, 0.0), axis=0, keepdims=True))
            z = jnp.concatenate(zs, axis=0)
            z_hi, z_lo = _split_bf16(z)
            act = (jnp.dot(z_hi, grp_ref[...], preferred_element_type=F32)
                   + jnp.dot(z_lo, grp_ref[...], preferred_element_type=F32))
            gelu = 0.5 * act * (1.0 + jnp.tanh(math.sqrt(2.0 / math.pi)
                                               * (act + 0.044715 * act * act * act)))
            w = gate_ref[g * tg:(g + 1) * tg, :] * gelu
            w_hi, w_lo = _split_bf16(w)
            w_rep = (jnp.dot(w_hi, grpt_ref[...], preferred_element_type=F32)
                     + jnp.dot(w_lo, grpt_ref[...], preferred_element_type=F32))
            for k, t in enumerate(toks):
                for j in range(half, PEER_SEL):
                    row_copy(idx_nxt_ref, 1 - slot, t, j).start(priority=j % 2)
                wt = jnp.where(diag, jnp.broadcast_to(w_rep[k:k + 1], diag.shape), 0.0)
                wt_hi, wt_lo = _split_bf16(wt)
                v2d = buf_ref[slot, t, :, SLAB:2 * SLAB, :].reshape(PEER_SEL * SLAB, LANES)
                r = jnp.dot(jnp.concatenate([wt_hi, wt_lo], axis=0), v2d,
                            preferred_element_type=F32)
                o_ref[t] = r[:SLAB] + r[SLAB:]

        @pl.when(i == nsteps - 1)
        def _():
            wait_tokens(1 - slot, range(tb))

    @pl.when((i & 1) == 0)
    def _():
        consume(0)

    @pl.when((i & 1) == 1)
    def _():
        consume(1)


def _peer_expert(idx_flat, uv, h_slab, gate, tb):
    t = h_slab.shape[0]
    nsteps = t // tb
    cols = lax.broadcasted_iota(I32, (PEER_SEL * SLAB, PEER_SEL), 0) // SLAB
    grp = (cols == lax.broadcasted_iota(I32, (PEER_SEL * SLAB, PEER_SEL), 1)).astype(BF16)
    blk = tb * PEER_SEL
    return pl.pallas_call(
        functools.partial(_peer_expert_kernel, tb=tb, tg=min(8, tb)),
        grid=(nsteps,),
        in_specs=[pl.BlockSpec((blk,), lambda i: (i,), memory_space=pltpu.SMEM),
                  pl.BlockSpec((blk,), lambda i: (jnp.minimum(i + 1, nsteps - 1),),
                               memory_space=pltpu.SMEM),
                  pl.BlockSpec(memory_space=pl.ANY),
                  pl.BlockSpec((tb, SLAB, LANES), lambda i: (i, 0, 0)),
                  pl.BlockSpec((tb, PEER_SEL), lambda i: (i, 0)),
                  pl.BlockSpec(grp.shape, lambda i: (0, 0)),
                  pl.BlockSpec(grp.T.shape, lambda i: (0, 0))],
        out_specs=pl.BlockSpec((tb, SLAB, LANES), lambda i: (i, 0, 0)),
        out_shape=jax.ShapeDtypeStruct((t, SLAB, LANES), F32),
        scratch_shapes=[pltpu.VMEM((2, tb, PEER_SEL, 2 * SLAB, LANES), BF16),
                        pltpu.SemaphoreType.DMA((2, tb))],
        compiler_params=_cparams(("arbitrary",)),
        name="peer_expert",
    )(idx_flat, idx_flat, uv, h_slab, gate, grp, grp.T)


def _final_kernel(x1_ref, pe_ref, g2_ref, gn_ref, o_ref):
    x2 = x1_ref[0] + g2_ref[0] * pe_ref[0]
    o_ref[0] = x2 * lax.rsqrt(jnp.mean(x2 * x2, axis=-1, keepdims=True) + EPS) * gn_ref[...]


def _final(x1, pe, g2, gain, tm):
    b, s, d = x1.shape
    tok = pl.BlockSpec((1, tm, d), lambda i, j: (i, j, 0))
    return pl.pallas_call(
        _final_kernel,
        grid=(b, s // tm),
        in_specs=[tok, tok, pl.BlockSpec((1, 1, d), lambda i, j: (i, 0, 0)),
                  pl.BlockSpec((1, d), lambda i, j: (0, 0))],
        out_specs=tok,
        out_shape=jax.ShapeDtypeStruct((b, s, d), F32),
        compiler_params=_cparams(("arbitrary", "arbitrary")),
        name="final_norm",
    )(x1, pe, g2, gain)


def kernel(x, c, ctx, c_ctx, w_ada, b_ada, norm_attn, w_in, diff_lambda_q1, diff_lambda_k1,
           diff_lambda_q2, diff_lambda_k2, diff_norm, swa_sink, swa_norm, w_out, norm_ffn,
           peer_w_q, peer_sub_keys, peer_u, peer_v, final_norm):
    b, s, d = x.shape
    assert w_ada.shape[0] == 1, "single layer only"
    t = b * s

    cc = jnp.zeros((8, d), F32).at[:b].set(c).at[b].set(c_ctx)
    mod = _adaln(cc, w_ada[0], b_ada[0])
    sh1, sc1, g1, sh2, sc2, g2 = [m[:, None, :] for m in jnp.split(mod, 6, axis=-1)]

    w_in_b = w_in[0].astype(BF16)
    gain_attn = norm_attn[0].reshape(1, d)
    cos_t, sin_t = _rope_tables(s)
    tm = min(512, s)
    dvw = DIFF_HEADS * LANES
    w_vt = w_in_b[:, 2 * dvw:3 * dvw].T
    p, vt = _inproj(x, sh1[:b], sc1[:b], gain_attn, w_in_b, w_vt, cos_t, sin_t, True, tm)
    lc = ctx.shape[1]
    ones = jnp.ones((lc, LANES), F32)
    ctx_sh = jnp.broadcast_to(sh1[b:b + 1], (b, 1, d))
    ctx_sc = jnp.broadcast_to(sc1[b:b + 1], (b, 1, d))
    pc, vtc = _inproj(ctx, ctx_sh, ctx_sc, gain_attn, w_in_b, w_vt, ones, ones, False, lc)

    lams = [v[0].reshape(1, DH).astype(F32) for v in
            (diff_lambda_q1, diff_lambda_k1, diff_lambda_q2, diff_lambda_k2)]
    a_diff = _diff_attn(p, vt, pc, vtc, lams, diff_norm[0].reshape(1, -1), min(256, s), min(512, s))
    a_swa = _swa_attn(p, pc, swa_sink[0].astype(F32), swa_norm[0].reshape(1, -1))

    w_out_b = w_out[0].astype(BF16)
    dw = a_diff.shape[2]
    x1, h2 = _outproj(a_diff, a_swa, w_out_b[:dw], w_out_b[dw:], x, g1[:b], sh2[:b], sc2[:b],
                      norm_ffn[0].reshape(1, d), tm)

    idx_t, gate_t = _peer_route(h2.reshape(t, d), peer_w_q[0].astype(BF16),
                                peer_sub_keys[0].astype(BF16), min(256, t))
    uv = jnp.concatenate([peer_u[0].astype(BF16).reshape(-1, SLAB, LANES),
                          peer_v[0].astype(BF16).reshape(-1, SLAB, LANES)], axis=1)
    pe = _peer_expert(idx_t.T.reshape(-1), uv, h2.reshape(t, SLAB, LANES), gate_t.T, 16)

    return _final(x1, pe.reshape(b, s, d), g2[:b], final_norm.reshape(1, d), tm)
```

```python
import functools
import math

import jax
import jax.numpy as jnp
from jax import lax
from jax.experimental import pallas as pl
from jax.experimental.pallas import tpu as pltpu

F32 = jnp.float32
BF16 = jnp.bfloat16
I32 = jnp.int32

EPS = 1e-6
NEG_INF = -1e30
ROPE_THETA = 10000.0
GRID_W = 64
ROPE_PAIRS = 16

LANES = 128
DH = 64
DIFF_HEADS = 8
SWA_HEADS = 16
SWA_KV = 4
SWA_GROUP = SWA_HEADS // SWA_KV
BAND = 128
LAM_INIT = 0.8 - 0.6 * math.exp(-0.3 * 0)
LOG2E = math.log2(math.e)

PEER_HEADS = 8
PEER_TOPK = 16
N_KEYS = 128
PEER_SEL = PEER_HEADS * PEER_TOPK
SLAB = 16
GATE_LAG = 6

VMEM_LIMIT = 56 * 1024 * 1024


def _cparams(sem):
    return pltpu.CompilerParams(dimension_semantics=sem, vmem_limit_bytes=VMEM_LIMIT)


def _adaln_kernel(c_ref, w_ref, b_ref, o_ref):
    c = c_ref[...]
    s = c * (1.0 / (1.0 + jnp.exp(-c)))
    o_ref[...] = jnp.dot(s, w_ref[...], preferred_element_type=F32,
                         precision=lax.Precision.HIGHEST) + b_ref[...]


def _adaln(cc, w, b):
    rows, d = cc.shape
    n = w.shape[1]
    tn = 1024
    return pl.pallas_call(
        _adaln_kernel,
        grid=(n // tn,),
        in_specs=[pl.BlockSpec((rows, d), lambda j: (0, 0)),
                  pl.BlockSpec((d, tn), lambda j: (0, j)),
                  pl.BlockSpec((1, tn), lambda j: (0, j))],
        out_specs=pl.BlockSpec((rows, tn), lambda j: (0, j)),
        out_shape=jax.ShapeDtypeStruct((rows, n), F32),
        compiler_params=_cparams(("arbitrary",)),
        name="adaln",
    )(cc, w, b.reshape(1, n))


def _modnorm(x, gain, shift, scale):
    y = x * lax.rsqrt(jnp.mean(x * x, axis=-1, keepdims=True) + EPS)
    return (y * gain) * (1.0 + scale) + shift


def _swap16(p):
    lane = lax.broadcasted_iota(I32, p.shape, 1)
    up = pltpu.roll(p, LANES - 16, 1)
    dn = pltpu.roll(p, 16, 1)
    return jnp.where((lane & 31) < 16, up, dn)


def _inproj_kernel(x_ref, sh_ref, sc_ref, g_ref, w_ref, wvt_ref, cos_ref, sin_ref, o_ref, vt_ref, *,
                   rope_chunks, chunk_scale):
    h = _modnorm(x_ref[0], g_ref[...], sh_ref[0], sc_ref[0]).astype(BF16)
    p = jnp.dot(h, w_ref[...], preferred_element_type=F32)
    n_chunks = p.shape[1] // LANES
    if rope_chunks:
        cs = cos_ref[...]
        sn = sin_ref[...]
    for j in range(n_chunks):
        pj = p[:, j * LANES:(j + 1) * LANES]
        if j in rope_chunks:
            pj = pj * cs + _swap16(pj) * sn
        if j in chunk_scale:
            pj = pj * chunk_scale[j]
        o_ref[0, :, j * LANES:(j + 1) * LANES] = pj.astype(BF16)
    vt_ref[0] = lax.dot_general(wvt_ref[...], h, (((1,), (1,)), ((), ())),
                                preferred_element_type=F32).astype(BF16)


def _inproj(x, shift, scale, gain, w, w_vt, cos_t, sin_t, rope, tm):
    b, s, d = x.shape
    n = w.shape[1]
    nv = w_vt.shape[0]
    rope_chunks = frozenset(list(range(0, 16)) + list(range(24, 34))) if rope else frozenset()
    chunk_scale = {j: DH ** -0.5 * LOG2E for j in range(0, 8)}
    chunk_scale.update({j: DH ** -0.5 for j in range(24, 32)})
    kern = functools.partial(_inproj_kernel, rope_chunks=rope_chunks, chunk_scale=chunk_scale)
    return pl.pallas_call(
        kern,
        grid=(b, s // tm),
        in_specs=[pl.BlockSpec((1, tm, d), lambda i, j: (i, j, 0)),
                  pl.BlockSpec((1, 1, d), lambda i, j: (i, 0, 0)),
                  pl.BlockSpec((1, 1, d), lambda i, j: (i, 0, 0)),
                  pl.BlockSpec((1, d), lambda i, j: (0, 0)),
                  pl.BlockSpec((d, n), lambda i, j: (0, 0), pipeline_mode=pl.Buffered(1)),
                  pl.BlockSpec((nv, d), lambda i, j: (0, 0), pipeline_mode=pl.Buffered(1)),
                  pl.BlockSpec((tm, LANES), lambda i, j: (j, 0)),
                  pl.BlockSpec((tm, LANES), lambda i, j: (j, 0))],
        out_specs=[pl.BlockSpec((1, tm, n), lambda i, j: (i, j, 0)),
                   pl.BlockSpec((1, nv, tm), lambda i, j: (i, 0, j))],
        out_shape=[jax.ShapeDtypeStruct((b, s, n), BF16), jax.ShapeDtypeStruct((b, nv, s), BF16)],
        compiler_params=_cparams(("arbitrary", "arbitrary")),
        name="in_proj",
    )(x, shift, scale, gain, w, w_vt, cos_t, sin_t)


def _rope_tables(s):
    rows = s // GRID_W
    row = jnp.repeat(jnp.arange(rows, dtype=F32), GRID_W)
    col = jnp.tile(jnp.arange(GRID_W, dtype=F32), rows)
    freqs = ROPE_THETA ** (-jnp.arange(ROPE_PAIRS, dtype=F32) / ROPE_PAIRS)
    ar = row[:, None] * freqs
    ac = col[:, None] * freqs
    cos64 = jnp.concatenate([jnp.cos(ar), jnp.cos(ar), jnp.cos(ac), jnp.cos(ac)], axis=1)
    sin64 = jnp.concatenate([-jnp.sin(ar), jnp.sin(ar), -jnp.sin(ac), jnp.sin(ac)], axis=1)
    return jnp.tile(cos64, (1, 2)), jnp.tile(sin64, (1, 2))


def _head_rms(o, gain):
    return o * lax.rsqrt(jnp.mean(o * o, axis=-1, keepdims=True) + EPS) * gain


def _diff_attn_kernel(lq1_ref, lk1_ref, lq2_ref, lk2_ref, q_ref, k_ref, vt_ref, kc_ref, vtc_ref,
                      gain_ref, o_ref, *, tk):
    tq = q_ref.shape[1]
    s_len = k_ref.shape[1]
    q = q_ref[0]
    lane = lax.broadcasted_iota(I32, q.shape, 1)
    zero = jnp.zeros_like(q)
    q_maps = (jnp.where(lane < DH, q, zero), jnp.where(lane >= DH, q, zero))
    state = [(jnp.full((1, tq), NEG_INF, F32), jnp.zeros((1, tq), F32),
              jnp.zeros((LANES, tq), F32)) for _ in q_maps]

    def scores(kblk):
        return [lax.dot_general(kblk, qm, (((1,), (1,)), ((), ())), preferred_element_type=F32)
                for qm in q_maps]

    def update(s, vtblk, m, l, acc):
        m_new = jnp.maximum(m, jnp.max(s, axis=0, keepdims=True))
        alpha = jnp.exp2(m - m_new)
        p = jnp.exp2(s - m_new)
        l = alpha * l + jnp.sum(p, axis=0, keepdims=True)
        acc = alpha * acc + jnp.dot(vtblk, p.astype(BF16), preferred_element_type=F32)
        return m_new, l, acc

    k_blocks = [lambda: kc_ref[0]]
    vt_blocks = [lambda: vtc_ref[0]]
    for j in range(s_len // tk):
        k_blocks.append(lambda j=j: k_ref[0, j * tk:(j + 1) * tk, :])
        vt_blocks.append(lambda j=j: vt_ref[0, :, j * tk:(j + 1) * tk])
    s_cur = scores(k_blocks[0]())
    for j in range(len(k_blocks)):
        s_nxt = scores(k_blocks[j + 1]()) if j + 1 < len(k_blocks) else None
        vtblk = vt_blocks[j]()
        state = [update(s, vtblk, *st) for s, st in zip(s_cur, state)]
        s_cur = s_nxt

    lam = (jnp.exp(jnp.sum(lq1_ref[...] * lk1_ref[...], keepdims=True))
           - jnp.exp(jnp.sum(lq2_ref[...] * lk2_ref[...], keepdims=True)) + LAM_INIT)
    (_, l1, acc1), (_, l2, acc2) = state
    out = (acc1 / l1 - lam * (acc2 / l2)).T
    o_ref[0] = (_head_rms(out, gain_ref[...]) * (1.0 - LAM_INIT)).astype(BF16)


def _diff_attn(p, vt, pc, vtc, lams, gain, tq, tk):
    b, s, _ = p.shape
    lc = pc.shape[1]
    lam_spec = pl.BlockSpec((1, DH), lambda i, h, j: (0, 0))
    return pl.pallas_call(
        functools.partial(_diff_attn_kernel, tk=tk),
        grid=(b, DIFF_HEADS, s // tq),
        in_specs=[lam_spec, lam_spec, lam_spec, lam_spec,
                  pl.BlockSpec((1, tq, LANES), lambda i, h, j: (i, j, h)),
                  pl.BlockSpec((1, s, LANES), lambda i, h, j: (i, 0, 8 + h)),
                  pl.BlockSpec((1, LANES, s), lambda i, h, j: (i, h, 0)),
                  pl.BlockSpec((1, lc, LANES), lambda i, h, j: (i, 0, 8 + h)),
                  pl.BlockSpec((1, LANES, lc), lambda i, h, j: (i, h, 0)),
                  pl.BlockSpec((1, LANES), lambda i, h, j: (0, h))],
        out_specs=pl.BlockSpec((1, tq, LANES), lambda i, h, j: (i, j, h)),
        out_shape=jax.ShapeDtypeStruct((b, s, DIFF_HEADS * LANES), BF16),
        compiler_params=_cparams(("arbitrary", "arbitrary", "arbitrary")),
        name="diff_attn",
    )(*lams, p, p, vt, pc, vtc, gain)


def _swa_attn_kernel(sink_ref, q_ref, k_ref, v_ref, kc_ref, vc_ref, gain_ref, o_ref,
                     kcat_ref, vcat_ref):
    n = pl.program_id(1)
    nb = pl.num_programs(1)
    s_len = k_ref.shape[1]
    lc = kc_ref.shape[1]
    n_band = 3 * BAND

    @pl.when(n == 0)
    def _():
        kcat_ref[n_band:, :] = kc_ref[0]
        vcat_ref[n_band:, :] = vc_ref[0]

    prev = pl.multiple_of(jnp.maximum(n - 1, 0) * BAND, BAND)
    cur = pl.multiple_of(n * BAND, BAND)
    nxt = pl.multiple_of(jnp.minimum(n + 1, nb - 1) * BAND, BAND)
    for t, off in enumerate((prev, cur, nxt)):
        kcat_ref[t * BAND:(t + 1) * BAND, :] = k_ref[0, pl.ds(off, BAND), :]
        vcat_ref[t * BAND:(t + 1) * BAND, :] = v_ref[0, pl.ds(off, BAND), :]

    rows = SWA_GROUP * BAND
    cols = n_band + lc
    qi = lax.broadcasted_iota(I32, (rows, cols), 0) & (BAND - 1)
    ki = lax.broadcasted_iota(I32, (rows, cols), 1)
    kpos = (n - 1) * BAND + ki
    in_band = jnp.where(jnp.abs(qi + BAND - ki) <= BAND,
                        jnp.where(kpos >= 0, jnp.where(kpos < s_len, 1, 0), 0), 0)
    ok = jnp.where(ki >= n_band, 1, in_band) > 0

    q = q_ref[0]
    for g in range(SWA_KV):
        kg = kcat_ref[:, g * DH:(g + 1) * DH]
        vg = vcat_ref[:, g * DH:(g + 1) * DH]
        qg = jnp.concatenate(
            [q[:, (g * SWA_GROUP + j) * DH:(g * SWA_GROUP + j + 1) * DH] for j in range(SWA_GROUP)],
            axis=0)
        s = lax.dot_general(qg, kg, (((1,), (1,)), ((), ())), preferred_element_type=F32)
        s = jnp.where(ok, s, NEG_INF)
        sink = jnp.concatenate(
            [jnp.full((BAND, 1), sink_ref[g * SWA_GROUP + j], F32) for j in range(SWA_GROUP)],
            axis=0)
        m = jnp.maximum(jnp.max(s, axis=-1, keepdims=True), sink)
        e = jnp.exp(s - m)
        denom = jnp.sum(e, axis=-1, keepdims=True) + jnp.exp(sink - m)
        o = jnp.dot(e.astype(BF16), vg, preferred_element_type=F32) / denom
        o = o * lax.rsqrt(jnp.mean(o * o, axis=-1, keepdims=True) + EPS)
        for j in range(SWA_GROUP):
            c0 = (g * SWA_GROUP + j) * DH
            o_ref[0, :, c0:c0 + DH] = (o[j * BAND:(j + 1) * BAND] * gain_ref[:, c0:c0 + DH]).astype(BF16)


def _swa_attn(p, pc, sink, gain):
    b, s, _ = p.shape
    lc = pc.shape[1]
    kvw = SWA_KV * DH
    return pl.pallas_call(
        _swa_attn_kernel,
        grid=(b, s // BAND),
        in_specs=[pl.BlockSpec(memory_space=pltpu.SMEM),
                  pl.BlockSpec((1, BAND, SWA_HEADS * DH), lambda i, j: (i, j, 3)),
                  pl.BlockSpec((1, s, kvw), lambda i, j: (i, 0, 16)),
                  pl.BlockSpec((1, s, kvw), lambda i, j: (i, 0, 17)),
                  pl.BlockSpec((1, lc, kvw), lambda i, j: (i, 0, 16)),
                  pl.BlockSpec((1, lc, kvw), lambda i, j: (i, 0, 17)),
                  pl.BlockSpec((1, SWA_HEADS * DH), lambda i, j: (0, 0))],
        out_specs=pl.BlockSpec((1, BAND, SWA_HEADS * DH), lambda i, j: (i, j, 0)),
        out_shape=jax.ShapeDtypeStruct((b, s, SWA_HEADS * DH), BF16),
        scratch_shapes=[pltpu.VMEM((3 * BAND + lc, kvw), BF16),
                        pltpu.VMEM((3 * BAND + lc, kvw), BF16)],
        compiler_params=_cparams(("arbitrary", "arbitrary")),
        name="swa_attn",
    )(sink, p, p, p, pc, pc, gain)


def _outproj_kernel(ad_ref, as_ref, wd_ref, ws_ref, x_ref, g1_ref, sh_ref, sc_ref, gn_ref,
                    x1_ref, h2_ref):
    a = (jnp.dot(ad_ref[0], wd_ref[...], preferred_element_type=F32)
         + jnp.dot(as_ref[0], ws_ref[...], preferred_element_type=F32))
    x1 = x_ref[0] + g1_ref[0] * a
    x1_ref[0] = x1
    h2_ref[0] = _modnorm(x1, gn_ref[...], sh_ref[0], sc_ref[0]).astype(BF16)


def _outproj(a_diff, a_swa, w_d, w_s, x, g1, sh2, sc2, gain, tm):
    b, s, d = x.shape
    wd = a_diff.shape[2]
    vec = pl.BlockSpec((1, 1, d), lambda i, j: (i, 0, 0))
    return pl.pallas_call(
        _outproj_kernel,
        grid=(b, s // tm),
        in_specs=[pl.BlockSpec((1, tm, wd), lambda i, j: (i, j, 0)),
                  pl.BlockSpec((1, tm, wd), lambda i, j: (i, j, 0)),
                  pl.BlockSpec((wd, d), lambda i, j: (0, 0)),
                  pl.BlockSpec((wd, d), lambda i, j: (0, 0)),
                  pl.BlockSpec((1, tm, d), lambda i, j: (i, j, 0)),
                  vec, vec, vec,
                  pl.BlockSpec((1, d), lambda i, j: (0, 0))],
        out_specs=[pl.BlockSpec((1, tm, d), lambda i, j: (i, j, 0)),
                   pl.BlockSpec((1, tm, d), lambda i, j: (i, j, 0))],
        out_shape=[jax.ShapeDtypeStruct((b, s, d), F32), jax.ShapeDtypeStruct((b, s, d), BF16)],
        compiler_params=_cparams(("arbitrary", "arbitrary")),
        name="out_proj",
    )(a_diff, a_swa, w_d, w_s, x, g1, sh2, sc2, gain)


def _extract_topk(s, rank, payload, k):
    vals, pays = [], []
    for _ in range(k):
        m = jnp.max(s, axis=0, keepdims=True)
        first = jnp.min(jnp.where(s == m, rank, jnp.inf), axis=0, keepdims=True)
        hit = rank == first
        vals.append(m)
        pays.append(first if payload is None
                    else jnp.sum(jnp.where(hit, payload, 0.0), axis=0, keepdims=True))
        s = jnp.where(hit, -jnp.inf, s)
    return jnp.concatenate(vals, axis=0), jnp.concatenate(pays, axis=0)


def _pair_candidates(s1, i1, s2, i2):
    k, tt = s1.shape
    assert k == 16, "block layout below is written for 16 x 16 pairs"
    sub = lax.broadcasted_iota(I32, (8, tt), 0)
    vals = [s1[0:1] + s2, s1[1:2] + s2[0:8]]
    pos = [lax.broadcasted_iota(I32, (k, tt), 0), k + sub]
    ids = [i1[0:1] * N_KEYS + i2, i1[1:2] * N_KEYS + i2[0:8]]
    for a in range(2, 8):
        keep = sub < k // (a + 1)
        vals.append(jnp.where(keep, s1[a:a + 1] + s2[0:8], -jnp.inf))
        pos.append(a * k + sub)
        ids.append(i1[a:a + 1] * N_KEYS + i2[0:8])
    vals.append(s1[8:16] + s2[0:1])
    pos.append((sub + 8) * k)
    ids.append(i1[8:16] * N_KEYS + i2[0:1])
    return (jnp.concatenate(vals, axis=0), jnp.concatenate(pos, axis=0).astype(F32),
            jnp.concatenate(ids, axis=0))


def _peer_route_kernel(h_ref, wq_ref, keys_ref, idx_ref, gate_ref):
    q = jnp.dot(h_ref[...], wq_ref[...], preferred_element_type=F32).astype(BF16)
    tt = q.shape[0]
    key_id = lax.broadcasted_iota(I32, (N_KEYS, tt), 0).astype(F32)
    for h in range(PEER_HEADS):
        halves = []
        for i in range(2):
            c0 = (h * 2 + i) * N_KEYS
            s = lax.dot_general(keys_ref[h, i], q[:, c0:c0 + N_KEYS], (((1,), (1,)), ((), ())),
                                preferred_element_type=F32)
            halves.append(_extract_topk(s, key_id, None, PEER_TOPK))
        (s1, i1), (s2, i2) = halves
        cand, pos, cidx = _pair_candidates(s1, i1, s2, i2)
        top_s, top_i = _extract_topk(cand, pos, cidx, PEER_TOPK)
        e = jnp.exp(top_s - jnp.max(top_s, axis=0, keepdims=True))
        r0 = h * PEER_TOPK
        idx_ref[r0:r0 + PEER_TOPK, :] = top_i.astype(I32)
        gate_ref[r0:r0 + PEER_TOPK, :] = e / jnp.sum(e, axis=0, keepdims=True)


def _peer_route(h2, wq, keys, tt):
    t, d = h2.shape
    nq = wq.shape[1]
    return pl.pallas_call(
        _peer_route_kernel,
        grid=(t // tt,),
        in_specs=[pl.BlockSpec((tt, d), lambda i: (i, 0)),
                  pl.BlockSpec((d, nq), lambda i: (0, 0)),
                  pl.BlockSpec(keys.shape, lambda i: (0, 0, 0, 0))],
        out_specs=[pl.BlockSpec((PEER_SEL, tt), lambda i: (0, i)),
                   pl.BlockSpec((PEER_SEL, tt), lambda i: (0, i))],
        out_shape=[jax.ShapeDtypeStruct((PEER_SEL, t), I32), jax.ShapeDtypeStruct((PEER_SEL, t), F32)],
        compiler_params=_cparams(("arbitrary",)),
        name="peer_route",
    )(h2, wq, keys)


def _split_bf16(x):
    hi = x.astype(BF16)
    return hi, (x - hi.astype(F32)).astype(BF16)


def _peer_expert_kernel(idx_cur_ref, idx_nxt_ref, uv_ref, h_ref, gate_ref, seg_ref,
                        o_ref, buf_ref, sem_ref, *, tb, tg):
    i = pl.program_id(0)
    nsteps = pl.num_programs(0)

    def row_copy(idx_ref, slot, t, j):
        return pltpu.make_async_copy(uv_ref.at[idx_ref[t * PEER_SEL + j]],
                                     buf_ref.at[slot, t, j], sem_ref.at[slot, t])

    def wait_tokens(slot, toks):
        for t in toks:
            for j in range(PEER_SEL):
                row_copy(idx_cur_ref, slot, t, j).wait()

    @pl.when(i == 0)
    def _():
        for t in range(tb):
            for j in range(PEER_SEL):
                row_copy(idx_cur_ref, 0, t, j).start(priority=j % 2)

    diag = (lax.broadcasted_iota(I32, (SLAB, PEER_SEL * SLAB), 1) & (SLAB - 1)) == \
        lax.broadcasted_iota(I32, (SLAB, PEER_SEL * SLAB), 0)

    def consume(slot):
        wait_tokens(slot, range(tb))

        def token_pair(t_u, t_v, w_row):
            z_parts, r = [], None
            if t_v is not None:
                wt = jnp.where(diag, jnp.broadcast_to(w_row, diag.shape), 0.0)
                wt_hi, wt_lo = _split_bf16(wt)
                wt2 = jnp.concatenate([wt_hi, wt_lo], axis=0)
            for n in range(PEER_SEL // SLAB):
                e0, c0 = n * SLAB, n * SLAB * SLAB
                if t_u is not None:
                    for j in range(e0, e0 + SLAB // 2):
                        row_copy(idx_nxt_ref, 1 - slot, t_u, j).start(priority=j % 2)
                    u_tile = buf_ref[slot, t_u, e0:e0 + SLAB, 0:SLAB, :].reshape(SLAB * SLAB, LANES)
                    y = lax.dot_general(h_ref[t_u], u_tile, (((1,), (1,)), ((), ())),
                                        preferred_element_type=F32)
                    z_parts.append(jnp.sum(jnp.where(diag[:, :SLAB * SLAB], y, 0.0),
                                           axis=0, keepdims=True))
                    for j in range(e0 + SLAB // 2, e0 + SLAB):
                        row_copy(idx_nxt_ref, 1 - slot, t_u, j).start(priority=j % 2)
                if t_v is not None:
                    v_tile = buf_ref[slot, t_v, e0:e0 + SLAB, SLAB:2 * SLAB, :].reshape(
                        SLAB * SLAB, LANES)
                    part = jnp.dot(wt2[:, c0:c0 + SLAB * SLAB], v_tile, preferred_element_type=F32)
                    r = part if r is None else r + part
            z = jnp.concatenate(z_parts, axis=1) if z_parts else None
            out = None if r is None else r[:SLAB] + r[SLAB:]
            return z, out

        def gate_rows(g, z):
            tile = SLAB * SLAB
            z8 = jnp.concatenate([z, jnp.zeros((8 - tg, z.shape[1]), F32)], axis=0) if tg < 8 else z
            stacked = jnp.concatenate([z8[:, n * tile:(n + 1) * tile]
                                       for n in range(z.shape[1] // tile)], axis=0)
            s_hi, s_lo = _split_bf16(stacked)
            seg = jnp.dot(jnp.concatenate([s_hi, s_lo], axis=0), seg_ref[...],
                          preferred_element_type=F32)
            seg = seg[:stacked.shape[0]] + seg[stacked.shape[0]:]
            act = jnp.concatenate([seg[8 * n:8 * n + tg] for n in range(z.shape[1] // tile)], axis=1)
            gelu = 0.5 * act * (1.0 + jnp.tanh(math.sqrt(2.0 / math.pi)
                                               * (act + 0.044715 * act * act * act)))
            return gate_ref[g * tg:(g + 1) * tg, :] * gelu

        assert GATE_LAG >= tg + 2
        w_reps, outs, zs = {}, {}, []
        for n in range(tb + GATE_LAG):
            t_u = n if n < tb else None
            t_v = n - GATE_LAG if n >= GATE_LAG else None
            w_row = None if t_v is None else w_reps[t_v // tg][t_v % tg:t_v % tg + 1]
            z, out = token_pair(t_u, t_v, w_row)
            if t_u is not None:
                zs.append(z)
            if t_v is not None:
                outs[t_v] = out
            if n % tg == 0 and tg <= n <= tb:
                g = n // tg - 1
                w_reps[g] = gate_rows(g, jnp.concatenate(zs[g * tg:(g + 1) * tg], axis=0))
        for t in range(tb):
            o_ref[t] = outs[t]

        @pl.when(i == nsteps - 1)
        def _():
            wait_tokens(1 - slot, range(tb))

    @pl.when((i & 1) == 0)
    def _():
        consume(0)

    @pl.when((i & 1) == 1)
    def _():
        consume(1)


def _peer_expert(idx_flat, uv, h_slab, gate_rep, tb):
    t = h_slab.shape[0]
    nsteps = t // tb
    tile = SLAB * SLAB
    seg = (lax.broadcasted_iota(I32, (tile, tile), 0) // SLAB
           == lax.broadcasted_iota(I32, (tile, tile), 1) // SLAB).astype(BF16)
    blk = tb * PEER_SEL
    return pl.pallas_call(
        functools.partial(_peer_expert_kernel, tb=tb, tg=min(4, tb)),
        grid=(nsteps,),
        in_specs=[pl.BlockSpec((blk,), lambda i: (i,), memory_space=pltpu.SMEM),
                  pl.BlockSpec((blk,), lambda i: (jnp.minimum(i + 1, nsteps - 1),),
                               memory_space=pltpu.SMEM),
                  pl.BlockSpec(memory_space=pl.ANY),
                  pl.BlockSpec((tb, SLAB, LANES), lambda i: (i, 0, 0)),
                  pl.BlockSpec((tb, PEER_SEL * SLAB), lambda i: (i, 0)),
                  pl.BlockSpec(seg.shape, lambda i: (0, 0))],
        out_specs=pl.BlockSpec((tb, SLAB, LANES), lambda i: (i, 0, 0)),
        out_shape=jax.ShapeDtypeStruct((t, SLAB, LANES), F32),
        scratch_shapes=[pltpu.VMEM((2, tb, PEER_SEL, 2 * SLAB, LANES), BF16),
                        pltpu.SemaphoreType.DMA((2, tb))],
        compiler_params=_cparams(("arbitrary",)),
        name="peer_expert",
    )(idx_flat, idx_flat, uv, h_slab, gate_rep, seg)


def _final_kernel(x1_ref, pe_ref, g2_ref, gn_ref, o_ref):
    x2 = x1_ref[0] + g2_ref[0] * pe_ref[0]
    o_ref[0] = x2 * lax.rsqrt(jnp.mean(x2 * x2, axis=-1, keepdims=True) + EPS) * gn_ref[...]


def _final(x1, pe, g2, gain, tm):
    b, s, d = x1.shape
    tok = pl.BlockSpec((1, tm, d), lambda i, j: (i, j, 0))
    return pl.pallas_call(
        _final_kernel,
        grid=(b, s // tm),
        in_specs=[tok, tok, pl.BlockSpec((1, 1, d), lambda i, j: (i, 0, 0)),
                  pl.BlockSpec((1, d), lambda i, j: (0, 0))],
        out_specs=tok,
        out_shape=jax.ShapeDtypeStruct((b, s, d), F32),
        compiler_params=_cparams(("arbitrary", "arbitrary")),
        name="final_norm",
    )(x1, pe, g2, gain)


def kernel(x, c, ctx, c_ctx, w_ada, b_ada, norm_attn, w_in, diff_lambda_q1, diff_lambda_k1,
           diff_lambda_q2, diff_lambda_k2, diff_norm, swa_sink, swa_norm, w_out, norm_ffn,
           peer_w_q, peer_sub_keys, peer_u, peer_v, final_norm):
    b, s, d = x.shape
    assert w_ada.shape[0] == 1, "single layer only"
    t = b * s

    cc = jnp.zeros((8, d), F32).at[:b].set(c).at[b].set(c_ctx)
    mod = _adaln(cc, w_ada[0], b_ada[0])
    sh1, sc1, g1, sh2, sc2, g2 = [m[:, None, :] for m in jnp.split(mod, 6, axis=-1)]

    w_in_b = w_in[0].astype(BF16)
    gain_attn = norm_attn[0].reshape(1, d)
    cos_t, sin_t = _rope_tables(s)
    tm = min(512, s)
    dvw = DIFF_HEADS * LANES
    w_vt = w_in_b[:, 2 * dvw:3 * dvw].T
    p, vt = _inproj(x, sh1[:b], sc1[:b], gain_attn, w_in_b, w_vt, cos_t, sin_t, True, tm)
    lc = ctx.shape[1]
    ones = jnp.ones((lc, LANES), F32)
    ctx_sh = jnp.broadcast_to(sh1[b:b + 1], (b, 1, d))
    ctx_sc = jnp.broadcast_to(sc1[b:b + 1], (b, 1, d))
    pc, vtc = _inproj(ctx, ctx_sh, ctx_sc, gain_attn, w_in_b, w_vt, ones, ones, False, lc)

    lams = [v[0].reshape(1, DH).astype(F32) for v in
            (diff_lambda_q1, diff_lambda_k1, diff_lambda_q2, diff_lambda_k2)]
    a_diff = _diff_attn(p, vt, pc, vtc, lams, diff_norm[0].reshape(1, -1), min(256, s), min(512, s))
    a_swa = _swa_attn(p, pc, swa_sink[0].astype(F32), swa_norm[0].reshape(1, -1))

    w_out_b = w_out[0].astype(BF16)
    dw = a_diff.shape[2]
    x1, h2 = _outproj(a_diff, a_swa, w_out_b[:dw], w_out_b[dw:], x, g1[:b], sh2[:b], sc2[:b],
                      norm_ffn[0].reshape(1, d), tm)

    idx_t, gate_t = _peer_route(h2.reshape(t, d), peer_w_q[0].astype(BF16),
                                peer_sub_keys[0].astype(BF16), min(256, t))
    uv = jnp.concatenate([peer_u[0].astype(BF16).reshape(-1, SLAB, LANES),
                          peer_v[0].astype(BF16).reshape(-1, SLAB, LANES)], axis=1)
    gate_rep = jnp.repeat(gate_t.T, SLAB, axis=1)
    pe = _peer_expert(idx_t.T.reshape(-1), uv, h2.reshape(t, SLAB, LANES), gate_rep, 16)

    return _final(x1, pe.reshape(b, s, d), g2[:b], final_norm.reshape(1, d), tm)
```

```python
import functools
import math

import jax
import jax.numpy as jnp
from jax import lax
from jax.experimental import pallas as pl
from jax.experimental.pallas import tpu as pltpu

F32 = jnp.float32
BF16 = jnp.bfloat16
I32 = jnp.int32

EPS = 1e-6
NEG_INF = -1e30
ROPE_THETA = 10000.0
GRID_W = 64
ROPE_PAIRS = 16

LANES = 128
DH = 64
DIFF_HEADS = 8
SWA_HEADS = 16
SWA_KV = 4
SWA_GROUP = SWA_HEADS // SWA_KV
BAND = 128
LAM_INIT = 0.8 - 0.6 * math.exp(-0.3 * 0)
LOG2E = math.log2(math.e)

PEER_HEADS = 8
PEER_TOPK = 16
N_KEYS = 128
PEER_SEL = PEER_HEADS * PEER_TOPK
SLAB = 16
PRE_ISSUE = 2
GATE_LAG = 6

VMEM_LIMIT = 56 * 1024 * 1024


def _cparams(sem):
    return pltpu.CompilerParams(dimension_semantics=sem, vmem_limit_bytes=VMEM_LIMIT)


def _adaln_kernel(c_ref, w_ref, b_ref, o_ref):
    c = c_ref[...]
    s = c * (1.0 / (1.0 + jnp.exp(-c)))
    o_ref[...] = jnp.dot(s, w_ref[...], preferred_element_type=F32,
                         precision=lax.Precision.HIGHEST) + b_ref[...]


def _adaln(cc, w, b):
    rows, d = cc.shape
    n = w.shape[1]
    tn = 1024
    return pl.pallas_call(
        _adaln_kernel,
        grid=(n // tn,),
        in_specs=[pl.BlockSpec((rows, d), lambda j: (0, 0)),
                  pl.BlockSpec((d, tn), lambda j: (0, j)),
                  pl.BlockSpec((1, tn), lambda j: (0, j))],
        out_specs=pl.BlockSpec((rows, tn), lambda j: (0, j)),
        out_shape=jax.ShapeDtypeStruct((rows, n), F32),
        compiler_params=_cparams(("arbitrary",)),
        name="adaln",
    )(cc, w, b.reshape(1, n))


def _modnorm(x, gain, shift, scale):
    y = x * lax.rsqrt(jnp.mean(x * x, axis=-1, keepdims=True) + EPS)
    return (y * gain) * (1.0 + scale) + shift


def _swap16(p):
    lane = lax.broadcasted_iota(I32, p.shape, 1)
    up = pltpu.roll(p, LANES - 16, 1)
    dn = pltpu.roll(p, 16, 1)
    return jnp.where((lane & 31) < 16, up, dn)


def _inproj_kernel(x_ref, sh_ref, sc_ref, g_ref, w_ref, wvt_ref, cos_ref, sin_ref, o_ref, vt_ref, *,
                   rope_chunks, chunk_scale):
    h = _modnorm(x_ref[0], g_ref[...], sh_ref[0], sc_ref[0]).astype(BF16)
    p = jnp.dot(h, w_ref[...], preferred_element_type=F32)
    n_chunks = p.shape[1] // LANES
    if rope_chunks:
        cs = cos_ref[...]
        sn = sin_ref[...]
    for j in range(n_chunks):
        pj = p[:, j * LANES:(j + 1) * LANES]
        if j in rope_chunks:
            pj = pj * cs + _swap16(pj) * sn
        if j in chunk_scale:
            pj = pj * chunk_scale[j]
        o_ref[0, :, j * LANES:(j + 1) * LANES] = pj.astype(BF16)
    vt_ref[0] = lax.dot_general(wvt_ref[...], h, (((1,), (1,)), ((), ())),
                                preferred_element_type=F32).astype(BF16)


def _inproj(x, shift, scale, gain, w, w_vt, cos_t, sin_t, rope, tm):
    b, s, d = x.shape
    n = w.shape[1]
    nv = w_vt.shape[0]
    rope_chunks = frozenset(list(range(0, 16)) + list(range(24, 34))) if rope else frozenset()
    chunk_scale = {j: DH ** -0.5 * LOG2E for j in range(0, 8)}
    chunk_scale.update({j: DH ** -0.5 for j in range(24, 32)})
    kern = functools.partial(_inproj_kernel, rope_chunks=rope_chunks, chunk_scale=chunk_scale)
    return pl.pallas_call(
        kern,
        grid=(b, s // tm),
        in_specs=[pl.BlockSpec((1, tm, d), lambda i, j: (i, j, 0)),
                  pl.BlockSpec((1, 1, d), lambda i, j: (i, 0, 0)),
                  pl.BlockSpec((1, 1, d), lambda i, j: (i, 0, 0)),
                  pl.BlockSpec((1, d), lambda i, j: (0, 0)),
                  pl.BlockSpec((d, n), lambda i, j: (0, 0), pipeline_mode=pl.Buffered(1)),
                  pl.BlockSpec((nv, d), lambda i, j: (0, 0), pipeline_mode=pl.Buffered(1)),
                  pl.BlockSpec((tm, LANES), lambda i, j: (j, 0)),
                  pl.BlockSpec((tm, LANES), lambda i, j: (j, 0))],
        out_specs=[pl.BlockSpec((1, tm, n), lambda i, j: (i, j, 0)),
                   pl.BlockSpec((1, nv, tm), lambda i, j: (i, 0, j))],
        out_shape=[jax.ShapeDtypeStruct((b, s, n), BF16), jax.ShapeDtypeStruct((b, nv, s), BF16)],
        compiler_params=_cparams(("arbitrary", "arbitrary")),
        name="in_proj",
    )(x, shift, scale, gain, w, w_vt, cos_t, sin_t)


def _rope_tables(s):
    rows = s // GRID_W
    row = jnp.repeat(jnp.arange(rows, dtype=F32), GRID_W)
    col = jnp.tile(jnp.arange(GRID_W, dtype=F32), rows)
    freqs = ROPE_THETA ** (-jnp.arange(ROPE_PAIRS, dtype=F32) / ROPE_PAIRS)
    ar = row[:, None] * freqs
    ac = col[:, None] * freqs
    cos64 = jnp.concatenate([jnp.cos(ar), jnp.cos(ar), jnp.cos(ac), jnp.cos(ac)], axis=1)
    sin64 = jnp.concatenate([-jnp.sin(ar), jnp.sin(ar), -jnp.sin(ac), jnp.sin(ac)], axis=1)
    return jnp.tile(cos64, (1, 2)), jnp.tile(sin64, (1, 2))


def _head_rms(o, gain):
    return o * lax.rsqrt(jnp.mean(o * o, axis=-1, keepdims=True) + EPS) * gain


def _diff_attn_kernel(lq1_ref, lk1_ref, lq2_ref, lk2_ref, q_ref, k_ref, vt_ref, kc_ref, vtc_ref,
                      gain_ref, o_ref, *, tk):
    tq = q_ref.shape[1]
    s_len = k_ref.shape[1]
    q = q_ref[0]
    lane = lax.broadcasted_iota(I32, q.shape, 1)
    zero = jnp.zeros_like(q)
    q_maps = (jnp.where(lane < DH, q, zero), jnp.where(lane >= DH, q, zero))
    state = [(jnp.full((1, tq), NEG_INF, F32), jnp.zeros((1, tq), F32),
              jnp.zeros((LANES, tq), F32)) for _ in q_maps]

    def scores(kblk):
        return [lax.dot_general(kblk, qm, (((1,), (1,)), ((), ())), preferred_element_type=F32)
                for qm in q_maps]

    def update(s, vtblk, m, l, acc):
        m_new = jnp.maximum(m, jnp.max(s, axis=0, keepdims=True))
        alpha = jnp.exp2(m - m_new)
        p = jnp.exp2(s - m_new)
        l = alpha * l + jnp.sum(p, axis=0, keepdims=True)
        acc = alpha * acc + jnp.dot(vtblk, p.astype(BF16), preferred_element_type=F32)
        return m_new, l, acc

    k_blocks = [lambda: kc_ref[0]]
    vt_blocks = [lambda: vtc_ref[0]]
    for j in range(s_len // tk):
        k_blocks.append(lambda j=j: k_ref[0, j * tk:(j + 1) * tk, :])
        vt_blocks.append(lambda j=j: vt_ref[0, :, j * tk:(j + 1) * tk])
    s_cur = scores(k_blocks[0]())
    for j in range(len(k_blocks)):
        s_nxt = scores(k_blocks[j + 1]()) if j + 1 < len(k_blocks) else None
        vtblk = vt_blocks[j]()
        state = [update(s, vtblk, *st) for s, st in zip(s_cur, state)]
        s_cur = s_nxt

    lam = (jnp.exp(jnp.sum(lq1_ref[...] * lk1_ref[...], keepdims=True))
           - jnp.exp(jnp.sum(lq2_ref[...] * lk2_ref[...], keepdims=True)) + LAM_INIT)
    (_, l1, acc1), (_, l2, acc2) = state
    out = (acc1 / l1 - lam * (acc2 / l2)).T
    o_ref[0] = (_head_rms(out, gain_ref[...]) * (1.0 - LAM_INIT)).astype(BF16)


def _diff_attn(p, vt, pc, vtc, lams, gain, tq, tk):
    b, s, _ = p.shape
    lc = pc.shape[1]
    lam_spec = pl.BlockSpec((1, DH), lambda i, h, j: (0, 0))
    return pl.pallas_call(
        functools.partial(_diff_attn_kernel, tk=tk),
        grid=(b, DIFF_HEADS, s // tq),
        in_specs=[lam_spec, lam_spec, lam_spec, lam_spec,
                  pl.BlockSpec((1, tq, LANES), lambda i, h, j: (i, j, h)),
                  pl.BlockSpec((1, s, LANES), lambda i, h, j: (i, 0, 8 + h)),
                  pl.BlockSpec((1, LANES, s), lambda i, h, j: (i, h, 0)),
                  pl.BlockSpec((1, lc, LANES), lambda i, h, j: (i, 0, 8 + h)),
                  pl.BlockSpec((1, LANES, lc), lambda i, h, j: (i, h, 0)),
                  pl.BlockSpec((1, LANES), lambda i, h, j: (0, h))],
        out_specs=pl.BlockSpec((1, tq, LANES), lambda i, h, j: (i, j, h)),
        out_shape=jax.ShapeDtypeStruct((b, s, DIFF_HEADS * LANES), BF16),
        compiler_params=_cparams(("arbitrary", "arbitrary", "arbitrary")),
        name="diff_attn",
    )(*lams, p, p, vt, pc, vtc, gain)


def _swa_attn_kernel(sink_ref, q_ref, k_ref, v_ref, kc_ref, vc_ref, gain_ref, o_ref,
                     kcat_ref, vcat_ref):
    n = pl.program_id(1)
    nb = pl.num_programs(1)
    s_len = k_ref.shape[1]
    lc = kc_ref.shape[1]
    n_band = 3 * BAND

    @pl.when(n == 0)
    def _():
        kcat_ref[n_band:, :] = kc_ref[0]
        vcat_ref[n_band:, :] = vc_ref[0]

    prev = pl.multiple_of(jnp.maximum(n - 1, 0) * BAND, BAND)
    cur = pl.multiple_of(n * BAND, BAND)
    nxt = pl.multiple_of(jnp.minimum(n + 1, nb - 1) * BAND, BAND)
    for t, off in enumerate((prev, cur, nxt)):
        kcat_ref[t * BAND:(t + 1) * BAND, :] = k_ref[0, pl.ds(off, BAND), :]
        vcat_ref[t * BAND:(t + 1) * BAND, :] = v_ref[0, pl.ds(off, BAND), :]

    rows = SWA_GROUP * BAND
    cols = n_band + lc
    qi = lax.broadcasted_iota(I32, (rows, cols), 0) & (BAND - 1)
    ki = lax.broadcasted_iota(I32, (rows, cols), 1)
    kpos = (n - 1) * BAND + ki
    in_band = jnp.where(jnp.abs(qi + BAND - ki) <= BAND,
                        jnp.where(kpos >= 0, jnp.where(kpos < s_len, 1, 0), 0), 0)
    ok = jnp.where(ki >= n_band, 1, in_band) > 0

    q = q_ref[0]
    for g in range(SWA_KV):
        kg = kcat_ref[:, g * DH:(g + 1) * DH]
        vg = vcat_ref[:, g * DH:(g + 1) * DH]
        qg = jnp.concatenate(
            [q[:, (g * SWA_GROUP + j) * DH:(g * SWA_GROUP + j + 1) * DH] for j in range(SWA_GROUP)],
            axis=0)
        s = lax.dot_general(qg, kg, (((1,), (1,)), ((), ())), preferred_element_type=F32)
        s = jnp.where(ok, s, NEG_INF)
        sink = jnp.concatenate(
            [jnp.full((BAND, 1), sink_ref[g * SWA_GROUP + j], F32) for j in range(SWA_GROUP)],
            axis=0)
        m = jnp.maximum(jnp.max(s, axis=-1, keepdims=True), sink)
        e = jnp.exp(s - m)
        denom = jnp.sum(e, axis=-1, keepdims=True) + jnp.exp(sink - m)
        o = jnp.dot(e.astype(BF16), vg, preferred_element_type=F32) / denom
        o = o * lax.rsqrt(jnp.mean(o * o, axis=-1, keepdims=True) + EPS)
        for j in range(SWA_GROUP):
            c0 = (g * SWA_GROUP + j) * DH
            o_ref[0, :, c0:c0 + DH] = (o[j * BAND:(j + 1) * BAND] * gain_ref[:, c0:c0 + DH]).astype(BF16)


def _swa_attn(p, pc, sink, gain):
    b, s, _ = p.shape
    lc = pc.shape[1]
    kvw = SWA_KV * DH
    return pl.pallas_call(
        _swa_attn_kernel,
        grid=(b, s // BAND),
        in_specs=[pl.BlockSpec(memory_space=pltpu.SMEM),
                  pl.BlockSpec((1, BAND, SWA_HEADS * DH), lambda i, j: (i, j, 3)),
                  pl.BlockSpec((1, s, kvw), lambda i, j: (i, 0, 16)),
                  pl.BlockSpec((1, s, kvw), lambda i, j: (i, 0, 17)),
                  pl.BlockSpec((1, lc, kvw), lambda i, j: (i, 0, 16)),
                  pl.BlockSpec((1, lc, kvw), lambda i, j: (i, 0, 17)),
                  pl.BlockSpec((1, SWA_HEADS * DH), lambda i, j: (0, 0))],
        out_specs=pl.BlockSpec((1, BAND, SWA_HEADS * DH), lambda i, j: (i, j, 0)),
        out_shape=jax.ShapeDtypeStruct((b, s, SWA_HEADS * DH), BF16),
        scratch_shapes=[pltpu.VMEM((3 * BAND + lc, kvw), BF16),
                        pltpu.VMEM((3 * BAND + lc, kvw), BF16)],
        compiler_params=_cparams(("arbitrary", "arbitrary")),
        name="swa_attn",
    )(sink, p, p, p, pc, pc, gain)


def _outproj_kernel(ad_ref, as_ref, wd_ref, ws_ref, x_ref, g1_ref, sh_ref, sc_ref, gn_ref,
                    x1_ref, h2_ref):
    a = (jnp.dot(ad_ref[0], wd_ref[...], preferred_element_type=F32)
         + jnp.dot(as_ref[0], ws_ref[...], preferred_element_type=F32))
    x1 = x_ref[0] + g1_ref[0] * a
    x1_ref[0] = x1
    h2_ref[0] = _modnorm(x1, gn_ref[...], sh_ref[0], sc_ref[0]).astype(BF16)


def _outproj(a_diff, a_swa, w_d, w_s, x, g1, sh2, sc2, gain, tm):
    b, s, d = x.shape
    wd = a_diff.shape[2]
    vec = pl.BlockSpec((1, 1, d), lambda i, j: (i, 0, 0))
    return pl.pallas_call(
        _outproj_kernel,
        grid=(b, s // tm),
        in_specs=[pl.BlockSpec((1, tm, wd), lambda i, j: (i, j, 0)),
                  pl.BlockSpec((1, tm, wd), lambda i, j: (i, j, 0)),
                  pl.BlockSpec((wd, d), lambda i, j: (0, 0)),
                  pl.BlockSpec((wd, d), lambda i, j: (0, 0)),
                  pl.BlockSpec((1, tm, d), lambda i, j: (i, j, 0)),
                  vec, vec, vec,
                  pl.BlockSpec((1, d), lambda i, j: (0, 0))],
        out_specs=[pl.BlockSpec((1, tm, d), lambda i, j: (i, j, 0)),
                   pl.BlockSpec((1, tm, d), lambda i, j: (i, j, 0))],
        out_shape=[jax.ShapeDtypeStruct((b, s, d), F32), jax.ShapeDtypeStruct((b, s, d), BF16)],
        compiler_params=_cparams(("arbitrary", "arbitrary")),
        name="out_proj",
    )(a_diff, a_swa, w_d, w_s, x, g1, sh2, sc2, gain)


def _extract_topk(s, rank, payload, k):
    vals, pays = [], []
    for _ in range(k):
        m = jnp.max(s, axis=0, keepdims=True)
        first = jnp.min(jnp.where(s == m, rank, jnp.inf), axis=0, keepdims=True)
        hit = rank == first
        vals.append(m)
        pays.append(first if payload is None
                    else jnp.sum(jnp.where(hit, payload, 0.0), axis=0, keepdims=True))
        s = jnp.where(hit, -jnp.inf, s)
    return jnp.concatenate(vals, axis=0), jnp.concatenate(pays, axis=0)


def _pair_candidates(s1, i1, s2, i2):
    k, tt = s1.shape
    assert k == 16, "block layout below is written for 16 x 16 pairs"
    sub = lax.broadcasted_iota(I32, (8, tt), 0)
    vals = [s1[0:1] + s2, s1[1:2] + s2[0:8]]
    pos = [lax.broadcasted_iota(I32, (k, tt), 0), k + sub]
    ids = [i1[0:1] * N_KEYS + i2, i1[1:2] * N_KEYS + i2[0:8]]
    for a in range(2, 8):
        keep = sub < k // (a + 1)
        vals.append(jnp.where(keep, s1[a:a + 1] + s2[0:8], -jnp.inf))
        pos.append(a * k + sub)
        ids.append(i1[a:a + 1] * N_KEYS + i2[0:8])
    vals.append(s1[8:16] + s2[0:1])
    pos.append((sub + 8) * k)
    ids.append(i1[8:16] * N_KEYS + i2[0:1])
    return (jnp.concatenate(vals, axis=0), jnp.concatenate(pos, axis=0).astype(F32),
            jnp.concatenate(ids, axis=0))


def _peer_route_kernel(h_ref, wq_ref, keys_ref, idx_ref, gate_ref):
    q = jnp.dot(h_ref[...], wq_ref[...], preferred_element_type=F32).astype(BF16)
    tt = q.shape[0]
    key_id = lax.broadcasted_iota(I32, (N_KEYS, tt), 0).astype(F32)
    for h in range(PEER_HEADS):
        halves = []
        for i in range(2):
            c0 = (h * 2 + i) * N_KEYS
            s = lax.dot_general(keys_ref[h, i], q[:, c0:c0 + N_KEYS], (((1,), (1,)), ((), ())),
                                preferred_element_type=F32)
            halves.append(_extract_topk(s, key_id, None, PEER_TOPK))
        (s1, i1), (s2, i2) = halves
        cand, pos, cidx = _pair_candidates(s1, i1, s2, i2)
        top_s, top_i = _extract_topk(cand, pos, cidx, PEER_TOPK)
        e = jnp.exp(top_s - jnp.max(top_s, axis=0, keepdims=True))
        r0 = h * PEER_TOPK
        idx_ref[r0:r0 + PEER_TOPK, :] = top_i.astype(I32)
        gate_ref[r0:r0 + PEER_TOPK, :] = e / jnp.sum(e, axis=0, keepdims=True)


def _peer_route(h2, wq, keys, tt):
    t, d = h2.shape
    nq = wq.shape[1]
    return pl.pallas_call(
        _peer_route_kernel,
        grid=(t // tt,),
        in_specs=[pl.BlockSpec((tt, d), lambda i: (i, 0)),
                  pl.BlockSpec((d, nq), lambda i: (0, 0)),
                  pl.BlockSpec(keys.shape, lambda i: (0, 0, 0, 0))],
        out_specs=[pl.BlockSpec((PEER_SEL, tt), lambda i: (0, i)),
                   pl.BlockSpec((PEER_SEL, tt), lambda i: (0, i))],
        out_shape=[jax.ShapeDtypeStruct((PEER_SEL, t), I32), jax.ShapeDtypeStruct((PEER_SEL, t), F32)],
        compiler_params=_cparams(("arbitrary",)),
        name="peer_route",
    )(h2, wq, keys)


def _split_bf16(x):
    hi = x.astype(BF16)
    return hi, (x - hi.astype(F32)).astype(BF16)


def _peer_expert_kernel(idx_cur_ref, idx_nxt_ref, uv_ref, h_ref, gate_ref, seg_ref, grpt_ref,
                        o_ref, buf_ref, sem_ref, *, tb, tg):
    i = pl.program_id(0)
    nsteps = pl.num_programs(0)

    def row_copy(idx_ref, slot, t, j):
        return pltpu.make_async_copy(uv_ref.at[idx_ref[t * PEER_SEL + j]],
                                     buf_ref.at[slot, t, j], sem_ref.at[slot, t])

    def wait_tokens(slot, toks):
        for t in toks:
            for j in range(PEER_SEL):
                row_copy(idx_cur_ref, slot, t, j).wait()

    @pl.when(i == 0)
    def _():
        for t in range(tb):
            for j in range(PEER_SEL):
                row_copy(idx_cur_ref, 0, t, j).start(priority=j % 2)

    diag = (lax.broadcasted_iota(I32, (SLAB, PEER_SEL * SLAB), 1) & (SLAB - 1)) == \
        lax.broadcasted_iota(I32, (SLAB, PEER_SEL * SLAB), 0)

    def consume(slot):
        for t in range(PRE_ISSUE):
            for j in range(PEER_SEL):
                row_copy(idx_nxt_ref, 1 - slot, t, j).start(priority=j % 2)
        wait_tokens(slot, range(tb))
        g_hi, g_lo = _split_bf16(gate_ref[...])
        gate_rep = (jnp.dot(g_hi, grpt_ref[...], preferred_element_type=F32)
                    + jnp.dot(g_lo, grpt_ref[...], preferred_element_type=F32))

        def token_pair(t_u, t_v, w_row):
            z_parts, r = [], None
            if t_v is not None:
                wt = jnp.where(diag, jnp.broadcast_to(w_row, diag.shape), 0.0)
                wt_hi, wt_lo = _split_bf16(wt)
                wt2 = jnp.concatenate([wt_hi, wt_lo], axis=0)
            for n in range(PEER_SEL // SLAB):
                e0, c0 = n * SLAB, n * SLAB * SLAB
                if t_u is not None:
                    for j in range(e0, e0 + SLAB // 2) if t_u >= PRE_ISSUE else ():
                        row_copy(idx_nxt_ref, 1 - slot, t_u, j).start(priority=j % 2)
                    u_tile = buf_ref[slot, t_u, e0:e0 + SLAB, 0:SLAB, :].reshape(SLAB * SLAB, LANES)
                    y = lax.dot_general(h_ref[t_u], u_tile, (((1,), (1,)), ((), ())),
                                        preferred_element_type=F32)
                    z_parts.append(jnp.sum(jnp.where(diag[:, :SLAB * SLAB], y, 0.0),
                                           axis=0, keepdims=True))
                    for j in range(e0 + SLAB // 2, e0 + SLAB) if t_u >= PRE_ISSUE else ():
                        row_copy(idx_nxt_ref, 1 - slot, t_u, j).start(priority=j % 2)
                if t_v is not None:
                    v_tile = buf_ref[slot, t_v, e0:e0 + SLAB, SLAB:2 * SLAB, :].reshape(
                        SLAB * SLAB, LANES)
                    part = jnp.dot(wt2[:, c0:c0 + SLAB * SLAB], v_tile, preferred_element_type=F32)
                    r = part if r is None else r + part
            z = jnp.concatenate(z_parts, axis=1) if z_parts else None
            out = None if r is None else r[:SLAB] + r[SLAB:]
            return z, out

        def gate_rows(g, z):
            tile = SLAB * SLAB
            z8 = jnp.concatenate([z, jnp.zeros((8 - tg, z.shape[1]), F32)], axis=0) if tg < 8 else z
            stacked = jnp.concatenate([z8[:, n * tile:(n + 1) * tile]
                                       for n in range(z.shape[1] // tile)], axis=0)
            s_hi, s_lo = _split_bf16(stacked)
            seg = jnp.dot(jnp.concatenate([s_hi, s_lo], axis=0), seg_ref[...],
                          preferred_element_type=F32)
            seg = seg[:stacked.shape[0]] + seg[stacked.shape[0]:]
            act = jnp.concatenate([seg[8 * n:8 * n + tg] for n in range(z.shape[1] // tile)], axis=1)
            gelu = 0.5 * act * (1.0 + jnp.tanh(math.sqrt(2.0 / math.pi)
                                               * (act + 0.044715 * act * act * act)))
            return gate_rep[g * tg:(g + 1) * tg, :] * gelu

        assert GATE_LAG >= tg + 2
        w_reps, outs, zs = {}, {}, []
        for n in range(tb + GATE_LAG):
            t_u = n if n < tb else None
            t_v = n - GATE_LAG if n >= GATE_LAG else None
            w_row = None if t_v is None else w_reps[t_v // tg][t_v % tg:t_v % tg + 1]
            z, out = token_pair(t_u, t_v, w_row)
            if t_u is not None:
                zs.append(z)
            if t_v is not None:
                outs[t_v] = out
            if n % tg == 0 and tg <= n <= tb:
                g = n // tg - 1
                w_reps[g] = gate_rows(g, jnp.concatenate(zs[g * tg:(g + 1) * tg], axis=0))
        for t in range(tb):
            for r in range(SLAB):
                o_ref[t:t + 1, r * LANES:(r + 1) * LANES] = outs[t][r:r + 1, :]

        @pl.when(i == nsteps - 1)
        def _():
            wait_tokens(1 - slot, range(tb))

    @pl.when((i & 1) == 0)
    def _():
        consume(0)

    @pl.when((i & 1) == 1)
    def _():
        consume(1)


def _peer_expert(idx_flat, uv, h_slab, gate, tb):
    t = h_slab.shape[0]
    nsteps = t // tb
    tile = SLAB * SLAB
    seg = (lax.broadcasted_iota(I32, (tile, tile), 0) // SLAB
           == lax.broadcasted_iota(I32, (tile, tile), 1) // SLAB).astype(BF16)
    grpt = (lax.broadcasted_iota(I32, (PEER_SEL, PEER_SEL * SLAB), 0)
            == lax.broadcasted_iota(I32, (PEER_SEL, PEER_SEL * SLAB), 1) // SLAB).astype(BF16)
    blk = tb * PEER_SEL
    return pl.pallas_call(
        functools.partial(_peer_expert_kernel, tb=tb, tg=min(4, tb)),
        grid=(nsteps,),
        in_specs=[pl.BlockSpec((blk,), lambda i: (i,), memory_space=pltpu.SMEM),
                  pl.BlockSpec((blk,), lambda i: (jnp.minimum(i + 1, nsteps - 1),),
                               memory_space=pltpu.SMEM),
                  pl.BlockSpec(memory_space=pl.ANY),
                  pl.BlockSpec((tb, SLAB, LANES), lambda i: (i, 0, 0)),
                  pl.BlockSpec((tb, PEER_SEL), lambda i: (i, 0)),
                  pl.BlockSpec(seg.shape, lambda i: (0, 0)),
                  pl.BlockSpec(grpt.shape, lambda i: (0, 0))],
        out_specs=pl.BlockSpec((tb, SLAB * LANES), lambda i: (i, 0)),
        out_shape=jax.ShapeDtypeStruct((t, SLAB * LANES), F32),
        scratch_shapes=[pltpu.VMEM((2, tb, PEER_SEL, 2 * SLAB, LANES), BF16),
                        pltpu.SemaphoreType.DMA((2, tb))],
        compiler_params=_cparams(("arbitrary",)),
        name="peer_expert",
    )(idx_flat, idx_flat, uv, h_slab, gate, seg, grpt)


def _final_kernel(x1_ref, pe_ref, g2_ref, gn_ref, o_ref):
    x2 = x1_ref[0] + g2_ref[0] * pe_ref[0]
    o_ref[0] = x2 * lax.rsqrt(jnp.mean(x2 * x2, axis=-1, keepdims=True) + EPS) * gn_ref[...]


def _final(x1, pe, g2, gain, tm):
    b, s, d = x1.shape
    tok = pl.BlockSpec((1, tm, d), lambda i, j: (i, j, 0))
    return pl.pallas_call(
        _final_kernel,
        grid=(b, s // tm),
        in_specs=[tok, tok, pl.BlockSpec((1, 1, d), lambda i, j: (i, 0, 0)),
                  pl.BlockSpec((1, d), lambda i, j: (0, 0))],
        out_specs=tok,
        out_shape=jax.ShapeDtypeStruct((b, s, d), F32),
        compiler_params=_cparams(("arbitrary", "arbitrary")),
        name="final_norm",
    )(x1, pe, g2, gain)


def kernel(x, c, ctx, c_ctx, w_ada, b_ada, norm_attn, w_in, diff_lambda_q1, diff_lambda_k1,
           diff_lambda_q2, diff_lambda_k2, diff_norm, swa_sink, swa_norm, w_out, norm_ffn,
           peer_w_q, peer_sub_keys, peer_u, peer_v, final_norm):
    b, s, d = x.shape
    assert w_ada.shape[0] == 1, "single layer only"
    t = b * s

    cc = jnp.zeros((8, d), F32).at[:b].set(c).at[b].set(c_ctx)
    mod = _adaln(cc, w_ada[0], b_ada[0])
    sh1, sc1, g1, sh2, sc2, g2 = [m[:, None, :] for m in jnp.split(mod, 6, axis=-1)]

    w_in_b = w_in[0].astype(BF16)
    gain_attn = norm_attn[0].reshape(1, d)
    cos_t, sin_t = _rope_tables(s)
    tm = min(512, s)
    dvw = DIFF_HEADS * LANES
    w_vt = w_in_b[:, 2 * dvw:3 * dvw].T
    p, vt = _inproj(x, sh1[:b], sc1[:b], gain_attn, w_in_b, w_vt, cos_t, sin_t, True, tm)
    lc = ctx.shape[1]
    ones = jnp.ones((lc, LANES), F32)
    ctx_sh = jnp.broadcast_to(sh1[b:b + 1], (b, 1, d))
    ctx_sc = jnp.broadcast_to(sc1[b:b + 1], (b, 1, d))
    pc, vtc = _inproj(ctx, ctx_sh, ctx_sc, gain_attn, w_in_b, w_vt, ones, ones, False, lc)

    lams = [v[0].reshape(1, DH).astype(F32) for v in
            (diff_lambda_q1, diff_lambda_k1, diff_lambda_q2, diff_lambda_k2)]
    a_diff = _diff_attn(p, vt, pc, vtc, lams, diff_norm[0].reshape(1, -1), min(512, s), min(512, s))
    a_swa = _swa_attn(p, pc, swa_sink[0].astype(F32), swa_norm[0].reshape(1, -1))

    w_out_b = w_out[0].astype(BF16)
    dw = a_diff.shape[2]
    x1, h2 = _outproj(a_diff, a_swa, w_out_b[:dw], w_out_b[dw:], x, g1[:b], sh2[:b], sc2[:b],
                      norm_ffn[0].reshape(1, d), tm)

    idx_t, gate_t = _peer_route(h2.reshape(t, d), peer_w_q[0].astype(BF16),
                                peer_sub_keys[0].astype(BF16), min(256, t))
    uv = jnp.concatenate([peer_u[0].astype(BF16).reshape(-1, SLAB, LANES),
                          peer_v[0].astype(BF16).reshape(-1, SLAB, LANES)], axis=1)
    pe = _peer_expert(idx_t.T.reshape(-1), uv, h2.reshape(t, SLAB, LANES), gate_t.T, 16)

    return _final(x1, pe.reshape(b, s, d), g2[:b], final_norm.reshape(1, d), tm)
```

```python
import functools
import math

import jax
import jax.numpy as jnp
from jax import lax
from jax.experimental import pallas as pl
from jax.experimental.pallas import tpu as pltpu

F32 = jnp.float32
BF16 = jnp.bfloat16
I32 = jnp.int32

EPS = 1e-6
NEG_INF = -1e30
ROPE_THETA = 10000.0
GRID_W = 64
ROPE_PAIRS = 16

LANES = 128
DH = 64
DIFF_HEADS = 8
SWA_HEADS = 16
SWA_KV = 4
SWA_GROUP = SWA_HEADS // SWA_KV
BAND = 128
LAM_INIT = 0.8 - 0.6 * math.exp(-0.3 * 0)
LOG2E = math.log2(math.e)

PEER_HEADS = 8
PEER_TOPK = 16
N_KEYS = 128
PEER_SEL = PEER_HEADS * PEER_TOPK
SLAB = 16
PRE_ISSUE = 2
GATE_LAG = 6

VMEM_LIMIT = 56 * 1024 * 1024


def _cparams(sem):
    return pltpu.CompilerParams(dimension_semantics=sem, vmem_limit_bytes=VMEM_LIMIT)


def _adaln_kernel(c_ref, w_ref, b_ref, o_ref):
    c = c_ref[...]
    s = c * (1.0 / (1.0 + jnp.exp(-c)))
    o_ref[...] = jnp.dot(s, w_ref[...], preferred_element_type=F32,
                         precision=lax.Precision.HIGHEST) + b_ref[...]


def _adaln(cc, w, b):
    rows, d = cc.shape
    n = w.shape[1]
    tn = 1024
    return pl.pallas_call(
        _adaln_kernel,
        grid=(n // tn,),
        in_specs=[pl.BlockSpec((rows, d), lambda j: (0, 0)),
                  pl.BlockSpec((d, tn), lambda j: (0, j)),
                  pl.BlockSpec((1, tn), lambda j: (0, j))],
        out_specs=pl.BlockSpec((rows, tn), lambda j: (0, j)),
        out_shape=jax.ShapeDtypeStruct((rows, n), F32),
        compiler_params=_cparams(("arbitrary",)),
        name="adaln",
    )(cc, w, b.reshape(1, n))


def _modnorm(x, gain, shift, scale):
    y = x * lax.rsqrt(jnp.mean(x * x, axis=-1, keepdims=True) + EPS)
    return (y * gain) * (1.0 + scale) + shift


def _swap16(p):
    lane = lax.broadcasted_iota(I32, p.shape, 1)
    up = pltpu.roll(p, LANES - 16, 1)
    dn = pltpu.roll(p, 16, 1)
    return jnp.where((lane & 31) < 16, up, dn)


def _inproj_kernel(x_ref, sh_ref, sc_ref, g_ref, w_ref, wvt_ref, cos_ref, sin_ref, o_ref, vt_ref, *,
                   rope_chunks, chunk_scale):
    h = _modnorm(x_ref[0], g_ref[...], sh_ref[0], sc_ref[0]).astype(BF16)
    p = jnp.dot(h, w_ref[...], preferred_element_type=F32)
    n_chunks = p.shape[1] // LANES
    if rope_chunks:
        cs = cos_ref[...]
        sn = sin_ref[...]
    for j in range(n_chunks):
        pj = p[:, j * LANES:(j + 1) * LANES]
        if j in rope_chunks:
            pj = pj * cs + _swap16(pj) * sn
        if j in chunk_scale:
            pj = pj * chunk_scale[j]
        o_ref[0, :, j * LANES:(j + 1) * LANES] = pj.astype(BF16)
    vt_ref[0] = lax.dot_general(wvt_ref[...], h, (((1,), (1,)), ((), ())),
                                preferred_element_type=F32).astype(BF16)


def _inproj(x, shift, scale, gain, w, w_vt, cos_t, sin_t, rope, tm):
    b, s, d = x.shape
    n = w.shape[1]
    nv = w_vt.shape[0]
    rope_chunks = frozenset(list(range(0, 16)) + list(range(24, 34))) if rope else frozenset()
    chunk_scale = {j: DH ** -0.5 * LOG2E for j in range(0, 8)}
    chunk_scale.update({j: DH ** -0.5 for j in range(24, 32)})
    kern = functools.partial(_inproj_kernel, rope_chunks=rope_chunks, chunk_scale=chunk_scale)
    return pl.pallas_call(
        kern,
        grid=(b, s // tm),
        in_specs=[pl.BlockSpec((1, tm, d), lambda i, j: (i, j, 0)),
                  pl.BlockSpec((1, 1, d), lambda i, j: (i, 0, 0)),
                  pl.BlockSpec((1, 1, d), lambda i, j: (i, 0, 0)),
                  pl.BlockSpec((1, d), lambda i, j: (0, 0)),
                  pl.BlockSpec((d, n), lambda i, j: (0, 0), pipeline_mode=pl.Buffered(1)),
                  pl.BlockSpec((nv, d), lambda i, j: (0, 0), pipeline_mode=pl.Buffered(1)),
                  pl.BlockSpec((tm, LANES), lambda i, j: (j, 0)),
                  pl.BlockSpec((tm, LANES), lambda i, j: (j, 0))],
        out_specs=[pl.BlockSpec((1, tm, n), lambda i, j: (i, j, 0)),
                   pl.BlockSpec((1, nv, tm), lambda i, j: (i, 0, j))],
        out_shape=[jax.ShapeDtypeStruct((b, s, n), BF16), jax.ShapeDtypeStruct((b, nv, s), BF16)],
        compiler_params=_cparams(("arbitrary", "arbitrary")),
        name="in_proj",
    )(x, shift, scale, gain, w, w_vt, cos_t, sin_t)


def _rope_tables(s):
    rows = s // GRID_W
    row = jnp.repeat(jnp.arange(rows, dtype=F32), GRID_W)
    col = jnp.tile(jnp.arange(GRID_W, dtype=F32), rows)
    freqs = ROPE_THETA ** (-jnp.arange(ROPE_PAIRS, dtype=F32) / ROPE_PAIRS)
    ar = row[:, None] * freqs
    ac = col[:, None] * freqs
    cos64 = jnp.concatenate([jnp.cos(ar), jnp.cos(ar), jnp.cos(ac), jnp.cos(ac)], axis=1)
    sin64 = jnp.concatenate([-jnp.sin(ar), jnp.sin(ar), -jnp.sin(ac), jnp.sin(ac)], axis=1)
    return jnp.tile(cos64, (1, 2)), jnp.tile(sin64, (1, 2))


def _head_rms(o, gain):
    return o * lax.rsqrt(jnp.mean(o * o, axis=-1, keepdims=True) + EPS) * gain


def _diff_attn_kernel(lq1_ref, lk1_ref, lq2_ref, lk2_ref, q_ref, k_ref, vt_ref, kc_ref, vtc_ref,
                      gain_ref, o_ref, s_ref, p_ref, *, tk):
    tq = q_ref.shape[1]
    s_len = k_ref.shape[1]
    q = q_ref[0]
    lane = lax.broadcasted_iota(I32, q.shape, 1)
    zero = jnp.zeros_like(q)
    q_maps = (jnp.where(lane < DH, q, zero), jnp.where(lane >= DH, q, zero))
    lc = kc_ref.shape[1]
    blocks = [(0, lc, lambda: kc_ref[0])]
    blocks += [(lc + j * tk, tk, lambda j=j: k_ref[0, j * tk:(j + 1) * tk, :])
               for j in range(s_len // tk)]

    def score_block(i, blk, m):
        r0, rows, kblk = blk
        s = lax.dot_general(kblk(), q_maps[i], (((1,), (1,)), ((), ())),
                            preferred_element_type=F32)
        s_ref[i, r0:r0 + rows, :] = s
        return jnp.maximum(m, jnp.max(s, axis=0, keepdims=True))

    def prob_block(i, blk, m, l):
        r0, rows, _ = blk
        p = jnp.exp2(s_ref[i, r0:r0 + rows, :] - m)
        p_ref[i, r0:r0 + rows, :] = p.astype(BF16)
        return l + jnp.sum(p, axis=0, keepdims=True)

    def values(i):
        return (jnp.dot(vtc_ref[0], p_ref[i, 0:lc, :], preferred_element_type=F32)
                + jnp.dot(vt_ref[0], p_ref[i, lc:, :], preferred_element_type=F32))

    m_init = jnp.full((1, tq), NEG_INF, F32)
    l_init = jnp.zeros((1, tq), F32)
    m1 = m_init
    for blk in blocks:
        m1 = score_block(0, blk, m1)
    m2, l1 = m_init, l_init
    for blk in blocks:
        m2 = score_block(1, blk, m2)
        l1 = prob_block(0, blk, m1, l1)
    acc1 = values(0)
    l2 = l_init
    for blk in blocks:
        l2 = prob_block(1, blk, m2, l2)
    acc2 = values(1)

    lam = (jnp.exp(jnp.sum(lq1_ref[...] * lk1_ref[...], keepdims=True))
           - jnp.exp(jnp.sum(lq2_ref[...] * lk2_ref[...], keepdims=True)) + LAM_INIT)
    out = (acc1 / l1 - lam * (acc2 / l2)).T
    o_ref[0] = (_head_rms(out, gain_ref[...]) * (1.0 - LAM_INIT)).astype(BF16)


def _diff_attn(p, vt, pc, vtc, lams, gain, tq, tk):
    b, s, _ = p.shape
    lc = pc.shape[1]
    lam_spec = pl.BlockSpec((1, DH), lambda i, h, j: (0, 0))
    return pl.pallas_call(
        functools.partial(_diff_attn_kernel, tk=tk),
        grid=(b, DIFF_HEADS, s // tq),
        in_specs=[lam_spec, lam_spec, lam_spec, lam_spec,
                  pl.BlockSpec((1, tq, LANES), lambda i, h, j: (i, j, h)),
                  pl.BlockSpec((1, s, LANES), lambda i, h, j: (i, 0, 8 + h)),
                  pl.BlockSpec((1, LANES, s), lambda i, h, j: (i, h, 0)),
                  pl.BlockSpec((1, lc, LANES), lambda i, h, j: (i, 0, 8 + h)),
                  pl.BlockSpec((1, LANES, lc), lambda i, h, j: (i, h, 0)),
                  pl.BlockSpec((1, LANES), lambda i, h, j: (0, h))],
        out_specs=pl.BlockSpec((1, tq, LANES), lambda i, h, j: (i, j, h)),
        out_shape=jax.ShapeDtypeStruct((b, s, DIFF_HEADS * LANES), BF16),
        scratch_shapes=[pltpu.VMEM((2, lc + s, tq), F32), pltpu.VMEM((2, lc + s, tq), BF16)],
        compiler_params=_cparams(("arbitrary", "arbitrary", "arbitrary")),
        name="diff_attn",
    )(*lams, p, p, vt, pc, vtc, gain)


def _swa_attn_kernel(sink_ref, q_ref, k_ref, v_ref, kc_ref, vc_ref, gain_ref, o_ref,
                     kcat_ref, vcat_ref):
    n = pl.program_id(1)
    nb = pl.num_programs(1)
    s_len = k_ref.shape[1]
    lc = kc_ref.shape[1]
    n_band = 3 * BAND

    @pl.when(n == 0)
    def _():
        kcat_ref[n_band:, :] = kc_ref[0]
        vcat_ref[n_band:, :] = vc_ref[0]

    prev = pl.multiple_of(jnp.maximum(n - 1, 0) * BAND, BAND)
    cur = pl.multiple_of(n * BAND, BAND)
    nxt = pl.multiple_of(jnp.minimum(n + 1, nb - 1) * BAND, BAND)
    for t, off in enumerate((prev, cur, nxt)):
        kcat_ref[t * BAND:(t + 1) * BAND, :] = k_ref[0, pl.ds(off, BAND), :]
        vcat_ref[t * BAND:(t + 1) * BAND, :] = v_ref[0, pl.ds(off, BAND), :]

    rows = SWA_GROUP * BAND
    cols = n_band + lc
    qi = lax.broadcasted_iota(I32, (rows, cols), 0) & (BAND - 1)
    ki = lax.broadcasted_iota(I32, (rows, cols), 1)
    kpos = (n - 1) * BAND + ki
    in_band = jnp.where(jnp.abs(qi + BAND - ki) <= BAND,
                        jnp.where(kpos >= 0, jnp.where(kpos < s_len, 1, 0), 0), 0)
    ok = jnp.where(ki >= n_band, 1, in_band) > 0

    q = q_ref[0]
    for g in range(SWA_KV):
        kg = kcat_ref[:, g * DH:(g + 1) * DH]
        vg = vcat_ref[:, g * DH:(g + 1) * DH]
        qg = jnp.concatenate(
            [q[:, (g * SWA_GROUP + j) * DH:(g * SWA_GROUP + j + 1) * DH] for j in range(SWA_GROUP)],
            axis=0)
        s = lax.dot_general(qg, kg, (((1,), (1,)), ((), ())), preferred_element_type=F32)
        s = jnp.where(ok, s, NEG_INF)
        sink = jnp.concatenate(
            [jnp.full((BAND, 1), sink_ref[g * SWA_GROUP + j], F32) for j in range(SWA_GROUP)],
            axis=0)
        m = jnp.maximum(jnp.max(s, axis=-1, keepdims=True), sink)
        e = jnp.exp(s - m)
        denom = jnp.sum(e, axis=-1, keepdims=True) + jnp.exp(sink - m)
        o = jnp.dot(e.astype(BF16), vg, preferred_element_type=F32) / denom
        o = o * lax.rsqrt(jnp.mean(o * o, axis=-1, keepdims=True) + EPS)
        for j in range(SWA_GROUP):
            c0 = (g * SWA_GROUP + j) * DH
            o_ref[0, :, c0:c0 + DH] = (o[j * BAND:(j + 1) * BAND] * gain_ref[:, c0:c0 + DH]).astype(BF16)


def _swa_attn(p, pc, sink, gain):
    b, s, _ = p.shape
    lc = pc.shape[1]
    kvw = SWA_KV * DH
    return pl.pallas_call(
        _swa_attn_kernel,
        grid=(b, s // BAND),
        in_specs=[pl.BlockSpec(memory_space=pltpu.SMEM),
                  pl.BlockSpec((1, BAND, SWA_HEADS * DH), lambda i, j: (i, j, 3)),
                  pl.BlockSpec((1, s, kvw), lambda i, j: (i, 0, 16)),
                  pl.BlockSpec((1, s, kvw), lambda i, j: (i, 0, 17)),
                  pl.BlockSpec((1, lc, kvw), lambda i, j: (i, 0, 16)),
                  pl.BlockSpec((1, lc, kvw), lambda i, j: (i, 0, 17)),
                  pl.BlockSpec((1, SWA_HEADS * DH), lambda i, j: (0, 0))],
        out_specs=pl.BlockSpec((1, BAND, SWA_HEADS * DH), lambda i, j: (i, j, 0)),
        out_shape=jax.ShapeDtypeStruct((b, s, SWA_HEADS * DH), BF16),
        scratch_shapes=[pltpu.VMEM((3 * BAND + lc, kvw), BF16),
                        pltpu.VMEM((3 * BAND + lc, kvw), BF16)],
        compiler_params=_cparams(("arbitrary", "arbitrary")),
        name="swa_attn",
    )(sink, p, p, p, pc, pc, gain)


def _outproj_kernel(ad_ref, as_ref, wd_ref, ws_ref, x_ref, g1_ref, sh_ref, sc_ref, gn_ref,
                    x1_ref, h2_ref):
    a = (jnp.dot(ad_ref[0], wd_ref[...], preferred_element_type=F32)
         + jnp.dot(as_ref[0], ws_ref[...], preferred_element_type=F32))
    x1 = x_ref[0] + g1_ref[0] * a
    x1_ref[0] = x1
    h2_ref[0] = _modnorm(x1, gn_ref[...], sh_ref[0], sc_ref[0]).astype(BF16)


def _outproj(a_diff, a_swa, w_d, w_s, x, g1, sh2, sc2, gain, tm):
    b, s, d = x.shape
    wd = a_diff.shape[2]
    vec = pl.BlockSpec((1, 1, d), lambda i, j: (i, 0, 0))
    return pl.pallas_call(
        _outproj_kernel,
        grid=(b, s // tm),
        in_specs=[pl.BlockSpec((1, tm, wd), lambda i, j: (i, j, 0)),
                  pl.BlockSpec((1, tm, wd), lambda i, j: (i, j, 0)),
                  pl.BlockSpec((wd, d), lambda i, j: (0, 0)),
                  pl.BlockSpec((wd, d), lambda i, j: (0, 0)),
                  pl.BlockSpec((1, tm, d), lambda i, j: (i, j, 0)),
                  vec, vec, vec,
                  pl.BlockSpec((1, d), lambda i, j: (0, 0))],
        out_specs=[pl.BlockSpec((1, tm, d), lambda i, j: (i, j, 0)),
                   pl.BlockSpec((1, tm, d), lambda i, j: (i, j, 0))],
        out_shape=[jax.ShapeDtypeStruct((b, s, d), F32), jax.ShapeDtypeStruct((b, s, d), BF16)],
        compiler_params=_cparams(("arbitrary", "arbitrary")),
        name="out_proj",
    )(a_diff, a_swa, w_d, w_s, x, g1, sh2, sc2, gain)


def _extract_topk(s, rank, payload, k):
    vals, pays = [], []
    for _ in range(k):
        m = jnp.max(s, axis=0, keepdims=True)
        first = jnp.min(jnp.where(s == m, rank, jnp.inf), axis=0, keepdims=True)
        hit = rank == first
        vals.append(m)
        pays.append(first if payload is None
                    else jnp.sum(jnp.where(hit, payload, 0.0), axis=0, keepdims=True))
        s = jnp.where(hit, -jnp.inf, s)
    return jnp.concatenate(vals, axis=0), jnp.concatenate(pays, axis=0)


def _pair_candidates(s1, i1, s2, i2):
    k, tt = s1.shape
    assert k == 16, "block layout below is written for 16 x 16 pairs"
    sub = lax.broadcasted_iota(I32, (8, tt), 0)
    vals = [s1[0:1] + s2, s1[1:2] + s2[0:8]]
    pos = [lax.broadcasted_iota(I32, (k, tt), 0), k + sub]
    ids = [i1[0:1] * N_KEYS + i2, i1[1:2] * N_KEYS + i2[0:8]]
    for a in range(2, 8):
        keep = sub < k // (a + 1)
        vals.append(jnp.where(keep, s1[a:a + 1] + s2[0:8], -jnp.inf))
        pos.append(a * k + sub)
        ids.append(i1[a:a + 1] * N_KEYS + i2[0:8])
    vals.append(s1[8:16] + s2[0:1])
    pos.append((sub + 8) * k)
    ids.append(i1[8:16] * N_KEYS + i2[0:1])
    return (jnp.concatenate(vals, axis=0), jnp.concatenate(pos, axis=0).astype(F32),
            jnp.concatenate(ids, axis=0))


def _peer_route_kernel(h_ref, wq_ref, keys_ref, idx_ref, gate_ref):
    q = jnp.dot(h_ref[...], wq_ref[...], preferred_element_type=F32).astype(BF16)
    tt = q.shape[0]
    key_id = lax.broadcasted_iota(I32, (N_KEYS, tt), 0).astype(F32)
    for h in range(PEER_HEADS):
        halves = []
        for i in range(2):
            c0 = (h * 2 + i) * N_KEYS
            s = lax.dot_general(keys_ref[h, i], q[:, c0:c0 + N_KEYS], (((1,), (1,)), ((), ())),
                                preferred_element_type=F32)
            halves.append(_extract_topk(s, key_id, None, PEER_TOPK))
        (s1, i1), (s2, i2) = halves
        cand, pos, cidx = _pair_candidates(s1, i1, s2, i2)
        top_s, top_i = _extract_topk(cand, pos, cidx, PEER_TOPK)
        e = jnp.exp(top_s - jnp.max(top_s, axis=0, keepdims=True))
        r0 = h * PEER_TOPK
        idx_ref[r0:r0 + PEER_TOPK, :] = top_i.astype(I32)
        gate_ref[r0:r0 + PEER_TOPK, :] = e / jnp.sum(e, axis=0, keepdims=True)


def _peer_route(h2, wq, keys, tt):
    t, d = h2.shape
    nq = wq.shape[1]
    return pl.pallas_call(
        _peer_route_kernel,
        grid=(t // tt,),
        in_specs=[pl.BlockSpec((tt, d), lambda i: (i, 0)),
                  pl.BlockSpec((d, nq), lambda i: (0, 0)),
                  pl.BlockSpec(keys.shape, lambda i: (0, 0, 0, 0))],
        out_specs=[pl.BlockSpec((PEER_SEL, tt), lambda i: (0, i)),
                   pl.BlockSpec((PEER_SEL, tt), lambda i: (0, i))],
        out_shape=[jax.ShapeDtypeStruct((PEER_SEL, t), I32), jax.ShapeDtypeStruct((PEER_SEL, t), F32)],
        compiler_params=_cparams(("arbitrary",)),
        name="peer_route",
    )(h2, wq, keys)


def _split_bf16(x):
    hi = x.astype(BF16)
    return hi, (x - hi.astype(F32)).astype(BF16)


def _peer_expert_kernel(idx_cur_ref, idx_nxt_ref, uv_ref, h_ref, gate_ref, seg_ref, grpt_ref,
                        o_ref, buf_ref, sem_ref, *, tb, tg):
    i = pl.program_id(0)
    nsteps = pl.num_programs(0)

    def row_copy(idx_ref, slot, t, j):
        return pltpu.make_async_copy(uv_ref.at[idx_ref[t * PEER_SEL + j]],
                                     buf_ref.at[slot, t, j], sem_ref.at[slot, t])

    def wait_tokens(slot, toks):
        for t in toks:
            for j in range(PEER_SEL):
                row_copy(idx_cur_ref, slot, t, j).wait()

    @pl.when(i == 0)
    def _():
        for t in range(tb):
            for j in range(PEER_SEL):
                row_copy(idx_cur_ref, 0, t, j).start(priority=j % 2)

    diag = (lax.broadcasted_iota(I32, (SLAB, PEER_SEL * SLAB), 1) & (SLAB - 1)) == \
        lax.broadcasted_iota(I32, (SLAB, PEER_SEL * SLAB), 0)

    def consume(slot):
        for t in range(PRE_ISSUE):
            for j in range(PEER_SEL):
                row_copy(idx_nxt_ref, 1 - slot, t, j).start(priority=j % 2)
        wait_tokens(slot, range(tb))
        g_hi, g_lo = _split_bf16(gate_ref[...])
        gate_rep = (jnp.dot(g_hi, grpt_ref[...], preferred_element_type=F32)
                    + jnp.dot(g_lo, grpt_ref[...], preferred_element_type=F32))

        def token_pair(t_u, t_v, w_row):
            z_parts, r = [], None
            if t_v is not None:
                wt = jnp.where(diag, jnp.broadcast_to(w_row, diag.shape), 0.0)
                wt_hi, wt_lo = _split_bf16(wt)
                wt2 = jnp.concatenate([wt_hi, wt_lo], axis=0)
            for n in range(PEER_SEL // SLAB):
                e0, c0 = n * SLAB, n * SLAB * SLAB
                if t_u is not None:
                    for j in range(e0, e0 + SLAB // 2) if t_u >= PRE_ISSUE else ():
                        row_copy(idx_nxt_ref, 1 - slot, t_u, j).start(priority=j % 2)
                    u_tile = buf_ref[slot, t_u, e0:e0 + SLAB, 0:SLAB, :].reshape(SLAB * SLAB, LANES)
                    y = lax.dot_general(h_ref[t_u], u_tile, (((1,), (1,)), ((), ())),
                                        preferred_element_type=F32)
                    z_parts.append(jnp.sum(jnp.where(diag[:, :SLAB * SLAB], y, 0.0),
                                           axis=0, keepdims=True))
                    for j in range(e0 + SLAB // 2, e0 + SLAB) if t_u >= PRE_ISSUE else ():
                        row_copy(idx_nxt_ref, 1 - slot, t_u, j).start(priority=j % 2)
                if t_v is not None:
                    v_tile = buf_ref[slot, t_v, e0:e0 + SLAB, SLAB:2 * SLAB, :].reshape(
                        SLAB * SLAB, LANES)
                    part = jnp.dot(wt2[:, c0:c0 + SLAB * SLAB], v_tile, preferred_element_type=F32)
                    r = part if r is None else r + part
            z = jnp.concatenate(z_parts, axis=1) if z_parts else None
            out = None if r is None else r[:SLAB] + r[SLAB:]
            return z, out

        def gate_rows(g, z):
            tile = SLAB * SLAB
            z8 = jnp.concatenate([z, jnp.zeros((8 - tg, z.shape[1]), F32)], axis=0) if tg < 8 else z
            stacked = jnp.concatenate([z8[:, n * tile:(n + 1) * tile]
                                       for n in range(z.shape[1] // tile)], axis=0)
            s_hi, s_lo = _split_bf16(stacked)
            seg = jnp.dot(jnp.concatenate([s_hi, s_lo], axis=0), seg_ref[...],
                          preferred_element_type=F32)
            seg = seg[:stacked.shape[0]] + seg[stacked.shape[0]:]
            act = jnp.concatenate([seg[8 * n:8 * n + tg] for n in range(z.shape[1] // tile)], axis=1)
            gelu = 0.5 * act * (1.0 + jnp.tanh(math.sqrt(2.0 / math.pi)
                                               * (act + 0.044715 * act * act * act)))
            return gate_rep[g * tg:(g + 1) * tg, :] * gelu

        assert GATE_LAG >= tg + 2
        w_reps, outs, zs = {}, {}, []
        for n in range(tb + GATE_LAG):
            t_u = n if n < tb else None
            t_v = n - GATE_LAG if n >= GATE_LAG else None
            w_row = None if t_v is None else w_reps[t_v // tg][t_v % tg:t_v % tg + 1]
            z, out = token_pair(t_u, t_v, w_row)
            if t_u is not None:
                zs.append(z)
            if t_v is not None:
                outs[t_v] = out
            if n % tg == 0 and tg <= n <= tb:
                g = n // tg - 1
                w_reps[g] = gate_rows(g, jnp.concatenate(zs[g * tg:(g + 1) * tg], axis=0))
        for t in range(tb):
            for r in range(SLAB):
                o_ref[t:t + 1, r * LANES:(r + 1) * LANES] = outs[t][r:r + 1, :]

        @pl.when(i == nsteps - 1)
        def _():
            wait_tokens(1 - slot, range(tb))

    @pl.when((i & 1) == 0)
    def _():
        consume(0)

    @pl.when((i & 1) == 1)
    def _():
        consume(1)


def _peer_expert(idx_flat, uv, h_slab, gate, tb):
    t = h_slab.shape[0]
    nsteps = t // tb
    tile = SLAB * SLAB
    seg = (lax.broadcasted_iota(I32, (tile, tile), 0) // SLAB
           == lax.broadcasted_iota(I32, (tile, tile), 1) // SLAB).astype(BF16)
    grpt = (lax.broadcasted_iota(I32, (PEER_SEL, PEER_SEL * SLAB), 0)
            == lax.broadcasted_iota(I32, (PEER_SEL, PEER_SEL * SLAB), 1) // SLAB).astype(BF16)
    blk = tb * PEER_SEL
    return pl.pallas_call(
        functools.partial(_peer_expert_kernel, tb=tb, tg=min(4, tb)),
        grid=(nsteps,),
        in_specs=[pl.BlockSpec((blk,), lambda i: (i,), memory_space=pltpu.SMEM),
                  pl.BlockSpec((blk,), lambda i: (jnp.minimum(i + 1, nsteps - 1),),
                               memory_space=pltpu.SMEM),
                  pl.BlockSpec(memory_space=pl.ANY),
                  pl.BlockSpec((tb, SLAB, LANES), lambda i: (i, 0, 0)),
                  pl.BlockSpec((tb, PEER_SEL), lambda i: (i, 0)),
                  pl.BlockSpec(seg.shape, lambda i: (0, 0)),
                  pl.BlockSpec(grpt.shape, lambda i: (0, 0))],
        out_specs=pl.BlockSpec((tb, SLAB * LANES), lambda i: (i, 0)),
        out_shape=jax.ShapeDtypeStruct((t, SLAB * LANES), F32),
        scratch_shapes=[pltpu.VMEM((2, tb, PEER_SEL, 2 * SLAB, LANES), BF16),
                        pltpu.SemaphoreType.DMA((2, tb))],
        compiler_params=_cparams(("arbitrary",)),
        name="peer_expert",
    )(idx_flat, idx_flat, uv, h_slab, gate, seg, grpt)


def _final_kernel(x1_ref, pe_ref, g2_ref, gn_ref, o_ref):
    x2 = x1_ref[0] + g2_ref[0] * pe_ref[0]
    o_ref[0] = x2 * lax.rsqrt(jnp.mean(x2 * x2, axis=-1, keepdims=True) + EPS) * gn_ref[...]


def _final(x1, pe, g2, gain, tm):
    b, s, d = x1.shape
    tok = pl.BlockSpec((1, tm, d), lambda i, j: (i, j, 0))
    return pl.pallas_call(
        _final_kernel,
        grid=(b, s // tm),
        in_specs=[tok, tok, pl.BlockSpec((1, 1, d), lambda i, j: (i, 0, 0)),
                  pl.BlockSpec((1, d), lambda i, j: (0, 0))],
        out_specs=tok,
        out_shape=jax.ShapeDtypeStruct((b, s, d), F32),
        compiler_params=_cparams(("arbitrary", "arbitrary")),
        name="final_norm",
    )(x1, pe, g2, gain)


def kernel(x, c, ctx, c_ctx, w_ada, b_ada, norm_attn, w_in, diff_lambda_q1, diff_lambda_k1,
           diff_lambda_q2, diff_lambda_k2, diff_norm, swa_sink, swa_norm, w_out, norm_ffn,
           peer_w_q, peer_sub_keys, peer_u, peer_v, final_norm):
    b, s, d = x.shape
    assert w_ada.shape[0] == 1, "single layer only"
    t = b * s

    cc = jnp.zeros((8, d), F32).at[:b].set(c).at[b].set(c_ctx)
    mod = _adaln(cc, w_ada[0], b_ada[0])
    sh1, sc1, g1, sh2, sc2, g2 = [m[:, None, :] for m in jnp.split(mod, 6, axis=-1)]

    w_in_b = w_in[0].astype(BF16)
    gain_attn = norm_attn[0].reshape(1, d)
    cos_t, sin_t = _rope_tables(s)
    tm = min(512, s)
    dvw = DIFF_HEADS * LANES
    w_vt = w_in_b[:, 2 * dvw:3 * dvw].T
    p, vt = _inproj(x, sh1[:b], sc1[:b], gain_attn, w_in_b, w_vt, cos_t, sin_t, True, tm)
    lc = ctx.shape[1]
    ones = jnp.ones((lc, LANES), F32)
    ctx_sh = jnp.broadcast_to(sh1[b:b + 1], (b, 1, d))
    ctx_sc = jnp.broadcast_to(sc1[b:b + 1], (b, 1, d))
    pc, vtc = _inproj(ctx, ctx_sh, ctx_sc, gain_attn, w_in_b, w_vt, ones, ones, False, lc)

    lams = [v[0].reshape(1, DH).astype(F32) for v in
            (diff_lambda_q1, diff_lambda_k1, diff_lambda_q2, diff_lambda_k2)]
    a_diff = _diff_attn(p, vt, pc, vtc, lams, diff_norm[0].reshape(1, -1), min(512, s), min(512, s))
    a_swa = _swa_attn(p, pc, swa_sink[0].astype(F32), swa_norm[0].reshape(1, -1))

    w_out_b = w_out[0].astype(BF16)
    dw = a_diff.shape[2]
    x1, h2 = _outproj(a_diff, a_swa, w_out_b[:dw], w_out_b[dw:], x, g1[:b], sh2[:b], sc2[:b],
                      norm_ffn[0].reshape(1, d), tm)

    idx_t, gate_t = _peer_route(h2.reshape(t, d), peer_w_q[0].astype(BF16),
                                peer_sub_keys[0].astype(BF16), min(256, t))
    uv = jnp.concatenate([peer_u[0].reshape(-1, SLAB, LANES),
                          peer_v[0].reshape(-1, SLAB, LANES)], axis=1).astype(BF16)
    pe = _peer_expert(idx_t.T.reshape(-1), uv, h2.reshape(t, SLAB, LANES), gate_t.T, 16)

    return _final(x1, pe.reshape(b, s, d), g2[:b], final_norm.reshape(1, d), tm)
```

```python
import functools
import math

import jax
import jax.numpy as jnp
from jax import lax
from jax.experimental import pallas as pl
from jax.experimental.pallas import tpu as pltpu

F32 = jnp.float32
BF16 = jnp.bfloat16
I32 = jnp.int32

EPS = 1e-6
NEG_INF = -1e30
ROPE_THETA = 10000.0
GRID_W = 64
ROPE_PAIRS = 16

LANES = 128
DH = 64
DIFF_HEADS = 8
SWA_HEADS = 16
SWA_KV = 4
SWA_GROUP = SWA_HEADS // SWA_KV
BAND = 128
LAM_INIT = 0.8 - 0.6 * math.exp(-0.3 * 0)
LOG2E = math.log2(math.e)

PEER_HEADS = 8
PEER_TOPK = 16
N_KEYS = 128
PEER_SEL = PEER_HEADS * PEER_TOPK
SLAB = 16
PRE_ISSUE = 2
GATE_LAG = 6

VMEM_LIMIT = 56 * 1024 * 1024


def _cparams(sem):
    return pltpu.CompilerParams(dimension_semantics=sem, vmem_limit_bytes=VMEM_LIMIT)


def _adaln_kernel(c_ref, w_ref, b_ref, o_ref):
    c = c_ref[...]
    s = c * (1.0 / (1.0 + jnp.exp(-c)))
    o_ref[...] = jnp.dot(s, w_ref[...], preferred_element_type=F32,
                         precision=lax.Precision.HIGHEST) + b_ref[...]


def _adaln(cc, w, b):
    rows, d = cc.shape
    n = w.shape[1]
    tn = 1024
    return pl.pallas_call(
        _adaln_kernel,
        grid=(n // tn,),
        in_specs=[pl.BlockSpec((rows, d), lambda j: (0, 0)),
                  pl.BlockSpec((d, tn), lambda j: (0, j)),
                  pl.BlockSpec((1, tn), lambda j: (0, j))],
        out_specs=pl.BlockSpec((rows, tn), lambda j: (0, j)),
        out_shape=jax.ShapeDtypeStruct((rows, n), F32),
        compiler_params=_cparams(("arbitrary",)),
        name="adaln",
    )(cc, w, b.reshape(1, n))


def _modnorm(x, gain, shift, scale):
    y = x * lax.rsqrt(jnp.mean(x * x, axis=-1, keepdims=True) + EPS)
    return (y * gain) * (1.0 + scale) + shift


def _swap16(p):
    lane = lax.broadcasted_iota(I32, p.shape, 1)
    up = pltpu.roll(p, LANES - 16, 1)
    dn = pltpu.roll(p, 16, 1)
    return jnp.where((lane & 31) < 16, up, dn)


def _inproj_kernel(x_ref, sh_ref, sc_ref, g_ref, w_ref, wvt_ref, cos_ref, sin_ref, o_ref, vt_ref, *,
                   rope_chunks, chunk_scale):
    h = _modnorm(x_ref[0], g_ref[...], sh_ref[0], sc_ref[0]).astype(BF16)
    p = jnp.dot(h, w_ref[...], preferred_element_type=F32)
    n_chunks = p.shape[1] // LANES
    if rope_chunks:
        cs = cos_ref[...]
        sn = sin_ref[...]
    for j in range(n_chunks):
        pj = p[:, j * LANES:(j + 1) * LANES]
        if j in rope_chunks:
            pj = pj * cs + _swap16(pj) * sn
        if j in chunk_scale:
            pj = pj * chunk_scale[j]
        o_ref[0, :, j * LANES:(j + 1) * LANES] = pj.astype(BF16)
    vt_ref[0] = lax.dot_general(wvt_ref[...], h, (((1,), (1,)), ((), ())),
                                preferred_element_type=F32).astype(BF16)


def _inproj(x, shift, scale, gain, w, w_vt, cos_t, sin_t, rope, tm):
    b, s, d = x.shape
    n = w.shape[1]
    nv = w_vt.shape[0]
    rope_chunks = frozenset(list(range(0, 16)) + list(range(24, 34))) if rope else frozenset()
    chunk_scale = {j: DH ** -0.5 * LOG2E for j in range(0, 8)}
    chunk_scale.update({j: DH ** -0.5 for j in range(24, 32)})
    kern = functools.partial(_inproj_kernel, rope_chunks=rope_chunks, chunk_scale=chunk_scale)
    return pl.pallas_call(
        kern,
        grid=(b, s // tm),
        in_specs=[pl.BlockSpec((1, tm, d), lambda i, j: (i, j, 0)),
                  pl.BlockSpec((1, 1, d), lambda i, j: (i, 0, 0)),
                  pl.BlockSpec((1, 1, d), lambda i, j: (i, 0, 0)),
                  pl.BlockSpec((1, d), lambda i, j: (0, 0)),
                  pl.BlockSpec((d, n), lambda i, j: (0, 0), pipeline_mode=pl.Buffered(1)),
                  pl.BlockSpec((nv, d), lambda i, j: (0, 0), pipeline_mode=pl.Buffered(1)),
                  pl.BlockSpec((tm, LANES), lambda i, j: (j, 0)),
                  pl.BlockSpec((tm, LANES), lambda i, j: (j, 0))],
        out_specs=[pl.BlockSpec((1, tm, n), lambda i, j: (i, j, 0)),
                   pl.BlockSpec((1, nv, tm), lambda i, j: (i, 0, j))],
        out_shape=[jax.ShapeDtypeStruct((b, s, n), BF16), jax.ShapeDtypeStruct((b, nv, s), BF16)],
        compiler_params=_cparams(("arbitrary", "arbitrary")),
        name="in_proj",
    )(x, shift, scale, gain, w, w_vt, cos_t, sin_t)


def _rope_tables(s):
    rows = s // GRID_W
    row = jnp.repeat(jnp.arange(rows, dtype=F32), GRID_W)
    col = jnp.tile(jnp.arange(GRID_W, dtype=F32), rows)
    freqs = ROPE_THETA ** (-jnp.arange(ROPE_PAIRS, dtype=F32) / ROPE_PAIRS)
    ar = row[:, None] * freqs
    ac = col[:, None] * freqs
    cos64 = jnp.concatenate([jnp.cos(ar), jnp.cos(ar), jnp.cos(ac), jnp.cos(ac)], axis=1)
    sin64 = jnp.concatenate([-jnp.sin(ar), jnp.sin(ar), -jnp.sin(ac), jnp.sin(ac)], axis=1)
    return jnp.tile(cos64, (1, 2)), jnp.tile(sin64, (1, 2))


def _head_rms(o, gain):
    return o * lax.rsqrt(jnp.mean(o * o, axis=-1, keepdims=True) + EPS) * gain


def _diff_attn_kernel(lq1_ref, lk1_ref, lq2_ref, lk2_ref, q_ref, k_ref, vt_ref, kc_ref, vtc_ref,
                      gain_ref, o_ref, s_ref, *, tk):
    tq = q_ref.shape[1]
    s_len = k_ref.shape[1]
    q = q_ref[0]
    lane = lax.broadcasted_iota(I32, q.shape, 1)
    zero = jnp.zeros_like(q)
    q_maps = (jnp.where(lane < DH, q, zero), jnp.where(lane >= DH, q, zero))
    lc = kc_ref.shape[1]
    blocks = [(0, lc, lambda: kc_ref[0], lambda: vtc_ref[0])]
    blocks += [(lc + j * tk, tk, lambda j=j: k_ref[0, j * tk:(j + 1) * tk, :],
                lambda j=j: vt_ref[0, :, j * tk:(j + 1) * tk]) for j in range(s_len // tk)]

    def score_block(i, blk, m):
        r0, rows, kblk, _ = blk
        s = lax.dot_general(kblk(), q_maps[i], (((1,), (1,)), ((), ())),
                            preferred_element_type=F32)
        s_ref[i, r0:r0 + rows, :] = s
        return jnp.maximum(m, jnp.max(s, axis=0, keepdims=True))

    def prob_block(i, blk, m, l, acc):
        r0, rows, _, vtblk = blk
        p = jnp.exp2(s_ref[i, r0:r0 + rows, :] - m)
        part = jnp.dot(vtblk(), p.astype(BF16), preferred_element_type=F32)
        return l + jnp.sum(p, axis=0, keepdims=True), part if acc is None else acc + part

    m_init = jnp.full((1, tq), NEG_INF, F32)
    l_init = jnp.zeros((1, tq), F32)
    m1 = m_init
    for blk in blocks:
        m1 = score_block(0, blk, m1)
    m2, l1, acc1 = m_init, l_init, None
    for blk in blocks:
        m2 = score_block(1, blk, m2)
        l1, acc1 = prob_block(0, blk, m1, l1, acc1)
    l2, acc2 = l_init, None
    for blk in blocks:
        l2, acc2 = prob_block(1, blk, m2, l2, acc2)

    lam = (jnp.exp(jnp.sum(lq1_ref[...] * lk1_ref[...], keepdims=True))
           - jnp.exp(jnp.sum(lq2_ref[...] * lk2_ref[...], keepdims=True)) + LAM_INIT)
    out = (acc1 / l1 - lam * (acc2 / l2)).T
    o_ref[0] = (_head_rms(out, gain_ref[...]) * (1.0 - LAM_INIT)).astype(BF16)


def _diff_attn(p, vt, pc, vtc, lams, gain, tq, tk):
    b, s, _ = p.shape
    lc = pc.shape[1]
    lam_spec = pl.BlockSpec((1, DH), lambda i, h, j: (0, 0))
    return pl.pallas_call(
        functools.partial(_diff_attn_kernel, tk=tk),
        grid=(b, DIFF_HEADS, s // tq),
        in_specs=[lam_spec, lam_spec, lam_spec, lam_spec,
                  pl.BlockSpec((1, tq, LANES), lambda i, h, j: (i, j, h)),
                  pl.BlockSpec((1, s, LANES), lambda i, h, j: (i, 0, 8 + h)),
                  pl.BlockSpec((1, LANES, s), lambda i, h, j: (i, h, 0)),
                  pl.BlockSpec((1, lc, LANES), lambda i, h, j: (i, 0, 8 + h)),
                  pl.BlockSpec((1, LANES, lc), lambda i, h, j: (i, h, 0)),
                  pl.BlockSpec((1, LANES), lambda i, h, j: (0, h))],
        out_specs=pl.BlockSpec((1, tq, LANES), lambda i, h, j: (i, j, h)),
        out_shape=jax.ShapeDtypeStruct((b, s, DIFF_HEADS * LANES), BF16),
        scratch_shapes=[pltpu.VMEM((2, lc + s, tq), F32)],
        compiler_params=_cparams(("arbitrary", "arbitrary", "arbitrary")),
        name="diff_attn",
    )(*lams, p, p, vt, pc, vtc, gain)


def _swa_attn_kernel(sink_ref, q_ref, k_ref, vt_ref, kc_ref, vtc_ref, gain_ref, o_ref,
                     kcat_ref, vtcat_ref):
    n = pl.program_id(1)
    nb = pl.num_programs(1)
    s_len = k_ref.shape[1]
    lc = kc_ref.shape[1]
    n_band = 3 * BAND

    @pl.when(n == 0)
    def _():
        kcat_ref[n_band:, :] = kc_ref[0]
        vtcat_ref[:, n_band:] = vtc_ref[0]

    prev = pl.multiple_of(jnp.maximum(n - 1, 0) * BAND, BAND)
    cur = pl.multiple_of(n * BAND, BAND)
    nxt = pl.multiple_of(jnp.minimum(n + 1, nb - 1) * BAND, BAND)
    for t, off in enumerate((prev, cur, nxt)):
        kcat_ref[t * BAND:(t + 1) * BAND, :] = k_ref[0, pl.ds(off, BAND), :]
        vtcat_ref[:, t * BAND:(t + 1) * BAND] = vt_ref[0, :, pl.ds(off, BAND)]

    keys = n_band + lc
    cols = SWA_GROUP * BAND
    ki = lax.broadcasted_iota(I32, (keys, cols), 0)
    qi = lax.broadcasted_iota(I32, (keys, cols), 1) & (BAND - 1)
    kpos = (n - 1) * BAND + ki
    in_band = jnp.where(jnp.abs(qi + BAND - ki) <= BAND,
                        jnp.where(kpos >= 0, jnp.where(kpos < s_len, 1, 0), 0), 0)
    ok = jnp.where(ki >= n_band, 1, in_band) > 0

    q = q_ref[0]
    outs = []
    for g in range(SWA_KV):
        kg = kcat_ref[:, g * DH:(g + 1) * DH]
        vtg = vtcat_ref[g * DH:(g + 1) * DH, :]
        qg = jnp.concatenate(
            [q[:, (g * SWA_GROUP + j) * DH:(g * SWA_GROUP + j + 1) * DH] for j in range(SWA_GROUP)],
            axis=0)
        s = lax.dot_general(kg, qg, (((1,), (1,)), ((), ())), preferred_element_type=F32)
        s = jnp.where(ok, s, NEG_INF)
        sink = jnp.concatenate(
            [jnp.full((1, BAND), sink_ref[g * SWA_GROUP + j], F32) for j in range(SWA_GROUP)],
            axis=1)
        m = jnp.maximum(jnp.max(s, axis=0, keepdims=True), sink)
        e = jnp.exp(s - m)
        denom = jnp.sum(e, axis=0, keepdims=True) + jnp.exp(sink - m)
        o = jnp.dot(vtg, e.astype(BF16), preferred_element_type=F32) / denom
        outs.append(o * lax.rsqrt(jnp.mean(o * o, axis=0, keepdims=True) + EPS))
    o_all = jnp.concatenate(outs, axis=0).T
    for g in range(SWA_KV):
        for j in range(SWA_GROUP):
            c0 = (g * SWA_GROUP + j) * DH
            o_ref[0, :, c0:c0 + DH] = (o_all[j * BAND:(j + 1) * BAND, g * DH:(g + 1) * DH]
                                       * gain_ref[:, c0:c0 + DH]).astype(BF16)


def _swa_attn(p, vt, pc, vtc, sink, gain):
    b, s, _ = p.shape
    lc = pc.shape[1]
    kvw = SWA_KV * DH
    vblk = vt.shape[1] // kvw - 1
    return pl.pallas_call(
        _swa_attn_kernel,
        grid=(b, s // BAND),
        in_specs=[pl.BlockSpec(memory_space=pltpu.SMEM),
                  pl.BlockSpec((1, BAND, SWA_HEADS * DH), lambda i, j: (i, j, 3)),
                  pl.BlockSpec((1, s, kvw), lambda i, j: (i, 0, 16)),
                  pl.BlockSpec((1, kvw, s), lambda i, j: (i, vblk, 0)),
                  pl.BlockSpec((1, lc, kvw), lambda i, j: (i, 0, 16)),
                  pl.BlockSpec((1, kvw, lc), lambda i, j: (i, vblk, 0)),
                  pl.BlockSpec((1, SWA_HEADS * DH), lambda i, j: (0, 0))],
        out_specs=pl.BlockSpec((1, BAND, SWA_HEADS * DH), lambda i, j: (i, j, 0)),
        out_shape=jax.ShapeDtypeStruct((b, s, SWA_HEADS * DH), BF16),
        scratch_shapes=[pltpu.VMEM((3 * BAND + lc, kvw), BF16),
                        pltpu.VMEM((kvw, 3 * BAND + lc), BF16)],
        compiler_params=_cparams(("arbitrary", "arbitrary")),
        name="swa_attn",
    )(sink, p, p, vt, pc, vtc, gain)


def _outproj_kernel(ad_ref, as_ref, wd_ref, ws_ref, x_ref, g1_ref, sh_ref, sc_ref, gn_ref,
                    x1_ref, h2_ref):
    a = (jnp.dot(ad_ref[0], wd_ref[...], preferred_element_type=F32)
         + jnp.dot(as_ref[0], ws_ref[...], preferred_element_type=F32))
    x1 = x_ref[0] + g1_ref[0] * a
    x1_ref[0] = x1
    h2_ref[0] = _modnorm(x1, gn_ref[...], sh_ref[0], sc_ref[0]).astype(BF16)


def _outproj(a_diff, a_swa, w_d, w_s, x, g1, sh2, sc2, gain, tm):
    b, s, d = x.shape
    wd = a_diff.shape[2]
    vec = pl.BlockSpec((1, 1, d), lambda i, j: (i, 0, 0))
    return pl.pallas_call(
        _outproj_kernel,
        grid=(b, s // tm),
        in_specs=[pl.BlockSpec((1, tm, wd), lambda i, j: (i, j, 0)),
                  pl.BlockSpec((1, tm, wd), lambda i, j: (i, j, 0)),
                  pl.BlockSpec((wd, d), lambda i, j: (0, 0)),
                  pl.BlockSpec((wd, d), lambda i, j: (0, 0)),
                  pl.BlockSpec((1, tm, d), lambda i, j: (i, j, 0)),
                  vec, vec, vec,
                  pl.BlockSpec((1, d), lambda i, j: (0, 0))],
        out_specs=[pl.BlockSpec((1, tm, d), lambda i, j: (i, j, 0)),
                   pl.BlockSpec((1, tm, d), lambda i, j: (i, j, 0))],
        out_shape=[jax.ShapeDtypeStruct((b, s, d), F32), jax.ShapeDtypeStruct((b, s, d), BF16)],
        compiler_params=_cparams(("arbitrary", "arbitrary")),
        name="out_proj",
    )(a_diff, a_swa, w_d, w_s, x, g1, sh2, sc2, gain)


def _extract_topk(s, rank, payload, k):
    vals, pays = [], []
    for _ in range(k):
        m = jnp.max(s, axis=0, keepdims=True)
        first = jnp.min(jnp.where(s == m, rank, jnp.inf), axis=0, keepdims=True)
        hit = rank == first
        vals.append(m)
        pays.append(first if payload is None
                    else jnp.sum(jnp.where(hit, payload, 0.0), axis=0, keepdims=True))
        s = jnp.where(hit, -jnp.inf, s)
    return jnp.concatenate(vals, axis=0), jnp.concatenate(pays, axis=0)


def _pair_candidates(s1, i1, s2, i2):
    k, tt = s1.shape
    assert k == 16, "block layout below is written for 16 x 16 pairs"
    sub = lax.broadcasted_iota(I32, (8, tt), 0)
    vals = [s1[0:1] + s2, s1[1:2] + s2[0:8]]
    pos = [lax.broadcasted_iota(I32, (k, tt), 0), k + sub]
    ids = [i1[0:1] * N_KEYS + i2, i1[1:2] * N_KEYS + i2[0:8]]
    for a in range(2, 8):
        keep = sub < k // (a + 1)
        vals.append(jnp.where(keep, s1[a:a + 1] + s2[0:8], -jnp.inf))
        pos.append(a * k + sub)
        ids.append(i1[a:a + 1] * N_KEYS + i2[0:8])
    vals.append(s1[8:16] + s2[0:1])
    pos.append((sub + 8) * k)
    ids.append(i1[8:16] * N_KEYS + i2[0:1])
    return (jnp.concatenate(vals, axis=0), jnp.concatenate(pos, axis=0).astype(F32),
            jnp.concatenate(ids, axis=0))


def _peer_route_kernel(h_ref, wq_ref, keys_ref, idx_ref, gate_ref):
    q = jnp.dot(h_ref[...], wq_ref[...], preferred_element_type=F32).astype(BF16)
    tt = q.shape[0]
    key_id = lax.broadcasted_iota(I32, (N_KEYS, tt), 0).astype(F32)
    for h in range(PEER_HEADS):
        halves = []
        for i in range(2):
            c0 = (h * 2 + i) * N_KEYS
            s = lax.dot_general(keys_ref[h, i], q[:, c0:c0 + N_KEYS], (((1,), (1,)), ((), ())),
                                preferred_element_type=F32)
            halves.append(_extract_topk(s, key_id, None, PEER_TOPK))
        (s1, i1), (s2, i2) = halves
        cand, pos, cidx = _pair_candidates(s1, i1, s2, i2)
        top_s, top_i = _extract_topk(cand, pos, cidx, PEER_TOPK)
        e = jnp.exp(top_s - jnp.max(top_s, axis=0, keepdims=True))
        r0 = h * PEER_TOPK
        idx_ref[r0:r0 + PEER_TOPK, :] = top_i.astype(I32)
        gate_ref[r0:r0 + PEER_TOPK, :] = e / jnp.sum(e, axis=0, keepdims=True)


def _peer_route(h2, wq, keys, tt):
    t, d = h2.shape
    nq = wq.shape[1]
    return pl.pallas_call(
        _peer_route_kernel,
        grid=(t // tt,),
        in_specs=[pl.BlockSpec((tt, d), lambda i: (i, 0)),
                  pl.BlockSpec((d, nq), lambda i: (0, 0)),
                  pl.BlockSpec(keys.shape, lambda i: (0, 0, 0, 0))],
        out_specs=[pl.BlockSpec((PEER_SEL, tt), lambda i: (0, i)),
                   pl.BlockSpec((PEER_SEL, tt), lambda i: (0, i))],
        out_shape=[jax.ShapeDtypeStruct((PEER_SEL, t), I32), jax.ShapeDtypeStruct((PEER_SEL, t), F32)],
        compiler_params=_cparams(("arbitrary",)),
        name="peer_route",
    )(h2, wq, keys)


def _split_bf16(x):
    hi = x.astype(BF16)
    return hi, (x - hi.astype(F32)).astype(BF16)


def _peer_expert_kernel(idx_cur_ref, idx_nxt_ref, uv_ref, h_ref, gate_ref, seg_ref, grpt_ref,
                        o_ref, buf_ref, sem_ref, *, tb, tg):
    i = pl.program_id(0)
    nsteps = pl.num_programs(0)

    def row_copy(idx_ref, slot, t, j):
        return pltpu.make_async_copy(uv_ref.at[idx_ref[t * PEER_SEL + j]],
                                     buf_ref.at[slot, t, j], sem_ref.at[slot, t])

    def wait_tokens(slot, toks):
        for t in toks:
            for j in range(PEER_SEL):
                row_copy(idx_cur_ref, slot, t, j).wait()

    @pl.when(i == 0)
    def _():
        for t in range(tb):
            for j in range(PEER_SEL):
                row_copy(idx_cur_ref, 0, t, j).start(priority=j % 2)

    diag = (lax.broadcasted_iota(I32, (SLAB, PEER_SEL * SLAB), 1) & (SLAB - 1)) == \
        lax.broadcasted_iota(I32, (SLAB, PEER_SEL * SLAB), 0)

    def consume(slot):
        for t in range(PRE_ISSUE):
            for j in range(PEER_SEL):
                row_copy(idx_nxt_ref, 1 - slot, t, j).start(priority=j % 2)
        wait_tokens(slot, range(tb))
        g_hi, g_lo = _split_bf16(gate_ref[...])
        gate_rep = (jnp.dot(g_hi, grpt_ref[...], preferred_element_type=F32)
                    + jnp.dot(g_lo, grpt_ref[...], preferred_element_type=F32))

        def token_pair(t_u, t_v, w_row):
            z_parts, r = [], None
            if t_v is not None:
                wt = jnp.where(diag, jnp.broadcast_to(w_row, diag.shape), 0.0)
                wt_hi, wt_lo = _split_bf16(wt)
                wt2 = jnp.concatenate([wt_hi, wt_lo], axis=0)
            for n in range(PEER_SEL // SLAB):
                e0, c0 = n * SLAB, n * SLAB * SLAB
                if t_u is not None:
                    for j in range(e0, e0 + SLAB // 2) if t_u >= PRE_ISSUE else ():
                        row_copy(idx_nxt_ref, 1 - slot, t_u, j).start(priority=j % 2)
                    u_tile = buf_ref[slot, t_u, e0:e0 + SLAB, 0:SLAB, :].reshape(SLAB * SLAB, LANES)
                    y = lax.dot_general(h_ref[t_u], u_tile, (((1,), (1,)), ((), ())),
                                        preferred_element_type=F32)
                    z_parts.append(jnp.sum(jnp.where(diag[:, :SLAB * SLAB], y, 0.0),
                                           axis=0, keepdims=True))
                    for j in range(e0 + SLAB // 2, e0 + SLAB) if t_u >= PRE_ISSUE else ():
                        row_copy(idx_nxt_ref, 1 - slot, t_u, j).start(priority=j % 2)
                if t_v is not None:
                    v_tile = buf_ref[slot, t_v, e0:e0 + SLAB, SLAB:2 * SLAB, :].reshape(
                        SLAB * SLAB, LANES)
                    part = jnp.dot(wt2[:, c0:c0 + SLAB * SLAB], v_tile, preferred_element_type=F32)
                    r = part if r is None else r + part
            z = jnp.concatenate(z_parts, axis=1) if z_parts else None
            out = None if r is None else r[:SLAB] + r[SLAB:]
            return z, out

        def gate_rows(g, z):
            tile = SLAB * SLAB
            z8 = jnp.concatenate([z, jnp.zeros((8 - tg, z.shape[1]), F32)], axis=0) if tg < 8 else z
            stacked = jnp.concatenate([z8[:, n * tile:(n + 1) * tile]
                                       for n in range(z.shape[1] // tile)], axis=0)
            s_hi, s_lo = _split_bf16(stacked)
            seg = jnp.dot(jnp.concatenate([s_hi, s_lo], axis=0), seg_ref[...],
                          preferred_element_type=F32)
            seg = seg[:stacked.shape[0]] + seg[stacked.shape[0]:]
            act = jnp.concatenate([seg[8 * n:8 * n + tg] for n in range(z.shape[1] // tile)], axis=1)
            gelu = 0.5 * act * (1.0 + jnp.tanh(math.sqrt(2.0 / math.pi)
                                               * (act + 0.044715 * act * act * act)))
            return gate_rep[g * tg:(g + 1) * tg, :] * gelu

        assert GATE_LAG >= tg + 2
        w_reps, outs, zs = {}, {}, []
        for n in range(tb + GATE_LAG):
            t_u = n if n < tb else None
            t_v = n - GATE_LAG if n >= GATE_LAG else None
            w_row = None if t_v is None else w_reps[t_v // tg][t_v % tg:t_v % tg + 1]
            z, out = token_pair(t_u, t_v, w_row)
            if t_u is not None:
                zs.append(z)
            if t_v is not None:
                outs[t_v] = out
            if n % tg == 0 and tg <= n <= tb:
                g = n // tg - 1
                w_reps[g] = gate_rows(g, jnp.concatenate(zs[g * tg:(g + 1) * tg], axis=0))
        for t in range(tb):
            for r in range(SLAB):
                o_ref[t:t + 1, r * LANES:(r + 1) * LANES] = outs[t][r:r + 1, :]

        @pl.when(i == nsteps - 1)
        def _():
            wait_tokens(1 - slot, range(tb))

    @pl.when((i & 1) == 0)
    def _():
        consume(0)

    @pl.when((i & 1) == 1)
    def _():
        consume(1)


def _peer_expert(idx_flat, uv, h_slab, gate, tb):
    t = h_slab.shape[0]
    nsteps = t // tb
    tile = SLAB * SLAB
    seg = (lax.broadcasted_iota(I32, (tile, tile), 0) // SLAB
           == lax.broadcasted_iota(I32, (tile, tile), 1) // SLAB).astype(BF16)
    grpt = (lax.broadcasted_iota(I32, (PEER_SEL, PEER_SEL * SLAB), 0)
            == lax.broadcasted_iota(I32, (PEER_SEL, PEER_SEL * SLAB), 1) // SLAB).astype(BF16)
    blk = tb * PEER_SEL
    return pl.pallas_call(
        functools.partial(_peer_expert_kernel, tb=tb, tg=min(4, tb)),
        grid=(nsteps,),
        in_specs=[pl.BlockSpec((blk,), lambda i: (i,), memory_space=pltpu.SMEM),
                  pl.BlockSpec((blk,), lambda i: (jnp.minimum(i + 1, nsteps - 1),),
                               memory_space=pltpu.SMEM),
                  pl.BlockSpec(memory_space=pl.ANY),
                  pl.BlockSpec((tb, SLAB, LANES), lambda i: (i, 0, 0)),
                  pl.BlockSpec((tb, PEER_SEL), lambda i: (i, 0)),
                  pl.BlockSpec(seg.shape, lambda i: (0, 0)),
                  pl.BlockSpec(grpt.shape, lambda i: (0, 0))],
        out_specs=pl.BlockSpec((tb, SLAB * LANES), lambda i: (i, 0)),
        out_shape=jax.ShapeDtypeStruct((t, SLAB * LANES), F32),
        scratch_shapes=[pltpu.VMEM((2, tb, PEER_SEL, 2 * SLAB, LANES), BF16),
                        pltpu.SemaphoreType.DMA((2, tb))],
        compiler_params=_cparams(("arbitrary",)),
        name="peer_expert",
    )(idx_flat, idx_flat, uv, h_slab, gate, seg, grpt)


def _final_kernel(x1_ref, pe_ref, g2_ref, gn_ref, o_ref):
    x2 = x1_ref[0] + g2_ref[0] * pe_ref[0]
    o_ref[0] = x2 * lax.rsqrt(jnp.mean(x2 * x2, axis=-1, keepdims=True) + EPS) * gn_ref[...]


def _final(x1, pe, g2, gain, tm):
    b, s, d = x1.shape
    tok = pl.BlockSpec((1, tm, d), lambda i, j: (i, j, 0))
    return pl.pallas_call(
        _final_kernel,
        grid=(b, s // tm),
        in_specs=[tok, tok, pl.BlockSpec((1, 1, d), lambda i, j: (i, 0, 0)),
                  pl.BlockSpec((1, d), lambda i, j: (0, 0))],
        out_specs=tok,
        out_shape=jax.ShapeDtypeStruct((b, s, d), F32),
        compiler_params=_cparams(("arbitrary", "arbitrary")),
        name="final_norm",
    )(x1, pe, g2, gain)


def kernel(x, c, ctx, c_ctx, w_ada, b_ada, norm_attn, w_in, diff_lambda_q1, diff_lambda_k1,
           diff_lambda_q2, diff_lambda_k2, diff_norm, swa_sink, swa_norm, w_out, norm_ffn,
           peer_w_q, peer_sub_keys, peer_u, peer_v, final_norm):
    b, s, d = x.shape
    assert w_ada.shape[0] == 1, "single layer only"
    t = b * s

    cc = jnp.zeros((8, d), F32).at[:b].set(c).at[b].set(c_ctx)
    mod = _adaln(cc, w_ada[0], b_ada[0])
    sh1, sc1, g1, sh2, sc2, g2 = [m[:, None, :] for m in jnp.split(mod, 6, axis=-1)]

    w_in_b = w_in[0].astype(BF16)
    gain_attn = norm_attn[0].reshape(1, d)
    cos_t, sin_t = _rope_tables(s)
    tm = min(512, s)
    dvw = DIFF_HEADS * LANES
    w_vt = jnp.concatenate([w_in_b[:, 2 * dvw:3 * dvw], w_in_b[:, -SWA_KV * DH:]], axis=1).T
    p, vt = _inproj(x, sh1[:b], sc1[:b], gain_attn, w_in_b, w_vt, cos_t, sin_t, True, tm)
    lc = ctx.shape[1]
    ones = jnp.ones((lc, LANES), F32)
    ctx_sh = jnp.broadcast_to(sh1[b:b + 1], (b, 1, d))
    ctx_sc = jnp.broadcast_to(sc1[b:b + 1], (b, 1, d))
    pc, vtc = _inproj(ctx, ctx_sh, ctx_sc, gain_attn, w_in_b, w_vt, ones, ones, False, lc)

    lams = [v[0].reshape(1, DH).astype(F32) for v in
            (diff_lambda_q1, diff_lambda_k1, diff_lambda_q2, diff_lambda_k2)]
    a_diff = _diff_attn(p, vt, pc, vtc, lams, diff_norm[0].reshape(1, -1), min(512, s), min(512, s))
    a_swa = _swa_attn(p, vt, pc, vtc, swa_sink[0].astype(F32), swa_norm[0].reshape(1, -1))

    w_out_b = w_out[0].astype(BF16)
    dw = a_diff.shape[2]
    x1, h2 = _outproj(a_diff, a_swa, w_out_b[:dw], w_out_b[dw:], x, g1[:b], sh2[:b], sc2[:b],
                      norm_ffn[0].reshape(1, d), tm)

    idx_t, gate_t = _peer_route(h2.reshape(t, d), peer_w_q[0].astype(BF16),
                                peer_sub_keys[0].astype(BF16), min(256, t))
    uv = jnp.concatenate([peer_u[0].reshape(-1, SLAB, LANES),
                          peer_v[0].reshape(-1, SLAB, LANES)], axis=1).astype(BF16)
    pe = _peer_expert(idx_t.T.reshape(-1), uv, h2.reshape(t, SLAB, LANES), gate_t.T, 16)

    return _final(x1, pe.reshape(b, s, d), g2[:b], final_norm.reshape(1, d), tm)
```

```python
import functools
import math

import jax
import jax.numpy as jnp
from jax import lax
from jax.experimental import pallas as pl
from jax.experimental.pallas import tpu as pltpu

F32 = jnp.float32
BF16 = jnp.bfloat16
I32 = jnp.int32

EPS = 1e-6
NEG_INF = -1e30
ROPE_THETA = 10000.0
GRID_W = 64
ROPE_PAIRS = 16

LANES = 128
DH = 64
DIFF_HEADS = 8
SWA_HEADS = 16
SWA_KV = 4
SWA_GROUP = SWA_HEADS // SWA_KV
BAND = 128
LAM_INIT = 0.8 - 0.6 * math.exp(-0.3 * 0)
LOG2E = math.log2(math.e)

PEER_HEADS = 8
PEER_TOPK = 16
N_KEYS = 128
PEER_SEL = PEER_HEADS * PEER_TOPK
SLAB = 16
PRE_ISSUE = 2
GATE_LAG = 6

VMEM_LIMIT = 56 * 1024 * 1024


def _cparams(sem):
    return pltpu.CompilerParams(dimension_semantics=sem, vmem_limit_bytes=VMEM_LIMIT)


def _adaln_kernel(c_ref, w_ref, b_ref, o_ref):
    c = c_ref[...]
    s = c * (1.0 / (1.0 + jnp.exp(-c)))
    o_ref[...] = jnp.dot(s, w_ref[...], preferred_element_type=F32,
                         precision=lax.Precision.HIGHEST) + b_ref[...]


def _adaln(cc, w, b):
    rows, d = cc.shape
    n = w.shape[1]
    tn = 1024
    return pl.pallas_call(
        _adaln_kernel,
        grid=(n // tn,),
        in_specs=[pl.BlockSpec((rows, d), lambda j: (0, 0)),
                  pl.BlockSpec((d, tn), lambda j: (0, j)),
                  pl.BlockSpec((1, tn), lambda j: (0, j))],
        out_specs=pl.BlockSpec((rows, tn), lambda j: (0, j)),
        out_shape=jax.ShapeDtypeStruct((rows, n), F32),
        compiler_params=_cparams(("arbitrary",)),
        name="adaln",
    )(cc, w, b.reshape(1, n))


def _modnorm(x, gain, shift, scale):
    y = x * lax.rsqrt(jnp.mean(x * x, axis=-1, keepdims=True) + EPS)
    return (y * gain) * (1.0 + scale) + shift


def _swap16(p):
    lane = lax.broadcasted_iota(I32, p.shape, 1)
    up = pltpu.roll(p, LANES - 16, 1)
    dn = pltpu.roll(p, 16, 1)
    return jnp.where((lane & 31) < 16, up, dn)


def _inproj_kernel(x_ref, sh_ref, sc_ref, g_ref, w_ref, wvt_ref, cos_ref, sin_ref, o_ref, vt_ref, *,
                   rope_chunks, chunk_scale):
    h = _modnorm(x_ref[0], g_ref[...], sh_ref[0], sc_ref[0]).astype(BF16)
    p = jnp.dot(h, w_ref[...], preferred_element_type=F32)
    n_chunks = p.shape[1] // LANES
    if rope_chunks:
        cs = cos_ref[...]
        sn = sin_ref[...]
    for j in range(n_chunks):
        pj = p[:, j * LANES:(j + 1) * LANES]
        if j in rope_chunks:
            pj = pj * cs + _swap16(pj) * sn
        if j in chunk_scale:
            pj = pj * chunk_scale[j]
        o_ref[0, :, j * LANES:(j + 1) * LANES] = pj.astype(BF16)
    vt_ref[0] = lax.dot_general(wvt_ref[...], h, (((1,), (1,)), ((), ())),
                                preferred_element_type=F32).astype(BF16)


def _inproj(x, shift, scale, gain, w, w_vt, cos_t, sin_t, rope, tm):
    b, s, d = x.shape
    n = w.shape[1]
    nv = w_vt.shape[0]
    rope_chunks = frozenset(list(range(0, 16)) + list(range(24, 34))) if rope else frozenset()
    chunk_scale = {j: DH ** -0.5 * LOG2E for j in range(0, 8)}
    chunk_scale.update({j: DH ** -0.5 for j in range(24, 32)})
    kern = functools.partial(_inproj_kernel, rope_chunks=rope_chunks, chunk_scale=chunk_scale)
    return pl.pallas_call(
        kern,
        grid=(b, s // tm),
        in_specs=[pl.BlockSpec((1, tm, d), lambda i, j: (i, j, 0)),
                  pl.BlockSpec((1, 1, d), lambda i, j: (i, 0, 0)),
                  pl.BlockSpec((1, 1, d), lambda i, j: (i, 0, 0)),
                  pl.BlockSpec((1, d), lambda i, j: (0, 0)),
                  pl.BlockSpec((d, n), lambda i, j: (0, 0), pipeline_mode=pl.Buffered(1)),
                  pl.BlockSpec((nv, d), lambda i, j: (0, 0), pipeline_mode=pl.Buffered(1)),
                  pl.BlockSpec((tm, LANES), lambda i, j: (j, 0)),
                  pl.BlockSpec((tm, LANES), lambda i, j: (j, 0))],
        out_specs=[pl.BlockSpec((1, tm, n), lambda i, j: (i, j, 0)),
                   pl.BlockSpec((1, nv, tm), lambda i, j: (i, 0, j))],
        out_shape=[jax.ShapeDtypeStruct((b, s, n), BF16), jax.ShapeDtypeStruct((b, nv, s), BF16)],
        compiler_params=_cparams(("arbitrary", "arbitrary")),
        name="in_proj",
    )(x, shift, scale, gain, w, w_vt, cos_t, sin_t)


def _rope_tables(s):
    rows = s // GRID_W
    row = jnp.repeat(jnp.arange(rows, dtype=F32), GRID_W)
    col = jnp.tile(jnp.arange(GRID_W, dtype=F32), rows)
    freqs = ROPE_THETA ** (-jnp.arange(ROPE_PAIRS, dtype=F32) / ROPE_PAIRS)
    ar = row[:, None] * freqs
    ac = col[:, None] * freqs
    cos64 = jnp.concatenate([jnp.cos(ar), jnp.cos(ar), jnp.cos(ac), jnp.cos(ac)], axis=1)
    sin64 = jnp.concatenate([-jnp.sin(ar), jnp.sin(ar), -jnp.sin(ac), jnp.sin(ac)], axis=1)
    return jnp.tile(cos64, (1, 2)), jnp.tile(sin64, (1, 2))


def _head_rms(o, gain):
    return o * lax.rsqrt(jnp.mean(o * o, axis=-1, keepdims=True) + EPS) * gain


def _diff_attn_kernel(lq1_ref, lk1_ref, lq2_ref, lk2_ref, q_ref, k_ref, vt_ref, kc_ref, vtc_ref,
                      gain_ref, o_ref, s_ref, *, tk):
    tq = q_ref.shape[1]
    s_len = k_ref.shape[1]
    q = q_ref[0]
    lane = lax.broadcasted_iota(I32, q.shape, 1)
    zero = jnp.zeros_like(q)
    q_maps = (jnp.where(lane < DH, q, zero), jnp.where(lane >= DH, q, zero))
    lc = kc_ref.shape[1]
    blocks = [(0, lc, lambda: kc_ref[0], lambda: vtc_ref[0])]
    blocks += [(lc + j * tk, tk, lambda j=j: k_ref[0, j * tk:(j + 1) * tk, :],
                lambda j=j: vt_ref[0, :, j * tk:(j + 1) * tk]) for j in range(s_len // tk)]

    def score_block(i, blk, m):
        r0, rows, kblk, _ = blk
        s = lax.dot_general(kblk(), q_maps[i], (((1,), (1,)), ((), ())),
                            preferred_element_type=F32)
        s_ref[i, r0:r0 + rows, :] = s
        return jnp.maximum(m, jnp.max(s, axis=0, keepdims=True))

    def prob_block(i, blk, m, l, acc):
        r0, rows, _, vtblk = blk
        p = jnp.exp2(s_ref[i, r0:r0 + rows, :] - m)
        part = jnp.dot(vtblk(), p.astype(BF16), preferred_element_type=F32)
        return l + jnp.sum(p, axis=0, keepdims=True), part if acc is None else acc + part

    m_init = jnp.full((1, tq), NEG_INF, F32)
    l_init = jnp.zeros((1, tq), F32)
    m1 = m_init
    for blk in blocks:
        m1 = score_block(0, blk, m1)
    m2, l1, acc1 = m_init, l_init, None
    for blk in blocks:
        m2 = score_block(1, blk, m2)
        l1, acc1 = prob_block(0, blk, m1, l1, acc1)
    l2, acc2 = l_init, None
    for blk in blocks:
        l2, acc2 = prob_block(1, blk, m2, l2, acc2)

    lam = (jnp.exp(jnp.sum(lq1_ref[...] * lk1_ref[...], keepdims=True))
           - jnp.exp(jnp.sum(lq2_ref[...] * lk2_ref[...], keepdims=True)) + LAM_INIT)
    out = (acc1 / l1 - lam * (acc2 / l2)).T
    o_ref[0] = (_head_rms(out, gain_ref[...]) * (1.0 - LAM_INIT)).astype(BF16)


def _diff_attn(p, vt, pc, vtc, lams, gain, tq, tk):
    b, s, _ = p.shape
    lc = pc.shape[1]
    lam_spec = pl.BlockSpec((1, DH), lambda i, h, j: (0, 0))
    return pl.pallas_call(
        functools.partial(_diff_attn_kernel, tk=tk),
        grid=(b, DIFF_HEADS, s // tq),
        in_specs=[lam_spec, lam_spec, lam_spec, lam_spec,
                  pl.BlockSpec((1, tq, LANES), lambda i, h, j: (i, j, h)),
                  pl.BlockSpec((1, s, LANES), lambda i, h, j: (i, 0, 8 + h)),
                  pl.BlockSpec((1, LANES, s), lambda i, h, j: (i, h, 0)),
                  pl.BlockSpec((1, lc, LANES), lambda i, h, j: (i, 0, 8 + h)),
                  pl.BlockSpec((1, LANES, lc), lambda i, h, j: (i, h, 0)),
                  pl.BlockSpec((1, LANES), lambda i, h, j: (0, h))],
        out_specs=pl.BlockSpec((1, tq, LANES), lambda i, h, j: (i, j, h)),
        out_shape=jax.ShapeDtypeStruct((b, s, DIFF_HEADS * LANES), BF16),
        scratch_shapes=[pltpu.VMEM((2, lc + s, tq), F32)],
        compiler_params=_cparams(("arbitrary", "arbitrary", "arbitrary")),
        name="diff_attn",
    )(*lams, p, p, vt, pc, vtc, gain)


def _swa_attn_kernel(sink_ref, q_ref, k_ref, vt_ref, kc_ref, vtc_ref, gain_ref, o_ref,
                     kcat_ref, vtcat_ref):
    n = pl.program_id(1)
    nb = pl.num_programs(1)
    s_len = k_ref.shape[1]
    lc = kc_ref.shape[1]
    n_band = 3 * BAND

    @pl.when(n == 0)
    def _():
        kcat_ref[n_band:, :] = kc_ref[0]
        vtcat_ref[:, n_band:] = vtc_ref[0]

    prev = pl.multiple_of(jnp.maximum(n - 1, 0) * BAND, BAND)
    cur = pl.multiple_of(n * BAND, BAND)
    nxt = pl.multiple_of(jnp.minimum(n + 1, nb - 1) * BAND, BAND)
    for t, off in enumerate((prev, cur, nxt)):
        kcat_ref[t * BAND:(t + 1) * BAND, :] = k_ref[0, pl.ds(off, BAND), :]
        vtcat_ref[:, t * BAND:(t + 1) * BAND] = vt_ref[0, :, pl.ds(off, BAND)]

    keys = n_band + lc
    cols = SWA_GROUP * BAND
    ki = lax.broadcasted_iota(I32, (keys, cols), 0)
    qi = lax.broadcasted_iota(I32, (keys, cols), 1) & (BAND - 1)
    kpos = (n - 1) * BAND + ki
    in_band = jnp.where(jnp.abs(qi + BAND - ki) <= BAND,
                        jnp.where(kpos >= 0, jnp.where(kpos < s_len, 1, 0), 0), 0)
    ok = jnp.where(ki >= n_band, 1, in_band) > 0

    q = q_ref[0]
    outs = []
    for g in range(SWA_KV):
        kg = kcat_ref[:, g * DH:(g + 1) * DH]
        vtg = vtcat_ref[g * DH:(g + 1) * DH, :]
        qg = jnp.concatenate(
            [q[:, (g * SWA_GROUP + j) * DH:(g * SWA_GROUP + j + 1) * DH] for j in range(SWA_GROUP)],
            axis=0)
        s = lax.dot_general(kg, qg, (((1,), (1,)), ((), ())), preferred_element_type=F32)
        s = jnp.where(ok, s, NEG_INF)
        sink = jnp.concatenate(
            [jnp.full((1, BAND), sink_ref[g * SWA_GROUP + j], F32) for j in range(SWA_GROUP)],
            axis=1)
        m = jnp.maximum(jnp.max(s, axis=0, keepdims=True), sink)
        e = jnp.exp(s - m)
        denom = jnp.sum(e, axis=0, keepdims=True) + jnp.exp(sink - m)
        o = jnp.dot(vtg, e.astype(BF16), preferred_element_type=F32) / denom
        outs.append(o * lax.rsqrt(jnp.mean(o * o, axis=0, keepdims=True) + EPS))
    o_all = jnp.concatenate(outs, axis=0).T
    for g in range(SWA_KV):
        for j in range(SWA_GROUP):
            c0 = (g * SWA_GROUP + j) * DH
            o_ref[0, :, c0:c0 + DH] = (o_all[j * BAND:(j + 1) * BAND, g * DH:(g + 1) * DH]
                                       * gain_ref[:, c0:c0 + DH]).astype(BF16)


def _swa_attn(p, vt, pc, vtc, sink, gain):
    b, s, _ = p.shape
    lc = pc.shape[1]
    kvw = SWA_KV * DH
    vblk = vt.shape[1] // kvw - 1
    return pl.pallas_call(
        _swa_attn_kernel,
        grid=(b, s // BAND),
        in_specs=[pl.BlockSpec(memory_space=pltpu.SMEM),
                  pl.BlockSpec((1, BAND, SWA_HEADS * DH), lambda i, j: (i, j, 3)),
                  pl.BlockSpec((1, s, kvw), lambda i, j: (i, 0, 16)),
                  pl.BlockSpec((1, kvw, s), lambda i, j: (i, vblk, 0)),
                  pl.BlockSpec((1, lc, kvw), lambda i, j: (i, 0, 16)),
                  pl.BlockSpec((1, kvw, lc), lambda i, j: (i, vblk, 0)),
                  pl.BlockSpec((1, SWA_HEADS * DH), lambda i, j: (0, 0))],
        out_specs=pl.BlockSpec((1, BAND, SWA_HEADS * DH), lambda i, j: (i, j, 0)),
        out_shape=jax.ShapeDtypeStruct((b, s, SWA_HEADS * DH), BF16),
        scratch_shapes=[pltpu.VMEM((3 * BAND + lc, kvw), BF16),
                        pltpu.VMEM((kvw, 3 * BAND + lc), BF16)],
        compiler_params=_cparams(("arbitrary", "arbitrary")),
        name="swa_attn",
    )(sink, p, p, vt, pc, vtc, gain)


def _outproj_kernel(ad_ref, as_ref, wd_ref, ws_ref, x_ref, g1_ref, sh_ref, sc_ref, gn_ref,
                    x1_ref, h2_ref):
    a = (jnp.dot(ad_ref[0], wd_ref[...], preferred_element_type=F32)
         + jnp.dot(as_ref[0], ws_ref[...], preferred_element_type=F32))
    x1 = x_ref[0] + g1_ref[0] * a
    x1_ref[0] = x1
    h2_ref[0] = _modnorm(x1, gn_ref[...], sh_ref[0], sc_ref[0]).astype(BF16)


def _outproj(a_diff, a_swa, w_d, w_s, x, g1, sh2, sc2, gain, tm):
    b, s, d = x.shape
    wd = a_diff.shape[2]
    vec = pl.BlockSpec((1, 1, d), lambda i, j: (i, 0, 0))
    return pl.pallas_call(
        _outproj_kernel,
        grid=(b, s // tm),
        in_specs=[pl.BlockSpec((1, tm, wd), lambda i, j: (i, j, 0)),
                  pl.BlockSpec((1, tm, wd), lambda i, j: (i, j, 0)),
                  pl.BlockSpec((wd, d), lambda i, j: (0, 0)),
                  pl.BlockSpec((wd, d), lambda i, j: (0, 0)),
                  pl.BlockSpec((1, tm, d), lambda i, j: (i, j, 0)),
                  vec, vec, vec,
                  pl.BlockSpec((1, d), lambda i, j: (0, 0))],
        out_specs=[pl.BlockSpec((1, tm, d), lambda i, j: (i, j, 0)),
                   pl.BlockSpec((1, tm, d), lambda i, j: (i, j, 0))],
        out_shape=[jax.ShapeDtypeStruct((b, s, d), F32), jax.ShapeDtypeStruct((b, s, d), BF16)],
        compiler_params=_cparams(("arbitrary", "arbitrary")),
        name="out_proj",
    )(a_diff, a_swa, w_d, w_s, x, g1, sh2, sc2, gain)


def _extract_topk(s, rank, payload, k):
    vals, pays = [], []
    for _ in range(k):
        m = jnp.max(s, axis=0, keepdims=True)
        first = jnp.min(jnp.where(s == m, rank, jnp.inf), axis=0, keepdims=True)
        hit = rank == first
        vals.append(m)
        pays.append(first if payload is None
                    else jnp.sum(jnp.where(hit, payload, 0.0), axis=0, keepdims=True))
        s = jnp.where(hit, -jnp.inf, s)
    return jnp.concatenate(vals, axis=0), jnp.concatenate(pays, axis=0)


def _extract_topk_paired(s, k):
    half = s.shape[0] // 2
    a, b = s[:half], s[half:]
    ra = lax.broadcasted_iota(I32, a.shape, 0).astype(F32)
    rb = ra + float(half)
    b_wins = b > a
    w, l = jnp.where(b_wins, b, a), jnp.where(b_wins, a, b)
    rw, rl = jnp.where(b_wins, rb, ra), jnp.where(b_wins, ra, rb)
    vals, ids = [], []
    for _ in range(k):
        m = jnp.max(w, axis=0, keepdims=True)
        first = jnp.min(jnp.where(w == m, rw, jnp.inf), axis=0, keepdims=True)
        hit = rw == first
        vals.append(m)
        ids.append(first)
        w = jnp.where(hit, l, w)
        rw = jnp.where(hit, rl, rw)
        l = jnp.where(hit, -jnp.inf, l)
    return jnp.concatenate(vals, axis=0), jnp.concatenate(ids, axis=0)


def _pair_candidates(s1, i1, s2, i2):
    k, tt = s1.shape
    assert k == 16, "block layout below is written for 16 x 16 pairs"
    sub = lax.broadcasted_iota(I32, (8, tt), 0)
    vals = [s1[0:1] + s2, s1[1:2] + s2[0:8]]
    pos = [lax.broadcasted_iota(I32, (k, tt), 0), k + sub]
    ids = [i1[0:1] * N_KEYS + i2, i1[1:2] * N_KEYS + i2[0:8]]
    for a in range(2, 8):
        keep = sub < k // (a + 1)
        vals.append(jnp.where(keep, s1[a:a + 1] + s2[0:8], -jnp.inf))
        pos.append(a * k + sub)
        ids.append(i1[a:a + 1] * N_KEYS + i2[0:8])
    vals.append(s1[8:16] + s2[0:1])
    pos.append((sub + 8) * k)
    ids.append(i1[8:16] * N_KEYS + i2[0:1])
    return (jnp.concatenate(vals, axis=0), jnp.concatenate(pos, axis=0).astype(F32),
            jnp.concatenate(ids, axis=0))


def _peer_route_kernel(h_ref, wq_ref, keys_ref, idx_ref, gate_ref):
    q = jnp.dot(h_ref[...], wq_ref[...], preferred_element_type=F32).astype(BF16)
    for h in range(PEER_HEADS):
        halves = []
        for i in range(2):
            c0 = (h * 2 + i) * N_KEYS
            s = lax.dot_general(keys_ref[h, i], q[:, c0:c0 + N_KEYS], (((1,), (1,)), ((), ())),
                                preferred_element_type=F32)
            halves.append(_extract_topk_paired(s, PEER_TOPK))
        (s1, i1), (s2, i2) = halves
        cand, pos, cidx = _pair_candidates(s1, i1, s2, i2)
        top_s, top_i = _extract_topk(cand, pos, cidx, PEER_TOPK)
        e = jnp.exp(top_s - jnp.max(top_s, axis=0, keepdims=True))
        r0 = h * PEER_TOPK
        idx_ref[r0:r0 + PEER_TOPK, :] = top_i.astype(I32)
        gate_ref[r0:r0 + PEER_TOPK, :] = e / jnp.sum(e, axis=0, keepdims=True)


def _peer_route(h2, wq, keys, tt):
    t, d = h2.shape
    nq = wq.shape[1]
    return pl.pallas_call(
        _peer_route_kernel,
        grid=(t // tt,),
        in_specs=[pl.BlockSpec((tt, d), lambda i: (i, 0)),
                  pl.BlockSpec((d, nq), lambda i: (0, 0)),
                  pl.BlockSpec(keys.shape, lambda i: (0, 0, 0, 0))],
        out_specs=[pl.BlockSpec((PEER_SEL, tt), lambda i: (0, i)),
                   pl.BlockSpec((PEER_SEL, tt), lambda i: (0, i))],
        out_shape=[jax.ShapeDtypeStruct((PEER_SEL, t), I32), jax.ShapeDtypeStruct((PEER_SEL, t), F32)],
        compiler_params=_cparams(("arbitrary",)),
        name="peer_route",
    )(h2, wq, keys)


def _split_bf16(x):
    hi = x.astype(BF16)
    return hi, (x - hi.astype(F32)).astype(BF16)


def _peer_expert_kernel(idx_cur_ref, idx_nxt_ref, uv_ref, h_ref, gate_ref, seg_ref, grpt_ref,
                        o_ref, buf_ref, sem_ref, *, tb, tg):
    i = pl.program_id(0)
    nsteps = pl.num_programs(0)

    def row_copy(idx_ref, slot, t, j):
        return pltpu.make_async_copy(uv_ref.at[idx_ref[t * PEER_SEL + j]],
                                     buf_ref.at[slot, t, j], sem_ref.at[slot, t])

    def wait_tokens(slot, toks):
        for t in toks:
            for j in range(PEER_SEL):
                row_copy(idx_cur_ref, slot, t, j).wait()

    @pl.when(i == 0)
    def _():
        for t in range(tb):
            for j in range(PEER_SEL):
                row_copy(idx_cur_ref, 0, t, j).start(priority=j % 2)

    diag = (lax.broadcasted_iota(I32, (SLAB, PEER_SEL * SLAB), 1) & (SLAB - 1)) == \
        lax.broadcasted_iota(I32, (SLAB, PEER_SEL * SLAB), 0)

    def consume(slot):
        for t in range(PRE_ISSUE):
            for j in range(PEER_SEL):
                row_copy(idx_nxt_ref, 1 - slot, t, j).start(priority=j % 2)
        wait_tokens(slot, range(tb))
        g_hi, g_lo = _split_bf16(gate_ref[...])
        gate_rep = (jnp.dot(g_hi, grpt_ref[...], preferred_element_type=F32)
                    + jnp.dot(g_lo, grpt_ref[...], preferred_element_type=F32))

        def token_pair(t_u, t_v, w_row):
            z_parts, r = [], None
            if t_v is not None:
                wt = jnp.where(diag, jnp.broadcast_to(w_row, diag.shape), 0.0)
                wt_hi, wt_lo = _split_bf16(wt)
                wt2 = jnp.concatenate([wt_hi, wt_lo], axis=0)
            for n in range(PEER_SEL // SLAB):
                e0, c0 = n * SLAB, n * SLAB * SLAB
                if t_u is not None:
                    for j in range(e0, e0 + SLAB // 2) if t_u >= PRE_ISSUE else ():
                        row_copy(idx_nxt_ref, 1 - slot, t_u, j).start(priority=j % 2)
                    u_tile = buf_ref[slot, t_u, e0:e0 + SLAB, 0:SLAB, :].reshape(SLAB * SLAB, LANES)
                    y = lax.dot_general(h_ref[t_u], u_tile, (((1,), (1,)), ((), ())),
                                        preferred_element_type=F32)
                    z_parts.append(jnp.sum(jnp.where(diag[:, :SLAB * SLAB], y, 0.0),
                                           axis=0, keepdims=True))
                    for j in range(e0 + SLAB // 2, e0 + SLAB) if t_u >= PRE_ISSUE else ():
                        row_copy(idx_nxt_ref, 1 - slot, t_u, j).start(priority=j % 2)
                if t_v is not None:
                    v_tile = buf_ref[slot, t_v, e0:e0 + SLAB, SLAB:2 * SLAB, :].reshape(
                        SLAB * SLAB, LANES)
                    part = jnp.dot(wt2[:, c0:c0 + SLAB * SLAB], v_tile, preferred_element_type=F32)
                    r = part if r is None else r + part
            z = jnp.concatenate(z_parts, axis=1) if z_parts else None
            out = None if r is None else r[:SLAB] + r[SLAB:]
            return z, out

        def gate_rows(g, z):
            tile = SLAB * SLAB
            z8 = jnp.concatenate([z, jnp.zeros((8 - tg, z.shape[1]), F32)], axis=0) if tg < 8 else z
            stacked = jnp.concatenate([z8[:, n * tile:(n + 1) * tile]
                                       for n in range(z.shape[1] // tile)], axis=0)
            s_hi, s_lo = _split_bf16(stacked)
            seg = jnp.dot(jnp.concatenate([s_hi, s_lo], axis=0), seg_ref[...],
                          preferred_element_type=F32)
            seg = seg[:stacked.shape[0]] + seg[stacked.shape[0]:]
            act = jnp.concatenate([seg[8 * n:8 * n + tg] for n in range(z.shape[1] // tile)], axis=1)
            gelu = 0.5 * act * (1.0 + jnp.tanh(math.sqrt(2.0 / math.pi)
                                               * (act + 0.044715 * act * act * act)))
            return gate_rep[g * tg:(g + 1) * tg, :] * gelu

        assert GATE_LAG >= tg + 2
        w_reps, outs, zs = {}, {}, []
        for n in range(tb + GATE_LAG):
            t_u = n if n < tb else None
            t_v = n - GATE_LAG if n >= GATE_LAG else None
            w_row = None if t_v is None else w_reps[t_v // tg][t_v % tg:t_v % tg + 1]
            z, out = token_pair(t_u, t_v, w_row)
            if t_u is not None:
                zs.append(z)
            if t_v is not None:
                outs[t_v] = out
            if n % tg == 0 and tg <= n <= tb:
                g = n // tg - 1
                w_reps[g] = gate_rows(g, jnp.concatenate(zs[g * tg:(g + 1) * tg], axis=0))
        for t in range(tb):
            for r in range(SLAB):
                o_ref[t:t + 1, r * LANES:(r + 1) * LANES] = outs[t][r:r + 1, :]

        @pl.when(i == nsteps - 1)
        def _():
            wait_tokens(1 - slot, range(tb))

    @pl.when((i & 1) == 0)
    def _():
        consume(0)

    @pl.when((i & 1) == 1)
    def _():
        consume(1)


def _peer_expert(idx_flat, uv, h_slab, gate, tb):
    t = h_slab.shape[0]
    nsteps = t // tb
    tile = SLAB * SLAB
    seg = (lax.broadcasted_iota(I32, (tile, tile), 0) // SLAB
           == lax.broadcasted_iota(I32, (tile, tile), 1) // SLAB).astype(BF16)
    grpt = (lax.broadcasted_iota(I32, (PEER_SEL, PEER_SEL * SLAB), 0)
            == lax.broadcasted_iota(I32, (PEER_SEL, PEER_SEL * SLAB), 1) // SLAB).astype(BF16)
    blk = tb * PEER_SEL
    return pl.pallas_call(
        functools.partial(_peer_expert_kernel, tb=tb, tg=min(4, tb)),
        grid=(nsteps,),
        in_specs=[pl.BlockSpec((blk,), lambda i: (i,), memory_space=pltpu.SMEM),
                  pl.BlockSpec((blk,), lambda i: (jnp.minimum(i + 1, nsteps - 1),),
                               memory_space=pltpu.SMEM),
                  pl.BlockSpec(memory_space=pl.ANY),
                  pl.BlockSpec((tb, SLAB, LANES), lambda i: (i, 0, 0)),
                  pl.BlockSpec((tb, PEER_SEL), lambda i: (i, 0)),
                  pl.BlockSpec(seg.shape, lambda i: (0, 0)),
                  pl.BlockSpec(grpt.shape, lambda i: (0, 0))],
        out_specs=pl.BlockSpec((tb, SLAB * LANES), lambda i: (i, 0)),
        out_shape=jax.ShapeDtypeStruct((t, SLAB * LANES), F32),
        scratch_shapes=[pltpu.VMEM((2, tb, PEER_SEL, 2 * SLAB, LANES), BF16),
                        pltpu.SemaphoreType.DMA((2, tb))],
        compiler_params=_cparams(("arbitrary",)),
        name="peer_expert",
    )(idx_flat, idx_flat, uv, h_slab, gate, seg, grpt)


def _final_kernel(x1_ref, pe_ref, g2_ref, gn_ref, o_ref):
    x2 = x1_ref[0] + g2_ref[0] * pe_ref[0]
    o_ref[0] = x2 * lax.rsqrt(jnp.mean(x2 * x2, axis=-1, keepdims=True) + EPS) * gn_ref[...]


def _final(x1, pe, g2, gain, tm):
    b, s, d = x1.shape
    tok = pl.BlockSpec((1, tm, d), lambda i, j: (i, j, 0))
    return pl.pallas_call(
        _final_kernel,
        grid=(b, s // tm),
        in_specs=[tok, tok, pl.BlockSpec((1, 1, d), lambda i, j: (i, 0, 0)),
                  pl.BlockSpec((1, d), lambda i, j: (0, 0))],
        out_specs=tok,
        out_shape=jax.ShapeDtypeStruct((b, s, d), F32),
        compiler_params=_cparams(("arbitrary", "arbitrary")),
        name="final_norm",
    )(x1, pe, g2, gain)


def kernel(x, c, ctx, c_ctx, w_ada, b_ada, norm_attn, w_in, diff_lambda_q1, diff_lambda_k1,
           diff_lambda_q2, diff_lambda_k2, diff_norm, swa_sink, swa_norm, w_out, norm_ffn,
           peer_w_q, peer_sub_keys, peer_u, peer_v, final_norm):
    b, s, d = x.shape
    assert w_ada.shape[0] == 1, "single layer only"
    t = b * s

    cc = jnp.zeros((8, d), F32).at[:b].set(c).at[b].set(c_ctx)
    mod = _adaln(cc, w_ada[0], b_ada[0])
    sh1, sc1, g1, sh2, sc2, g2 = [m[:, None, :] for m in jnp.split(mod, 6, axis=-1)]

    w_in_b = w_in[0].astype(BF16)
    gain_attn = norm_attn[0].reshape(1, d)
    cos_t, sin_t = _rope_tables(s)
    tm = min(512, s)
    dvw = DIFF_HEADS * LANES
    w_vt = jnp.concatenate([w_in_b[:, 2 * dvw:3 * dvw], w_in_b[:, -SWA_KV * DH:]], axis=1).T
    p, vt = _inproj(x, sh1[:b], sc1[:b], gain_attn, w_in_b, w_vt, cos_t, sin_t, True, tm)
    lc = ctx.shape[1]
    ones = jnp.ones((lc, LANES), F32)
    ctx_sh = jnp.broadcast_to(sh1[b:b + 1], (b, 1, d))
    ctx_sc = jnp.broadcast_to(sc1[b:b + 1], (b, 1, d))
    pc, vtc = _inproj(ctx, ctx_sh, ctx_sc, gain_attn, w_in_b, w_vt, ones, ones, False, lc)

    lams = [v[0].reshape(1, DH).astype(F32) for v in
            (diff_lambda_q1, diff_lambda_k1, diff_lambda_q2, diff_lambda_k2)]
    a_diff = _diff_attn(p, vt, pc, vtc, lams, diff_norm[0].reshape(1, -1), min(512, s), min(512, s))
    a_swa = _swa_attn(p, vt, pc, vtc, swa_sink[0].astype(F32), swa_norm[0].reshape(1, -1))

    w_out_b = w_out[0].astype(BF16)
    dw = a_diff.shape[2]
    x1, h2 = _outproj(a_diff, a_swa, w_out_b[:dw], w_out_b[dw:], x, g1[:b], sh2[:b], sc2[:b],
                      norm_ffn[0].reshape(1, d), tm)

    idx_t, gate_t = _peer_route(h2.reshape(t, d), peer_w_q[0].astype(BF16),
                                peer_sub_keys[0].astype(BF16), min(256, t))
    uv = jnp.concatenate([peer_u[0].reshape(-1, SLAB, LANES),
                          peer_v[0].reshape(-1, SLAB, LANES)], axis=1).astype(BF16)
    pe = _peer_expert(idx_t.T.reshape(-1), uv, h2.reshape(t, SLAB, LANES), gate_t.T, 16)

    return _final(x1, pe.reshape(b, s, d), g2[:b], final_norm.reshape(1, d), tm)
```

```python
import functools
import math

import jax
import jax.numpy as jnp
from jax import lax
from jax.experimental import pallas as pl
from jax.experimental.pallas import tpu as pltpu

F32 = jnp.float32
BF16 = jnp.bfloat16
I32 = jnp.int32

EPS = 1e-6
NEG_INF = -1e30
ROPE_THETA = 10000.0
GRID_W = 64
ROPE_PAIRS = 16

LANES = 128
DH = 64
DIFF_HEADS = 8
SWA_HEADS = 16
SWA_KV = 4
SWA_GROUP = SWA_HEADS // SWA_KV
BAND = 128
QK_CHUNKS = 26
LAM_INIT = 0.8 - 0.6 * math.exp(-0.3 * 0)
LOG2E = math.log2(math.e)

PEER_HEADS = 8
PEER_TOPK = 16
N_KEYS = 128
PEER_SEL = PEER_HEADS * PEER_TOPK
SLAB = 16
PRE_ISSUE = 2
GATE_LAG = 6

VMEM_LIMIT = 56 * 1024 * 1024


def _cparams(sem):
    return pltpu.CompilerParams(dimension_semantics=sem, vmem_limit_bytes=VMEM_LIMIT)


def _adaln_kernel(c_ref, w_ref, b_ref, o_ref):
    c = c_ref[...]
    s = c * (1.0 / (1.0 + jnp.exp(-c)))
    rows = s.shape[0]
    s_hi, s_lo = _split_bf16(s)
    w_hi, w_lo = _split_bf16(w_ref[...])
    both = jnp.dot(jnp.concatenate([s_hi, s_lo], axis=0), w_hi, preferred_element_type=F32)
    o_ref[...] = (both[:rows] + both[rows:] + jnp.dot(s_hi, w_lo, preferred_element_type=F32)
                  + b_ref[...])


def _adaln(cc, w, b):
    rows, d = cc.shape
    n = w.shape[1]
    tn = 1024
    return pl.pallas_call(
        _adaln_kernel,
        grid=(n // tn,),
        in_specs=[pl.BlockSpec((rows, d), lambda j: (0, 0)),
                  pl.BlockSpec((d, tn), lambda j: (0, j)),
                  pl.BlockSpec((1, tn), lambda j: (0, j))],
        out_specs=pl.BlockSpec((rows, tn), lambda j: (0, j)),
        out_shape=jax.ShapeDtypeStruct((rows, n), F32),
        compiler_params=_cparams(("arbitrary",)),
        name="adaln",
    )(cc, w, b.reshape(1, n))


def _modnorm(x, gain, shift, scale):
    y = x * lax.rsqrt(jnp.mean(x * x, axis=-1, keepdims=True) + EPS)
    return (y * gain) * (1.0 + scale) + shift


def _swap16(p):
    lane = lax.broadcasted_iota(I32, p.shape, 1)
    up = pltpu.roll(p, LANES - 16, 1)
    dn = pltpu.roll(p, 16, 1)
    return jnp.where((lane & 31) < 16, up, dn)


def _inproj_kernel(x_ref, sh_ref, sc_ref, g_ref, w_ref, wvt_ref, cos_ref, sin_ref, o_ref, vt_ref, *,
                   rope_chunks, chunk_scale):
    h = _modnorm(x_ref[0], g_ref[...], sh_ref[0], sc_ref[0]).astype(BF16)
    p = jnp.dot(h, w_ref[...], preferred_element_type=F32)
    n_chunks = p.shape[1] // LANES
    if rope_chunks:
        cs = cos_ref[...]
        sn = sin_ref[...]
    for j in range(n_chunks):
        pj = p[:, j * LANES:(j + 1) * LANES]
        if j in rope_chunks:
            pj = pj * cs + _swap16(pj) * sn
        if j in chunk_scale:
            pj = pj * chunk_scale[j]
        o_ref[0, :, j * LANES:(j + 1) * LANES] = pj.astype(BF16)
    vt_ref[0] = lax.dot_general(wvt_ref[...], h, (((1,), (1,)), ((), ())),
                                preferred_element_type=F32).astype(BF16)


def _inproj(x, shift, scale, gain, w, w_vt, cos_t, sin_t, rope, tm):
    b, s, d = x.shape
    n = w.shape[1]
    nv = w_vt.shape[0]
    assert n == QK_CHUNKS * LANES
    rope_chunks = frozenset(range(QK_CHUNKS)) if rope else frozenset()
    chunk_scale = {j: DH ** -0.5 * LOG2E for j in range(0, 8)}
    chunk_scale.update({j: DH ** -0.5 for j in range(16, 24)})
    kern = functools.partial(_inproj_kernel, rope_chunks=rope_chunks, chunk_scale=chunk_scale)
    return pl.pallas_call(
        kern,
        grid=(b, s // tm),
        in_specs=[pl.BlockSpec((1, tm, d), lambda i, j: (i, j, 0)),
                  pl.BlockSpec((1, 1, d), lambda i, j: (i, 0, 0)),
                  pl.BlockSpec((1, 1, d), lambda i, j: (i, 0, 0)),
                  pl.BlockSpec((1, d), lambda i, j: (0, 0)),
                  pl.BlockSpec((d, n), lambda i, j: (0, 0), pipeline_mode=pl.Buffered(1)),
                  pl.BlockSpec((nv, d), lambda i, j: (0, 0), pipeline_mode=pl.Buffered(1)),
                  pl.BlockSpec((tm, LANES), lambda i, j: (j, 0)),
                  pl.BlockSpec((tm, LANES), lambda i, j: (j, 0))],
        out_specs=[pl.BlockSpec((1, tm, n), lambda i, j: (i, j, 0)),
                   pl.BlockSpec((1, nv, tm), lambda i, j: (i, 0, j))],
        out_shape=[jax.ShapeDtypeStruct((b, s, n), BF16), jax.ShapeDtypeStruct((b, nv, s), BF16)],
        compiler_params=_cparams(("arbitrary", "arbitrary")),
        name="in_proj",
    )(x, shift, scale, gain, w, w_vt, cos_t, sin_t)


def _rope_tables(s):
    rows = s // GRID_W
    row = jnp.repeat(jnp.arange(rows, dtype=F32), GRID_W)
    col = jnp.tile(jnp.arange(GRID_W, dtype=F32), rows)
    freqs = ROPE_THETA ** (-jnp.arange(ROPE_PAIRS, dtype=F32) / ROPE_PAIRS)
    ar = row[:, None] * freqs
    ac = col[:, None] * freqs
    cos64 = jnp.concatenate([jnp.cos(ar), jnp.cos(ar), jnp.cos(ac), jnp.cos(ac)], axis=1)
    sin64 = jnp.concatenate([-jnp.sin(ar), jnp.sin(ar), -jnp.sin(ac), jnp.sin(ac)], axis=1)
    return jnp.tile(cos64, (1, 2)), jnp.tile(sin64, (1, 2))


def _head_rms(o, gain):
    return o * lax.rsqrt(jnp.mean(o * o, axis=-1, keepdims=True) + EPS) * gain


def _diff_attn_kernel(lq1_ref, lk1_ref, lq2_ref, lk2_ref, q_ref, k_ref, vt_ref, kc_ref, vtc_ref,
                      gain_ref, o_ref, s_ref, *, tk):
    tq = q_ref.shape[1]
    s_len = k_ref.shape[1]
    q = q_ref[0]
    lane = lax.broadcasted_iota(I32, q.shape, 1)
    zero = jnp.zeros_like(q)
    q_maps = (jnp.where(lane < DH, q, zero), jnp.where(lane >= DH, q, zero))
    lc = kc_ref.shape[1]
    blocks = [(0, lc, lambda: kc_ref[0], lambda: vtc_ref[0])]
    blocks += [(lc + j * tk, tk, lambda j=j: k_ref[0, j * tk:(j + 1) * tk, :],
                lambda j=j: vt_ref[0, :, j * tk:(j + 1) * tk]) for j in range(s_len // tk)]

    def score_block(i, blk, m):
        r0, rows, kblk, _ = blk
        s = lax.dot_general(kblk(), q_maps[i], (((1,), (1,)), ((), ())),
                            preferred_element_type=F32)
        s_ref[i, r0:r0 + rows, :] = s
        return jnp.maximum(m, jnp.max(s, axis=0, keepdims=True))

    def prob_block(i, blk, m, l, acc):
        r0, rows, _, vtblk = blk
        p = jnp.exp2(s_ref[i, r0:r0 + rows, :] - m)
        part = jnp.dot(vtblk(), p.astype(BF16), preferred_element_type=F32)
        return l + jnp.sum(p, axis=0, keepdims=True), part if acc is None else acc + part

    m_init = jnp.full((1, tq), NEG_INF, F32)
    l_init = jnp.zeros((1, tq), F32)
    m1 = m_init
    for blk in blocks:
        m1 = score_block(0, blk, m1)
    m2, l1, acc1 = m_init, l_init, None
    for blk in blocks:
        m2 = score_block(1, blk, m2)
        l1, acc1 = prob_block(0, blk, m1, l1, acc1)
    l2, acc2 = l_init, None
    for blk in blocks:
        l2, acc2 = prob_block(1, blk, m2, l2, acc2)

    lam = (jnp.exp(jnp.sum(lq1_ref[...] * lk1_ref[...], keepdims=True))
           - jnp.exp(jnp.sum(lq2_ref[...] * lk2_ref[...], keepdims=True)) + LAM_INIT)
    out = (acc1 / l1 - lam * (acc2 / l2)).T
    o_ref[0] = (_head_rms(out, gain_ref[...]) * (1.0 - LAM_INIT)).astype(BF16)


def _diff_attn(p, vt, pc, vtc, lams, gain, tq, tk):
    b, s, _ = p.shape
    lc = pc.shape[1]
    lam_spec = pl.BlockSpec((1, DH), lambda i, h, j: (0, 0))
    return pl.pallas_call(
        functools.partial(_diff_attn_kernel, tk=tk),
        grid=(b, DIFF_HEADS, s // tq),
        in_specs=[lam_spec, lam_spec, lam_spec, lam_spec,
                  pl.BlockSpec((1, tq, LANES), lambda i, h, j: (i, j, h)),
                  pl.BlockSpec((1, s, LANES), lambda i, h, j: (i, 0, 8 + h)),
                  pl.BlockSpec((1, LANES, s), lambda i, h, j: (i, h, 0)),
                  pl.BlockSpec((1, lc, LANES), lambda i, h, j: (i, 0, 8 + h)),
                  pl.BlockSpec((1, LANES, lc), lambda i, h, j: (i, h, 0)),
                  pl.BlockSpec((1, LANES), lambda i, h, j: (0, h))],
        out_specs=pl.BlockSpec((1, tq, LANES), lambda i, h, j: (i, j, h)),
        out_shape=jax.ShapeDtypeStruct((b, s, DIFF_HEADS * LANES), BF16),
        scratch_shapes=[pltpu.VMEM((2, lc + s, tq), F32)],
        compiler_params=_cparams(("arbitrary", "arbitrary", "arbitrary")),
        name="diff_attn",
    )(*lams, p, p, vt, pc, vtc, gain)


def _swa_attn_kernel(sink_ref, q_ref, k_ref, vt_ref, kc_ref, vtc_ref, gain_ref, o_ref,
                     kcat_ref, vtcat_ref):
    n = pl.program_id(1)
    nb = pl.num_programs(1)
    s_len = k_ref.shape[1]
    lc = kc_ref.shape[1]
    n_band = 3 * BAND

    @pl.when(n == 0)
    def _():
        kcat_ref[n_band:, :] = kc_ref[0]
        vtcat_ref[:, n_band:] = vtc_ref[0]

    prev = pl.multiple_of(jnp.maximum(n - 1, 0) * BAND, BAND)
    cur = pl.multiple_of(n * BAND, BAND)
    nxt = pl.multiple_of(jnp.minimum(n + 1, nb - 1) * BAND, BAND)
    for t, off in enumerate((prev, cur, nxt)):
        kcat_ref[t * BAND:(t + 1) * BAND, :] = k_ref[0, pl.ds(off, BAND), :]
        vtcat_ref[:, t * BAND:(t + 1) * BAND] = vt_ref[0, :, pl.ds(off, BAND)]

    keys = n_band + lc
    cols = SWA_GROUP * BAND
    ki = lax.broadcasted_iota(I32, (keys, cols), 0)
    qi = lax.broadcasted_iota(I32, (keys, cols), 1) & (BAND - 1)
    kpos = (n - 1) * BAND + ki
    in_band = jnp.where(jnp.abs(qi + BAND - ki) <= BAND,
                        jnp.where(kpos >= 0, jnp.where(kpos < s_len, 1, 0), 0), 0)
    ok = jnp.where(ki >= n_band, 1, in_band) > 0

    q = q_ref[0]
    outs = []
    for g in range(SWA_KV):
        kg = kcat_ref[:, g * DH:(g + 1) * DH]
        vtg = vtcat_ref[g * DH:(g + 1) * DH, :]
        qg = jnp.concatenate(
            [q[:, (g * SWA_GROUP + j) * DH:(g * SWA_GROUP + j + 1) * DH] for j in range(SWA_GROUP)],
            axis=0)
        s = lax.dot_general(kg, qg, (((1,), (1,)), ((), ())), preferred_element_type=F32)
        s = jnp.where(ok, s, NEG_INF)
        sink = jnp.concatenate(
            [jnp.full((1, BAND), sink_ref[g * SWA_GROUP + j], F32) for j in range(SWA_GROUP)],
            axis=1)
        m = jnp.maximum(jnp.max(s, axis=0, keepdims=True), sink)
        e = jnp.exp(s - m)
        denom = jnp.sum(e, axis=0, keepdims=True) + jnp.exp(sink - m)
        o = jnp.dot(vtg, e.astype(BF16), preferred_element_type=F32) / denom
        outs.append(o * lax.rsqrt(jnp.mean(o * o, axis=0, keepdims=True) + EPS))
    o_all = jnp.concatenate(outs, axis=0).T
    for g in range(SWA_KV):
        for j in range(SWA_GROUP):
            c0 = (g * SWA_GROUP + j) * DH
            o_ref[0, :, c0:c0 + DH] = (o_all[j * BAND:(j + 1) * BAND, g * DH:(g + 1) * DH]
                                       * gain_ref[:, c0:c0 + DH]).astype(BF16)


def _swa_attn(p, vt, pc, vtc, sink, gain):
    b, s, _ = p.shape
    lc = pc.shape[1]
    kvw = SWA_KV * DH
    vblk = vt.shape[1] // kvw - 1
    return pl.pallas_call(
        _swa_attn_kernel,
        grid=(b, s // BAND),
        in_specs=[pl.BlockSpec(memory_space=pltpu.SMEM),
                  pl.BlockSpec((1, BAND, SWA_HEADS * DH), lambda i, j: (i, j, 2)),
                  pl.BlockSpec((1, s, kvw), lambda i, j: (i, 0, 12)),
                  pl.BlockSpec((1, kvw, s), lambda i, j: (i, vblk, 0)),
                  pl.BlockSpec((1, lc, kvw), lambda i, j: (i, 0, 12)),
                  pl.BlockSpec((1, kvw, lc), lambda i, j: (i, vblk, 0)),
                  pl.BlockSpec((1, SWA_HEADS * DH), lambda i, j: (0, 0))],
        out_specs=pl.BlockSpec((1, BAND, SWA_HEADS * DH), lambda i, j: (i, j, 0)),
        out_shape=jax.ShapeDtypeStruct((b, s, SWA_HEADS * DH), BF16),
        scratch_shapes=[pltpu.VMEM((3 * BAND + lc, kvw), BF16),
                        pltpu.VMEM((kvw, 3 * BAND + lc), BF16)],
        compiler_params=_cparams(("arbitrary", "arbitrary")),
        name="swa_attn",
    )(sink, p, p, vt, pc, vtc, gain)


def _outproj_kernel(ad_ref, as_ref, wd_ref, ws_ref, x_ref, g1_ref, sh_ref, sc_ref, gn_ref,
                    x1_ref, h2_ref):
    a = (jnp.dot(ad_ref[0], wd_ref[...], preferred_element_type=F32)
         + jnp.dot(as_ref[0], ws_ref[...], preferred_element_type=F32))
    x1 = x_ref[0] + g1_ref[0] * a
    x1_ref[0] = x1
    h2_ref[0] = _modnorm(x1, gn_ref[...], sh_ref[0], sc_ref[0]).astype(BF16)


def _outproj(a_diff, a_swa, w_d, w_s, x, g1, sh2, sc2, gain, tm):
    b, s, d = x.shape
    wd = a_diff.shape[2]
    vec = pl.BlockSpec((1, 1, d), lambda i, j: (i, 0, 0))
    return pl.pallas_call(
        _outproj_kernel,
        grid=(b, s // tm),
        in_specs=[pl.BlockSpec((1, tm, wd), lambda i, j: (i, j, 0)),
                  pl.BlockSpec((1, tm, wd), lambda i, j: (i, j, 0)),
                  pl.BlockSpec((wd, d), lambda i, j: (0, 0)),
                  pl.BlockSpec((wd, d), lambda i, j: (0, 0)),
                  pl.BlockSpec((1, tm, d), lambda i, j: (i, j, 0)),
                  vec, vec, vec,
                  pl.BlockSpec((1, d), lambda i, j: (0, 0))],
        out_specs=[pl.BlockSpec((1, tm, d), lambda i, j: (i, j, 0)),
                   pl.BlockSpec((1, tm, d), lambda i, j: (i, j, 0))],
        out_shape=[jax.ShapeDtypeStruct((b, s, d), F32), jax.ShapeDtypeStruct((b, s, d), BF16)],
        compiler_params=_cparams(("arbitrary", "arbitrary")),
        name="out_proj",
    )(a_diff, a_swa, w_d, w_s, x, g1, sh2, sc2, gain)


def _extract_topk(s, rank, payload, k):
    vals, pays = [], []
    for _ in range(k):
        m = jnp.max(s, axis=0, keepdims=True)
        first = jnp.min(jnp.where(s == m, rank, jnp.inf), axis=0, keepdims=True)
        hit = rank == first
        vals.append(m)
        pays.append(first if payload is None
                    else jnp.sum(jnp.where(hit, payload, 0.0), axis=0, keepdims=True))
        s = jnp.where(hit, -jnp.inf, s)
    return jnp.concatenate(vals, axis=0), jnp.concatenate(pays, axis=0)


def _extract_topk_paired(s, k):
    half = s.shape[0] // 2
    a, b = s[:half], s[half:]
    ra = lax.broadcasted_iota(I32, a.shape, 0).astype(F32)
    rb = ra + float(half)
    b_wins = b > a
    w, l = jnp.where(b_wins, b, a), jnp.where(b_wins, a, b)
    rw, rl = jnp.where(b_wins, rb, ra), jnp.where(b_wins, ra, rb)
    vals, ids = [], []
    for _ in range(k):
        m = jnp.max(w, axis=0, keepdims=True)
        first = jnp.min(jnp.where(w == m, rw, jnp.inf), axis=0, keepdims=True)
        hit = rw == first
        vals.append(m)
        ids.append(first)
        w = jnp.where(hit, l, w)
        rw = jnp.where(hit, rl, rw)
        l = jnp.where(hit, -jnp.inf, l)
    return jnp.concatenate(vals, axis=0), jnp.concatenate(ids, axis=0)


def _pair_candidates(s1, i1, s2, i2):
    k, tt = s1.shape
    assert k == 16, "block layout below is written for 16 x 16 pairs"
    sub = lax.broadcasted_iota(I32, (8, tt), 0)
    vals = [s1[0:1] + s2, s1[1:2] + s2[0:8]]
    pos = [lax.broadcasted_iota(I32, (k, tt), 0), k + sub]
    ids = [i1[0:1] * N_KEYS + i2, i1[1:2] * N_KEYS + i2[0:8]]
    for a in range(2, 8):
        keep = sub < k // (a + 1)
        vals.append(jnp.where(keep, s1[a:a + 1] + s2[0:8], -jnp.inf))
        pos.append(a * k + sub)
        ids.append(i1[a:a + 1] * N_KEYS + i2[0:8])
    vals.append(s1[8:16] + s2[0:1])
    pos.append((sub + 8) * k)
    ids.append(i1[8:16] * N_KEYS + i2[0:1])
    return (jnp.concatenate(vals, axis=0), jnp.concatenate(pos, axis=0).astype(F32),
            jnp.concatenate(ids, axis=0))


def _peer_route_kernel(h_ref, wq_ref, keys_ref, idx_ref, gate_ref):
    q = jnp.dot(h_ref[...], wq_ref[...], preferred_element_type=F32).astype(BF16)
    for h in range(PEER_HEADS):
        halves = []
        for i in range(2):
            c0 = (h * 2 + i) * N_KEYS
            s = lax.dot_general(keys_ref[h, i], q[:, c0:c0 + N_KEYS], (((1,), (1,)), ((), ())),
                                preferred_element_type=F32)
            halves.append(_extract_topk_paired(s, PEER_TOPK))
        (s1, i1), (s2, i2) = halves
        cand, pos, cidx = _pair_candidates(s1, i1, s2, i2)
        top_s, top_i = _extract_topk(cand, pos, cidx, PEER_TOPK)
        e = jnp.exp(top_s - jnp.max(top_s, axis=0, keepdims=True))
        r0 = h * PEER_TOPK
        idx_ref[r0:r0 + PEER_TOPK, :] = top_i.astype(I32)
        gate_ref[r0:r0 + PEER_TOPK, :] = e / jnp.sum(e, axis=0, keepdims=True)


def _peer_route(h2, wq, keys, tt):
    t, d = h2.shape
    nq = wq.shape[1]
    return pl.pallas_call(
        _peer_route_kernel,
        grid=(t // tt,),
        in_specs=[pl.BlockSpec((tt, d), lambda i: (i, 0)),
                  pl.BlockSpec((d, nq), lambda i: (0, 0)),
                  pl.BlockSpec(keys.shape, lambda i: (0, 0, 0, 0))],
        out_specs=[pl.BlockSpec((PEER_SEL, tt), lambda i: (0, i)),
                   pl.BlockSpec((PEER_SEL, tt), lambda i: (0, i))],
        out_shape=[jax.ShapeDtypeStruct((PEER_SEL, t), I32), jax.ShapeDtypeStruct((PEER_SEL, t), F32)],
        compiler_params=_cparams(("arbitrary",)),
        name="peer_route",
    )(h2, wq, keys)


def _split_bf16(x):
    hi = x.astype(BF16)
    return hi, (x - hi.astype(F32)).astype(BF16)


def _peer_expert_kernel(idx_cur_ref, idx_nxt_ref, uv_ref, h_ref, gate_ref, seg_ref, grpt_ref,
                        o_ref, buf_ref, sem_ref, *, tb, tg):
    i = pl.program_id(0)
    nsteps = pl.num_programs(0)

    def row_copy(idx_ref, slot, t, j):
        return pltpu.make_async_copy(uv_ref.at[idx_ref[t * PEER_SEL + j]],
                                     buf_ref.at[slot, t, j], sem_ref.at[slot, t])

    def wait_tokens(slot, toks):
        for t in toks:
            for j in range(PEER_SEL):
                row_copy(idx_cur_ref, slot, t, j).wait()

    @pl.when(i == 0)
    def _():
        for t in range(tb):
            for j in range(PEER_SEL):
                row_copy(idx_cur_ref, 0, t, j).start(priority=j % 2)

    diag = (lax.broadcasted_iota(I32, (SLAB, PEER_SEL * SLAB), 1) & (SLAB - 1)) == \
        lax.broadcasted_iota(I32, (SLAB, PEER_SEL * SLAB), 0)

    def consume(slot):
        for t in range(PRE_ISSUE):
            for j in range(PEER_SEL):
                row_copy(idx_nxt_ref, 1 - slot, t, j).start(priority=j % 2)
        wait_tokens(slot, range(tb))
        g_hi, g_lo = _split_bf16(gate_ref[...])
        gate_rep = (jnp.dot(g_hi, grpt_ref[...], preferred_element_type=F32)
                    + jnp.dot(g_lo, grpt_ref[...], preferred_element_type=F32))

        def token_pair(t_u, t_v, w_row):
            z_parts, r = [], None
            if t_v is not None:
                wt = jnp.where(diag, jnp.broadcast_to(w_row, diag.shape), 0.0)
                wt_hi, wt_lo = _split_bf16(wt)
                wt2 = jnp.concatenate([wt_hi, wt_lo], axis=0)
            for n in range(PEER_SEL // SLAB):
                e0, c0 = n * SLAB, n * SLAB * SLAB
                if t_u is not None:
                    for j in range(e0, e0 + SLAB // 2) if t_u >= PRE_ISSUE else ():
                        row_copy(idx_nxt_ref, 1 - slot, t_u, j).start(priority=j % 2)
                    u_tile = buf_ref[slot, t_u, e0:e0 + SLAB, 0:SLAB, :].reshape(SLAB * SLAB, LANES)
                    y = lax.dot_general(h_ref[t_u], u_tile, (((1,), (1,)), ((), ())),
                                        preferred_element_type=F32)
                    z_parts.append(jnp.sum(jnp.where(diag[:, :SLAB * SLAB], y, 0.0),
                                           axis=0, keepdims=True))
                    for j in range(e0 + SLAB // 2, e0 + SLAB) if t_u >= PRE_ISSUE else ():
                        row_copy(idx_nxt_ref, 1 - slot, t_u, j).start(priority=j % 2)
                if t_v is not None:
                    v_tile = buf_ref[slot, t_v, e0:e0 + SLAB, SLAB:2 * SLAB, :].reshape(
                        SLAB * SLAB, LANES)
                    part = jnp.dot(wt2[:, c0:c0 + SLAB * SLAB], v_tile, preferred_element_type=F32)
                    r = part if r is None else r + part
            z = jnp.concatenate(z_parts, axis=1) if z_parts else None
            out = None if r is None else r[:SLAB] + r[SLAB:]
            return z, out

        def gate_rows(g, z):
            tile = SLAB * SLAB
            z8 = jnp.concatenate([z, jnp.zeros((8 - tg, z.shape[1]), F32)], axis=0) if tg < 8 else z
            stacked = jnp.concatenate([z8[:, n * tile:(n + 1) * tile]
                                       for n in range(z.shape[1] // tile)], axis=0)
            s_hi, s_lo = _split_bf16(stacked)
            seg = jnp.dot(jnp.concatenate([s_hi, s_lo], axis=0), seg_ref[...],
                          preferred_element_type=F32)
            seg = seg[:stacked.shape[0]] + seg[stacked.shape[0]:]
            act = jnp.concatenate([seg[8 * n:8 * n + tg] for n in range(z.shape[1] // tile)], axis=1)
            gelu = 0.5 * act * (1.0 + jnp.tanh(math.sqrt(2.0 / math.pi)
                                               * (act + 0.044715 * act * act * act)))
            return gate_rep[g * tg:(g + 1) * tg, :] * gelu

        assert GATE_LAG >= tg + 2
        w_reps, outs, zs = {}, {}, []
        for n in range(tb + GATE_LAG):
            t_u = n if n < tb else None
            t_v = n - GATE_LAG if n >= GATE_LAG else None
            w_row = None if t_v is None else w_reps[t_v // tg][t_v % tg:t_v % tg + 1]
            z, out = token_pair(t_u, t_v, w_row)
            if t_u is not None:
                zs.append(z)
            if t_v is not None:
                outs[t_v] = out
            if n % tg == 0 and tg <= n <= tb:
                g = n // tg - 1
                w_reps[g] = gate_rows(g, jnp.concatenate(zs[g * tg:(g + 1) * tg], axis=0))
        for t in range(tb):
            for r in range(SLAB):
                o_ref[t:t + 1, r * LANES:(r + 1) * LANES] = outs[t][r:r + 1, :]

        @pl.when(i == nsteps - 1)
        def _():
            wait_tokens(1 - slot, range(tb))

    @pl.when((i & 1) == 0)
    def _():
        consume(0)

    @pl.when((i & 1) == 1)
    def _():
        consume(1)


def _peer_expert(idx_flat, uv, h_slab, gate, tb):
    t = h_slab.shape[0]
    nsteps = t // tb
    tile = SLAB * SLAB
    seg = (lax.broadcasted_iota(I32, (tile, tile), 0) // SLAB
           == lax.broadcasted_iota(I32, (tile, tile), 1) // SLAB).astype(BF16)
    grpt = (lax.broadcasted_iota(I32, (PEER_SEL, PEER_SEL * SLAB), 0)
            == lax.broadcasted_iota(I32, (PEER_SEL, PEER_SEL * SLAB), 1) // SLAB).astype(BF16)
    blk = tb * PEER_SEL
    return pl.pallas_call(
        functools.partial(_peer_expert_kernel, tb=tb, tg=min(4, tb)),
        grid=(nsteps,),
        in_specs=[pl.BlockSpec((blk,), lambda i: (i,), memory_space=pltpu.SMEM),
                  pl.BlockSpec((blk,), lambda i: (jnp.minimum(i + 1, nsteps - 1),),
                               memory_space=pltpu.SMEM),
                  pl.BlockSpec(memory_space=pl.ANY),
                  pl.BlockSpec((tb, SLAB, LANES), lambda i: (i, 0, 0)),
                  pl.BlockSpec((tb, PEER_SEL), lambda i: (i, 0)),
                  pl.BlockSpec(seg.shape, lambda i: (0, 0)),
                  pl.BlockSpec(grpt.shape, lambda i: (0, 0))],
        out_specs=pl.BlockSpec((tb, SLAB * LANES), lambda i: (i, 0)),
        out_shape=jax.ShapeDtypeStruct((t, SLAB * LANES), F32),
        scratch_shapes=[pltpu.VMEM((2, tb, PEER_SEL, 2 * SLAB, LANES), BF16),
                        pltpu.SemaphoreType.DMA((2, tb))],
        compiler_params=_cparams(("arbitrary",)),
        name="peer_expert",
    )(idx_flat, idx_flat, uv, h_slab, gate, seg, grpt)


def _final_kernel(x1_ref, pe_ref, g2_ref, gn_ref, o_ref):
    x2 = x1_ref[0] + g2_ref[0] * pe_ref[0]
    o_ref[0] = x2 * lax.rsqrt(jnp.mean(x2 * x2, axis=-1, keepdims=True) + EPS) * gn_ref[...]


def _final(x1, pe, g2, gain, tm):
    b, s, d = x1.shape
    tok = pl.BlockSpec((1, tm, d), lambda i, j: (i, j, 0))
    return pl.pallas_call(
        _final_kernel,
        grid=(b, s // tm),
        in_specs=[tok, tok, pl.BlockSpec((1, 1, d), lambda i, j: (i, 0, 0)),
                  pl.BlockSpec((1, d), lambda i, j: (0, 0))],
        out_specs=tok,
        out_shape=jax.ShapeDtypeStruct((b, s, d), F32),
        compiler_params=_cparams(("arbitrary", "arbitrary")),
        name="final_norm",
    )(x1, pe, g2, gain)


def kernel(x, c, ctx, c_ctx, w_ada, b_ada, norm_attn, w_in, diff_lambda_q1, diff_lambda_k1,
           diff_lambda_q2, diff_lambda_k2, diff_norm, swa_sink, swa_norm, w_out, norm_ffn,
           peer_w_q, peer_sub_keys, peer_u, peer_v, final_norm):
    b, s, d = x.shape
    assert w_ada.shape[0] == 1, "single layer only"
    t = b * s

    cc = jnp.zeros((8, d), F32).at[:b].set(c).at[b].set(c_ctx)
    mod = _adaln(cc, w_ada[0], b_ada[0])
    sh1, sc1, g1, sh2, sc2, g2 = [m[:, None, :] for m in jnp.split(mod, 6, axis=-1)]

    w_in_b = w_in[0].astype(BF16)
    gain_attn = norm_attn[0].reshape(1, d)
    cos_t, sin_t = _rope_tables(s)
    tm = min(512, s)
    dvw = DIFF_HEADS * LANES
    kvw = SWA_KV * DH
    w_qk = jnp.concatenate([w_in_b[:, :2 * dvw], w_in_b[:, 3 * dvw:-kvw]], axis=1)
    w_vt = jnp.concatenate([w_in_b[:, 2 * dvw:3 * dvw], w_in_b[:, -kvw:]], axis=1).T
    p, vt = _inproj(x, sh1[:b], sc1[:b], gain_attn, w_qk, w_vt, cos_t, sin_t, True, tm)
    lc = ctx.shape[1]
    ones = jnp.ones((lc, LANES), F32)
    ctx_sh = jnp.broadcast_to(sh1[b:b + 1], (b, 1, d))
    ctx_sc = jnp.broadcast_to(sc1[b:b + 1], (b, 1, d))
    pc, vtc = _inproj(ctx, ctx_sh, ctx_sc, gain_attn, w_qk, w_vt, ones, ones, False, lc)

    lams = [v[0].reshape(1, DH).astype(F32) for v in
            (diff_lambda_q1, diff_lambda_k1, diff_lambda_q2, diff_lambda_k2)]
    a_diff = _diff_attn(p, vt, pc, vtc, lams, diff_norm[0].reshape(1, -1), min(512, s), min(512, s))
    a_swa = _swa_attn(p, vt, pc, vtc, swa_sink[0].astype(F32), swa_norm[0].reshape(1, -1))

    w_out_b = w_out[0].astype(BF16)
    dw = a_diff.shape[2]
    x1, h2 = _outproj(a_diff, a_swa, w_out_b[:dw], w_out_b[dw:], x, g1[:b], sh2[:b], sc2[:b],
                      norm_ffn[0].reshape(1, d), tm)

    idx_t, gate_t = _peer_route(h2.reshape(t, d), peer_w_q[0].astype(BF16),
                                peer_sub_keys[0].astype(BF16), min(256, t))
    uv = jnp.concatenate([peer_u[0].reshape(-1, SLAB, LANES),
                          peer_v[0].reshape(-1, SLAB, LANES)], axis=1).astype(BF16)
    pe = _peer_expert(idx_t.T.reshape(-1), uv, h2.reshape(t, SLAB, LANES), gate_t.T, 16)

    return _final(x1, pe.reshape(b, s, d), g2[:b], final_norm.reshape(1, d), tm)
```

```python
import functools
import math

import jax
import jax.numpy as jnp
from jax import lax
from jax.experimental import pallas as pl
from jax.experimental.pallas import tpu as pltpu

F32 = jnp.float32
BF16 = jnp.bfloat16
I32 = jnp.int32

EPS = 1e-6
NEG_INF = -1e30
ROPE_THETA = 10000.0
GRID_W = 64
ROPE_PAIRS = 16

LANES = 128
DH = 64
DIFF_HEADS = 8
SWA_HEADS = 16
SWA_KV = 4
SWA_GROUP = SWA_HEADS // SWA_KV
BAND = 128
QK_CHUNKS = 26
LAM_INIT = 0.8 - 0.6 * math.exp(-0.3 * 0)
LOG2E = math.log2(math.e)

PEER_HEADS = 8
PEER_TOPK = 16
N_KEYS = 128
PEER_SEL = PEER_HEADS * PEER_TOPK
SLAB = 16
PRE_ISSUE = 2
GATE_LAG = 6

VMEM_LIMIT = 56 * 1024 * 1024


def _cparams(sem):
    return pltpu.CompilerParams(dimension_semantics=sem, vmem_limit_bytes=VMEM_LIMIT)


def _adaln_kernel(c_ref, w_ref, b_ref, o_ref):
    c = c_ref[...]
    s = c * (1.0 / (1.0 + jnp.exp(-c)))
    rows = s.shape[0]
    s_hi, s_lo = _split_bf16(s)
    w_hi, w_lo = _split_bf16(w_ref[...])
    both = jnp.dot(jnp.concatenate([s_hi, s_lo], axis=0), w_hi, preferred_element_type=F32)
    o_ref[...] = (both[:rows] + both[rows:] + jnp.dot(s_hi, w_lo, preferred_element_type=F32)
                  + b_ref[...])


def _adaln(cc, w, b):
    rows, d = cc.shape
    n = w.shape[1]
    tn = 1024
    return pl.pallas_call(
        _adaln_kernel,
        grid=(n // tn,),
        in_specs=[pl.BlockSpec((rows, d), lambda j: (0, 0)),
                  pl.BlockSpec((d, tn), lambda j: (0, j)),
                  pl.BlockSpec((1, tn), lambda j: (0, j))],
        out_specs=pl.BlockSpec((rows, tn), lambda j: (0, j)),
        out_shape=jax.ShapeDtypeStruct((rows, n), F32),
        compiler_params=_cparams(("arbitrary",)),
        name="adaln",
    )(cc, w, b.reshape(1, n))


def _modnorm(x, gain, shift, scale):
    y = x * lax.rsqrt(jnp.mean(x * x, axis=-1, keepdims=True) + EPS)
    return (y * gain) * (1.0 + scale) + shift


def _swap16(p):
    lane = lax.broadcasted_iota(I32, p.shape, 1)
    up = pltpu.roll(p, LANES - 16, 1)
    dn = pltpu.roll(p, 16, 1)
    return jnp.where((lane & 31) < 16, up, dn)


def _inproj_kernel(x_ref, sh_ref, sc_ref, g_ref, w_ref, wvt_ref, cos_ref, sin_ref, o_ref, vt_ref, *,
                   rope_chunks, chunk_scale):
    h = _modnorm(x_ref[0], g_ref[...], sh_ref[0], sc_ref[0]).astype(BF16)
    p = jnp.dot(h, w_ref[...], preferred_element_type=F32)
    n_chunks = p.shape[1] // LANES
    if rope_chunks:
        cs = cos_ref[...]
        sn = sin_ref[...]
    for j in range(n_chunks):
        pj = p[:, j * LANES:(j + 1) * LANES]
        if j in rope_chunks:
            pj = pj * cs + _swap16(pj) * sn
        if j in chunk_scale:
            pj = pj * chunk_scale[j]
        o_ref[0, :, j * LANES:(j + 1) * LANES] = pj.astype(BF16)
    vt_ref[0] = lax.dot_general(wvt_ref[...], h, (((1,), (1,)), ((), ())),
                                preferred_element_type=F32).astype(BF16)


def _inproj(x, shift, scale, gain, w, w_vt, cos_t, sin_t, rope, tm):
    b, s, d = x.shape
    n = w.shape[1]
    nv = w_vt.shape[0]
    assert n == QK_CHUNKS * LANES
    rope_chunks = frozenset(range(QK_CHUNKS)) if rope else frozenset()
    chunk_scale = {j: DH ** -0.5 * LOG2E for j in range(0, 8)}
    chunk_scale.update({j: DH ** -0.5 for j in range(16, 24)})
    kern = functools.partial(_inproj_kernel, rope_chunks=rope_chunks, chunk_scale=chunk_scale)
    return pl.pallas_call(
        kern,
        grid=(b, s // tm),
        in_specs=[pl.BlockSpec((1, tm, d), lambda i, j: (i, j, 0)),
                  pl.BlockSpec((1, 1, d), lambda i, j: (i, 0, 0)),
                  pl.BlockSpec((1, 1, d), lambda i, j: (i, 0, 0)),
                  pl.BlockSpec((1, d), lambda i, j: (0, 0)),
                  pl.BlockSpec((d, n), lambda i, j: (0, 0), pipeline_mode=pl.Buffered(1)),
                  pl.BlockSpec((nv, d), lambda i, j: (0, 0), pipeline_mode=pl.Buffered(1)),
                  pl.BlockSpec((tm, LANES), lambda i, j: (j, 0)),
                  pl.BlockSpec((tm, LANES), lambda i, j: (j, 0))],
        out_specs=[pl.BlockSpec((1, tm, n), lambda i, j: (i, j, 0)),
                   pl.BlockSpec((1, nv, tm), lambda i, j: (i, 0, j))],
        out_shape=[jax.ShapeDtypeStruct((b, s, n), BF16), jax.ShapeDtypeStruct((b, nv, s), BF16)],
        compiler_params=_cparams(("arbitrary", "arbitrary")),
        name="in_proj",
    )(x, shift, scale, gain, w, w_vt, cos_t, sin_t)


def _rope_tables(s):
    rows = s // GRID_W
    row = jnp.repeat(jnp.arange(rows, dtype=F32), GRID_W)
    col = jnp.tile(jnp.arange(GRID_W, dtype=F32), rows)
    freqs = ROPE_THETA ** (-jnp.arange(ROPE_PAIRS, dtype=F32) / ROPE_PAIRS)
    ar = row[:, None] * freqs
    ac = col[:, None] * freqs
    cos64 = jnp.concatenate([jnp.cos(ar), jnp.cos(ar), jnp.cos(ac), jnp.cos(ac)], axis=1)
    sin64 = jnp.concatenate([-jnp.sin(ar), jnp.sin(ar), -jnp.sin(ac), jnp.sin(ac)], axis=1)
    return jnp.tile(cos64, (1, 2)), jnp.tile(sin64, (1, 2))


def _head_rms(o, gain):
    return o * lax.rsqrt(jnp.mean(o * o, axis=-1, keepdims=True) + EPS) * gain


def _diff_attn_kernel(lq1_ref, lk1_ref, lq2_ref, lk2_ref, q_ref, k_ref, vt_ref, kc_ref, vtc_ref,
                      gain_ref, o_ref, s_ref, m_ref, *, tk, n_tiles):
    j = pl.program_id(2)
    tq = q_ref.shape[1]
    s_len = k_ref.shape[1]
    lc = kc_ref.shape[1]
    blocks = [(0, lc, lambda: kc_ref[0], lambda: vtc_ref[0])]
    blocks += [(lc + b * tk, tk, lambda b=b: k_ref[0, b * tk:(b + 1) * tk, :],
                lambda b=b: vt_ref[0, :, b * tk:(b + 1) * tk]) for b in range(s_len // tk)]
    m_init = jnp.full((1, tq), NEG_INF, F32)

    def query_maps():
        q = q_ref[0]
        lane = lax.broadcasted_iota(I32, q.shape, 1)
        zero = jnp.zeros_like(q)
        return jnp.where(lane < DH, q, zero), jnp.where(lane >= DH, q, zero)

    def score_block(slot, i, blk, qm, m):
        r0, rows, kblk, _ = blk
        s = lax.dot_general(kblk(), qm, (((1,), (1,)), ((), ())), preferred_element_type=F32)
        s_ref[slot, i, r0:r0 + rows, :] = s
        return jnp.maximum(m, jnp.max(s, axis=0, keepdims=True))

    def prob_block(slot, i, blk, m, l, acc):
        r0, rows, _, vtblk = blk
        p = jnp.exp2(s_ref[slot, i, r0:r0 + rows, :] - m)
        part = jnp.dot(vtblk(), p.astype(BF16), preferred_element_type=F32)
        psum = jnp.sum(p, axis=0, keepdims=True)
        return (psum if l is None else l + psum), (part if acc is None else acc + part)

    def finish(l, acc):
        lam = (jnp.exp(jnp.sum(lq1_ref[...] * lk1_ref[...], keepdims=True))
               - jnp.exp(jnp.sum(lq2_ref[...] * lk2_ref[...], keepdims=True)) + LAM_INIT)
        out = (acc[0] / l[0] - lam * (acc[1] / l[1])).T
        o_ref[0] = (_head_rms(out, gain_ref[...]) * (1.0 - LAM_INIT)).astype(BF16)

    def step(score_slot, prob_slot):
        if score_slot is not None:
            qm = query_maps()
            m_new = [m_init, m_init]
        if prob_slot is not None:
            m_old = [m_ref[prob_slot, 0], m_ref[prob_slot, 1]]
            l, acc = [None, None], [None, None]
        for blk in blocks:
            for i in range(2):
                if score_slot is not None:
                    m_new[i] = score_block(score_slot, i, blk, qm[i], m_new[i])
            for i in range(2):
                if prob_slot is not None:
                    l[i], acc[i] = prob_block(prob_slot, i, blk, m_old[i], l[i], acc[i])
        if score_slot is not None:
            m_ref[score_slot, 0] = m_new[0]
            m_ref[score_slot, 1] = m_new[1]
        if prob_slot is not None:
            finish(l, acc)

    @pl.when(j == 0)
    def _():
        step(0, None)

    for parity in range(2):
        @pl.when((j > 0) & (j < n_tiles) & ((j & 1) == parity))
        def _():
            step(parity, 1 - parity)

    @pl.when(j == n_tiles)
    def _():
        step(None, (n_tiles - 1) % 2)


def _diff_attn(p, vt, pc, vtc, lams, gain, tq, tk):
    b, s, _ = p.shape
    lc = pc.shape[1]
    n_tiles = s // tq
    lam_spec = pl.BlockSpec((1, DH), lambda i, h, j: (0, 0))
    return pl.pallas_call(
        functools.partial(_diff_attn_kernel, tk=tk, n_tiles=n_tiles),
        grid=(b, DIFF_HEADS, n_tiles + 1),
        in_specs=[lam_spec, lam_spec, lam_spec, lam_spec,
                  pl.BlockSpec((1, tq, LANES), lambda i, h, j: (i, jnp.minimum(j, n_tiles - 1), h)),
                  pl.BlockSpec((1, s, LANES), lambda i, h, j: (i, 0, 8 + h)),
                  pl.BlockSpec((1, LANES, s), lambda i, h, j: (i, h, 0)),
                  pl.BlockSpec((1, lc, LANES), lambda i, h, j: (i, 0, 8 + h)),
                  pl.BlockSpec((1, LANES, lc), lambda i, h, j: (i, h, 0)),
                  pl.BlockSpec((1, LANES), lambda i, h, j: (0, h))],
        out_specs=pl.BlockSpec((1, tq, LANES), lambda i, h, j: (i, jnp.maximum(j - 1, 0), h)),
        out_shape=jax.ShapeDtypeStruct((b, s, DIFF_HEADS * LANES), BF16),
        scratch_shapes=[pltpu.VMEM((2, 2, lc + s, tq), F32), pltpu.VMEM((2, 2, 1, tq), F32)],
        compiler_params=_cparams(("arbitrary", "arbitrary", "arbitrary")),
        name="diff_attn",
    )(*lams, p, p, vt, pc, vtc, gain)


def _swa_attn_kernel(sink_ref, q_ref, k_ref, vt_ref, kc_ref, vtc_ref, gain_ref, o_ref,
                     kcat_ref, vtcat_ref):
    n = pl.program_id(1)
    nb = pl.num_programs(1)
    s_len = k_ref.shape[1]
    lc = kc_ref.shape[1]
    n_band = 3 * BAND

    @pl.when(n == 0)
    def _():
        kcat_ref[n_band:, :] = kc_ref[0]
        vtcat_ref[:, n_band:] = vtc_ref[0]

    prev = pl.multiple_of(jnp.maximum(n - 1, 0) * BAND, BAND)
    cur = pl.multiple_of(n * BAND, BAND)
    nxt = pl.multiple_of(jnp.minimum(n + 1, nb - 1) * BAND, BAND)
    for t, off in enumerate((prev, cur, nxt)):
        kcat_ref[t * BAND:(t + 1) * BAND, :] = k_ref[0, pl.ds(off, BAND), :]
        vtcat_ref[:, t * BAND:(t + 1) * BAND] = vt_ref[0, :, pl.ds(off, BAND)]

    keys = n_band + lc
    cols = SWA_GROUP * BAND
    ki = lax.broadcasted_iota(I32, (keys, cols), 0)
    qi = lax.broadcasted_iota(I32, (keys, cols), 1) & (BAND - 1)
    kpos = (n - 1) * BAND + ki
    in_band = jnp.where(jnp.abs(qi + BAND - ki) <= BAND,
                        jnp.where(kpos >= 0, jnp.where(kpos < s_len, 1, 0), 0), 0)
    ok = jnp.where(ki >= n_band, 1, in_band) > 0

    q = q_ref[0]
    outs = []
    for g in range(SWA_KV):
        kg = kcat_ref[:, g * DH:(g + 1) * DH]
        vtg = vtcat_ref[g * DH:(g + 1) * DH, :]
        qg = jnp.concatenate(
            [q[:, (g * SWA_GROUP + j) * DH:(g * SWA_GROUP + j + 1) * DH] for j in range(SWA_GROUP)],
            axis=0)
        s = lax.dot_general(kg, qg, (((1,), (1,)), ((), ())), preferred_element_type=F32)
        s = jnp.where(ok, s, NEG_INF)
        sink = jnp.concatenate(
            [jnp.full((1, BAND), sink_ref[g * SWA_GROUP + j], F32) for j in range(SWA_GROUP)],
            axis=1)
        m = jnp.maximum(jnp.max(s, axis=0, keepdims=True), sink)
        e = jnp.exp(s - m)
        denom = jnp.sum(e, axis=0, keepdims=True) + jnp.exp(sink - m)
        o = jnp.dot(vtg, e.astype(BF16), preferred_element_type=F32) / denom
        outs.append(o * lax.rsqrt(jnp.mean(o * o, axis=0, keepdims=True) + EPS))
    o_all = jnp.concatenate(outs, axis=0).T
    for g in range(SWA_KV):
        for j in range(SWA_GROUP):
            c0 = (g * SWA_GROUP + j) * DH
            o_ref[0, :, c0:c0 + DH] = (o_all[j * BAND:(j + 1) * BAND, g * DH:(g + 1) * DH]
                                       * gain_ref[:, c0:c0 + DH]).astype(BF16)


def _swa_attn(p, vt, pc, vtc, sink, gain):
    b, s, _ = p.shape
    lc = pc.shape[1]
    kvw = SWA_KV * DH
    vblk = vt.shape[1] // kvw - 1
    return pl.pallas_call(
        _swa_attn_kernel,
        grid=(b, s // BAND),
        in_specs=[pl.BlockSpec(memory_space=pltpu.SMEM),
                  pl.BlockSpec((1, BAND, SWA_HEADS * DH), lambda i, j: (i, j, 2)),
                  pl.BlockSpec((1, s, kvw), lambda i, j: (i, 0, 12)),
                  pl.BlockSpec((1, kvw, s), lambda i, j: (i, vblk, 0)),
                  pl.BlockSpec((1, lc, kvw), lambda i, j: (i, 0, 12)),
                  pl.BlockSpec((1, kvw, lc), lambda i, j: (i, vblk, 0)),
                  pl.BlockSpec((1, SWA_HEADS * DH), lambda i, j: (0, 0))],
        out_specs=pl.BlockSpec((1, BAND, SWA_HEADS * DH), lambda i, j: (i, j, 0)),
        out_shape=jax.ShapeDtypeStruct((b, s, SWA_HEADS * DH), BF16),
        scratch_shapes=[pltpu.VMEM((3 * BAND + lc, kvw), BF16),
                        pltpu.VMEM((kvw, 3 * BAND + lc), BF16)],
        compiler_params=_cparams(("arbitrary", "arbitrary")),
        name="swa_attn",
    )(sink, p, p, vt, pc, vtc, gain)


def _outproj_kernel(ad_ref, as_ref, wd_ref, ws_ref, x_ref, g1_ref, sh_ref, sc_ref, gn_ref,
                    x1_ref, h2_ref):
    a = (jnp.dot(ad_ref[0], wd_ref[...], preferred_element_type=F32)
         + jnp.dot(as_ref[0], ws_ref[...], preferred_element_type=F32))
    x1 = x_ref[0] + g1_ref[0] * a
    x1_ref[0] = x1
    h2_ref[0] = _modnorm(x1, gn_ref[...], sh_ref[0], sc_ref[0]).astype(BF16)


def _outproj(a_diff, a_swa, w_d, w_s, x, g1, sh2, sc2, gain, tm):
    b, s, d = x.shape
    wd = a_diff.shape[2]
    vec = pl.BlockSpec((1, 1, d), lambda i, j: (i, 0, 0))
    return pl.pallas_call(
        _outproj_kernel,
        grid=(b, s // tm),
        in_specs=[pl.BlockSpec((1, tm, wd), lambda i, j: (i, j, 0)),
                  pl.BlockSpec((1, tm, wd), lambda i, j: (i, j, 0)),
                  pl.BlockSpec((wd, d), lambda i, j: (0, 0)),
                  pl.BlockSpec((wd, d), lambda i, j: (0, 0)),
                  pl.BlockSpec((1, tm, d), lambda i, j: (i, j, 0)),
                  vec, vec, vec,
                  pl.BlockSpec((1, d), lambda i, j: (0, 0))],
        out_specs=[pl.BlockSpec((1, tm, d), lambda i, j: (i, j, 0)),
                   pl.BlockSpec((1, tm, d), lambda i, j: (i, j, 0))],
        out_shape=[jax.ShapeDtypeStruct((b, s, d), F32), jax.ShapeDtypeStruct((b, s, d), BF16)],
        compiler_params=_cparams(("arbitrary", "arbitrary")),
        name="out_proj",
    )(a_diff, a_swa, w_d, w_s, x, g1, sh2, sc2, gain)


def _extract_topk(s, rank, payload, k):
    vals, pays = [], []
    for _ in range(k):
        m = jnp.max(s, axis=0, keepdims=True)
        first = jnp.min(jnp.where(s == m, rank, jnp.inf), axis=0, keepdims=True)
        hit = rank == first
        vals.append(m)
        pays.append(first if payload is None
                    else jnp.sum(jnp.where(hit, payload, 0.0), axis=0, keepdims=True))
        s = jnp.where(hit, -jnp.inf, s)
    return jnp.concatenate(vals, axis=0), jnp.concatenate(pays, axis=0)


def _extract_topk_paired(s, k):
    half = s.shape[0] // 2
    a, b = s[:half], s[half:]
    ra = lax.broadcasted_iota(I32, a.shape, 0).astype(F32)
    rb = ra + float(half)
    b_wins = b > a
    w, l = jnp.where(b_wins, b, a), jnp.where(b_wins, a, b)
    rw, rl = jnp.where(b_wins, rb, ra), jnp.where(b_wins, ra, rb)
    vals, ids = [], []
    for _ in range(k):
        m = jnp.max(w, axis=0, keepdims=True)
        first = jnp.min(jnp.where(w == m, rw, jnp.inf), axis=0, keepdims=True)
        hit = rw == first
        vals.append(m)
        ids.append(first)
        w = jnp.where(hit, l, w)
        rw = jnp.where(hit, rl, rw)
        l = jnp.where(hit, -jnp.inf, l)
    return jnp.concatenate(vals, axis=0), jnp.concatenate(ids, axis=0)


def _pair_candidates(s1, i1, s2, i2):
    k, tt = s1.shape
    assert k == 16, "block layout below is written for 16 x 16 pairs"
    sub = lax.broadcasted_iota(I32, (8, tt), 0)
    vals = [s1[0:1] + s2, s1[1:2] + s2[0:8]]
    pos = [lax.broadcasted_iota(I32, (k, tt), 0), k + sub]
    ids = [i1[0:1] * N_KEYS + i2, i1[1:2] * N_KEYS + i2[0:8]]
    for a in range(2, 8):
        keep = sub < k // (a + 1)
        vals.append(jnp.where(keep, s1[a:a + 1] + s2[0:8], -jnp.inf))
        pos.append(a * k + sub)
        ids.append(i1[a:a + 1] * N_KEYS + i2[0:8])
    vals.append(s1[8:16] + s2[0:1])
    pos.append((sub + 8) * k)
    ids.append(i1[8:16] * N_KEYS + i2[0:1])
    return (jnp.concatenate(vals, axis=0), jnp.concatenate(pos, axis=0).astype(F32),
            jnp.concatenate(ids, axis=0))


def _peer_route_kernel(h_ref, wq_ref, keys_ref, idx_ref, gate_ref):
    q = jnp.dot(h_ref[...], wq_ref[...], preferred_element_type=F32).astype(BF16)
    for h in range(PEER_HEADS):
        halves = []
        for i in range(2):
            c0 = (h * 2 + i) * N_KEYS
            s = lax.dot_general(keys_ref[h, i], q[:, c0:c0 + N_KEYS], (((1,), (1,)), ((), ())),
                                preferred_element_type=F32)
            halves.append(_extract_topk_paired(s, PEER_TOPK))
        (s1, i1), (s2, i2) = halves
        cand, pos, cidx = _pair_candidates(s1, i1, s2, i2)
        top_s, top_i = _extract_topk(cand, pos, cidx, PEER_TOPK)
        e = jnp.exp(top_s - jnp.max(top_s, axis=0, keepdims=True))
        r0 = h * PEER_TOPK
        idx_ref[r0:r0 + PEER_TOPK, :] = top_i.astype(I32)
        gate_ref[r0:r0 + PEER_TOPK, :] = e / jnp.sum(e, axis=0, keepdims=True)


def _peer_route(h2, wq, keys, tt):
    t, d = h2.shape
    nq = wq.shape[1]
    return pl.pallas_call(
        _peer_route_kernel,
        grid=(t // tt,),
        in_specs=[pl.BlockSpec((tt, d), lambda i: (i, 0)),
                  pl.BlockSpec((d, nq), lambda i: (0, 0)),
                  pl.BlockSpec(keys.shape, lambda i: (0, 0, 0, 0))],
        out_specs=[pl.BlockSpec((PEER_SEL, tt), lambda i: (0, i)),
                   pl.BlockSpec((PEER_SEL, tt), lambda i: (0, i))],
        out_shape=[jax.ShapeDtypeStruct((PEER_SEL, t), I32), jax.ShapeDtypeStruct((PEER_SEL, t), F32)],
        compiler_params=_cparams(("arbitrary",)),
        name="peer_route",
    )(h2, wq, keys)


def _split_bf16(x):
    hi = x.astype(BF16)
    return hi, (x - hi.astype(F32)).astype(BF16)


def _peer_expert_kernel(idx_cur_ref, idx_nxt_ref, uv_ref, h_ref, gate_ref, seg_ref, grpt_ref,
                        o_ref, buf_ref, sem_ref, *, tb, tg):
    i = pl.program_id(0)
    nsteps = pl.num_programs(0)

    def row_copy(idx_ref, slot, t, j):
        return pltpu.make_async_copy(uv_ref.at[idx_ref[t * PEER_SEL + j]],
                                     buf_ref.at[slot, t, j], sem_ref.at[slot, t])

    def wait_tokens(slot, toks):
        for t in toks:
            for j in range(PEER_SEL):
                row_copy(idx_cur_ref, slot, t, j).wait()

    @pl.when(i == 0)
    def _():
        for t in range(tb):
            for j in range(PEER_SEL):
                row_copy(idx_cur_ref, 0, t, j).start(priority=j % 2)

    diag = (lax.broadcasted_iota(I32, (SLAB, PEER_SEL * SLAB), 1) & (SLAB - 1)) == \
        lax.broadcasted_iota(I32, (SLAB, PEER_SEL * SLAB), 0)

    def consume(slot):
        for t in range(PRE_ISSUE):
            for j in range(PEER_SEL):
                row_copy(idx_nxt_ref, 1 - slot, t, j).start(priority=j % 2)
        wait_tokens(slot, range(tb))
        g_hi, g_lo = _split_bf16(gate_ref[...])
        gate_rep = (jnp.dot(g_hi, grpt_ref[...], preferred_element_type=F32)
                    + jnp.dot(g_lo, grpt_ref[...], preferred_element_type=F32))

        def token_pair(t_u, t_v, w_row):
            z_parts, r = [], None
            if t_v is not None:
                wt = jnp.where(diag, jnp.broadcast_to(w_row, diag.shape), 0.0)
                wt_hi, wt_lo = _split_bf16(wt)
                wt2 = jnp.concatenate([wt_hi, wt_lo], axis=0)
            for n in range(PEER_SEL // SLAB):
                e0, c0 = n * SLAB, n * SLAB * SLAB
                if t_u is not None:
                    for j in range(e0, e0 + SLAB // 2) if t_u >= PRE_ISSUE else ():
                        row_copy(idx_nxt_ref, 1 - slot, t_u, j).start(priority=j % 2)
                    u_tile = buf_ref[slot, t_u, e0:e0 + SLAB, 0:SLAB, :].reshape(SLAB * SLAB, LANES)
                    y = lax.dot_general(h_ref[t_u], u_tile, (((1,), (1,)), ((), ())),
                                        preferred_element_type=F32)
                    z_parts.append(jnp.sum(jnp.where(diag[:, :SLAB * SLAB], y, 0.0),
                                           axis=0, keepdims=True))
                    for j in range(e0 + SLAB // 2, e0 + SLAB) if t_u >= PRE_ISSUE else ():
                        row_copy(idx_nxt_ref, 1 - slot, t_u, j).start(priority=j % 2)
                if t_v is not None:
                    v_tile = buf_ref[slot, t_v, e0:e0 + SLAB, SLAB:2 * SLAB, :].reshape(
                        SLAB * SLAB, LANES)
                    part = jnp.dot(wt2[:, c0:c0 + SLAB * SLAB], v_tile, preferred_element_type=F32)
                    r = part if r is None else r + part
            z = jnp.concatenate(z_parts, axis=1) if z_parts else None
            out = None if r is None else r[:SLAB] + r[SLAB:]
            return z, out

        def gate_rows(g, z):
            tile = SLAB * SLAB
            z8 = jnp.concatenate([z, jnp.zeros((8 - tg, z.shape[1]), F32)], axis=0) if tg < 8 else z
            stacked = jnp.concatenate([z8[:, n * tile:(n + 1) * tile]
                                       for n in range(z.shape[1] // tile)], axis=0)
            s_hi, s_lo = _split_bf16(stacked)
            seg = jnp.dot(jnp.concatenate([s_hi, s_lo], axis=0), seg_ref[...],
                          preferred_element_type=F32)
            seg = seg[:stacked.shape[0]] + seg[stacked.shape[0]:]
            act = jnp.concatenate([seg[8 * n:8 * n + tg] for n in range(z.shape[1] // tile)], axis=1)
            gelu = 0.5 * act * (1.0 + jnp.tanh(math.sqrt(2.0 / math.pi)
                                               * (act + 0.044715 * act * act * act)))
            return gate_rep[g * tg:(g + 1) * tg, :] * gelu

        assert GATE_LAG >= tg + 2
        w_reps, outs, zs = {}, {}, []
        for n in range(tb + GATE_LAG):
            t_u = n if n < tb else None
            t_v = n - GATE_LAG if n >= GATE_LAG else None
            w_row = None if t_v is None else w_reps[t_v // tg][t_v % tg:t_v % tg + 1]
            z, out = token_pair(t_u, t_v, w_row)
            if t_u is not None:
                zs.append(z)
            if t_v is not None:
                outs[t_v] = out
            if n % tg == 0 and tg <= n <= tb:
                g = n // tg - 1
                w_reps[g] = gate_rows(g, jnp.concatenate(zs[g * tg:(g + 1) * tg], axis=0))
        for t in range(tb):
            for r in range(SLAB):
                o_ref[t:t + 1, r * LANES:(r + 1) * LANES] = outs[t][r:r + 1, :]

        @pl.when(i == nsteps - 1)
        def _():
            wait_tokens(1 - slot, range(tb))

    @pl.when((i & 1) == 0)
    def _():
        consume(0)

    @pl.when((i & 1) == 1)
    def _():
        consume(1)


def _peer_expert(idx_flat, uv, h_slab, gate, tb):
    t = h_slab.shape[0]
    nsteps = t // tb
    tile = SLAB * SLAB
    seg = (lax.broadcasted_iota(I32, (tile, tile), 0) // SLAB
           == lax.broadcasted_iota(I32, (tile, tile), 1) // SLAB).astype(BF16)
    grpt = (lax.broadcasted_iota(I32, (PEER_SEL, PEER_SEL * SLAB), 0)
            == lax.broadcasted_iota(I32, (PEER_SEL, PEER_SEL * SLAB), 1) // SLAB).astype(BF16)
    blk = tb * PEER_SEL
    return pl.pallas_call(
        functools.partial(_peer_expert_kernel, tb=tb, tg=min(4, tb)),
        grid=(nsteps,),
        in_specs=[pl.BlockSpec((blk,), lambda i: (i,), memory_space=pltpu.SMEM),
                  pl.BlockSpec((blk,), lambda i: (jnp.minimum(i + 1, nsteps - 1),),
                               memory_space=pltpu.SMEM),
                  pl.BlockSpec(memory_space=pl.ANY),
                  pl.BlockSpec((tb, SLAB, LANES), lambda i: (i, 0, 0)),
                  pl.BlockSpec((tb, PEER_SEL), lambda i: (i, 0)),
                  pl.BlockSpec(seg.shape, lambda i: (0, 0)),
                  pl.BlockSpec(grpt.shape, lambda i: (0, 0))],
        out_specs=pl.BlockSpec((tb, SLAB * LANES), lambda i: (i, 0)),
        out_shape=jax.ShapeDtypeStruct((t, SLAB * LANES), F32),
        scratch_shapes=[pltpu.VMEM((2, tb, PEER_SEL, 2 * SLAB, LANES), BF16),
                        pltpu.SemaphoreType.DMA((2, tb))],
        compiler_params=_cparams(("arbitrary",)),
        name="peer_expert",
    )(idx_flat, idx_flat, uv, h_slab, gate, seg, grpt)


def _final_kernel(x1_ref, pe_ref, g2_ref, gn_ref, o_ref):
    x2 = x1_ref[0] + g2_ref[0] * pe_ref[0]
    o_ref[0] = x2 * lax.rsqrt(jnp.mean(x2 * x2, axis=-1, keepdims=True) + EPS) * gn_ref[...]


def _final(x1, pe, g2, gain, tm):
    b, s, d = x1.shape
    tok = pl.BlockSpec((1, tm, d), lambda i, j: (i, j, 0))
    return pl.pallas_call(
        _final_kernel,
        grid=(b, s // tm),
        in_specs=[tok, tok, pl.BlockSpec((1, 1, d), lambda i, j: (i, 0, 0)),
                  pl.BlockSpec((1, d), lambda i, j: (0, 0))],
        out_specs=tok,
        out_shape=jax.ShapeDtypeStruct((b, s, d), F32),
        compiler_params=_cparams(("arbitrary", "arbitrary")),
        name="final_norm",
    )(x1, pe, g2, gain)


def kernel(x, c, ctx, c_ctx, w_ada, b_ada, norm_attn, w_in, diff_lambda_q1, diff_lambda_k1,
           diff_lambda_q2, diff_lambda_k2, diff_norm, swa_sink, swa_norm, w_out, norm_ffn,
           peer_w_q, peer_sub_keys, peer_u, peer_v, final_norm):
    b, s, d = x.shape
    assert w_ada.shape[0] == 1, "single layer only"
    t = b * s

    cc = jnp.zeros((8, d), F32).at[:b].set(c).at[b].set(c_ctx)
    mod = _adaln(cc, w_ada[0], b_ada[0])
    sh1, sc1, g1, sh2, sc2, g2 = [m[:, None, :] for m in jnp.split(mod, 6, axis=-1)]

    w_in_b = w_in[0].astype(BF16)
    gain_attn = norm_attn[0].reshape(1, d)
    cos_t, sin_t = _rope_tables(s)
    tm = min(512, s)
    dvw = DIFF_HEADS * LANES
    kvw = SWA_KV * DH
    w_qk = jnp.concatenate([w_in_b[:, :2 * dvw], w_in_b[:, 3 * dvw:-kvw]], axis=1)
    w_vt = jnp.concatenate([w_in_b[:, 2 * dvw:3 * dvw], w_in_b[:, -kvw:]], axis=1).T
    p, vt = _inproj(x, sh1[:b], sc1[:b], gain_attn, w_qk, w_vt, cos_t, sin_t, True, tm)
    lc = ctx.shape[1]
    ones = jnp.ones((lc, LANES), F32)
    ctx_sh = jnp.broadcast_to(sh1[b:b + 1], (b, 1, d))
    ctx_sc = jnp.broadcast_to(sc1[b:b + 1], (b, 1, d))
    pc, vtc = _inproj(ctx, ctx_sh, ctx_sc, gain_attn, w_qk, w_vt, ones, ones, False, lc)

    lams = [v[0].reshape(1, DH).astype(F32) for v in
            (diff_lambda_q1, diff_lambda_k1, diff_lambda_q2, diff_lambda_k2)]
    a_diff = _diff_attn(p, vt, pc, vtc, lams, diff_norm[0].reshape(1, -1), min(512, s), min(512, s))
    a_swa = _swa_attn(p, vt, pc, vtc, swa_sink[0].astype(F32), swa_norm[0].reshape(1, -1))

    w_out_b = w_out[0].astype(BF16)
    dw = a_diff.shape[2]
    x1, h2 = _outproj(a_diff, a_swa, w_out_b[:dw], w_out_b[dw:], x, g1[:b], sh2[:b], sc2[:b],
                      norm_ffn[0].reshape(1, d), tm)

    idx_t, gate_t = _peer_route(h2.reshape(t, d), peer_w_q[0].astype(BF16),
                                peer_sub_keys[0].astype(BF16), min(256, t))
    uv = jnp.concatenate([peer_u[0].reshape(-1, SLAB, LANES),
                          peer_v[0].reshape(-1, SLAB, LANES)], axis=1).astype(BF16)
    pe = _peer_expert(idx_t.T.reshape(-1), uv, h2.reshape(t, SLAB, LANES), gate_t.T, 16)

    return _final(x1, pe.reshape(b, s, d), g2[:b], final_norm.reshape(1, d), tm)
```

```python
import functools
import math

import jax
import jax.numpy as jnp
from jax import lax
from jax.experimental import pallas as pl
from jax.experimental.pallas import tpu as pltpu

F32 = jnp.float32
BF16 = jnp.bfloat16
I32 = jnp.int32

EPS = 1e-6
NEG_INF = -1e30
ROPE_THETA = 10000.0
GRID_W = 64
ROPE_PAIRS = 16

LANES = 128
DH = 64
DIFF_HEADS = 8
SWA_HEADS = 16
SWA_KV = 4
SWA_GROUP = SWA_HEADS // SWA_KV
BAND = 128
QK_CHUNKS = 26
LAM_INIT = 0.8 - 0.6 * math.exp(-0.3 * 0)
LOG2E = math.log2(math.e)

PEER_HEADS = 8
PEER_TOPK = 16
N_KEYS = 128
PEER_SEL = PEER_HEADS * PEER_TOPK
SLAB = 16
PRE_ISSUE = 2
GATE_LAG = 6

VMEM_LIMIT = 56 * 1024 * 1024


def _cparams(sem):
    return pltpu.CompilerParams(dimension_semantics=sem, vmem_limit_bytes=VMEM_LIMIT)


def _adaln_kernel(c_ref, w_ref, b_ref, o_ref):
    c = c_ref[...]
    s = c * (1.0 / (1.0 + jnp.exp(-c)))
    rows = s.shape[0]
    s_hi, s_lo = _split_bf16(s)
    w_hi, w_lo = _split_bf16(w_ref[...])
    both = jnp.dot(jnp.concatenate([s_hi, s_lo], axis=0), w_hi, preferred_element_type=F32)
    o_ref[...] = (both[:rows] + both[rows:] + jnp.dot(s_hi, w_lo, preferred_element_type=F32)
                  + b_ref[...])


def _adaln(cc, w, b):
    rows, d = cc.shape
    n = w.shape[1]
    tn = 1024
    return pl.pallas_call(
        _adaln_kernel,
        grid=(n // tn,),
        in_specs=[pl.BlockSpec((rows, d), lambda j: (0, 0)),
                  pl.BlockSpec((d, tn), lambda j: (0, j)),
                  pl.BlockSpec((1, tn), lambda j: (0, j))],
        out_specs=pl.BlockSpec((rows, tn), lambda j: (0, j)),
        out_shape=jax.ShapeDtypeStruct((rows, n), F32),
        compiler_params=_cparams(("arbitrary",)),
        name="adaln",
    )(cc, w, b.reshape(1, n))


def _modnorm(x, gain, shift, scale):
    y = x * lax.rsqrt(jnp.mean(x * x, axis=-1, keepdims=True) + EPS)
    return (y * gain) * (1.0 + scale) + shift


def _swap16(p):
    lane = lax.broadcasted_iota(I32, p.shape, 1)
    up = pltpu.roll(p, LANES - 16, 1)
    dn = pltpu.roll(p, 16, 1)
    return jnp.where((lane & 31) < 16, up, dn)


def _inproj_kernel(x_ref, sh_ref, sc_ref, g_ref, w_ref, wvt_ref, cos_ref, sin_ref, o_ref, vt_ref, *,
                   rope_chunks, chunk_scale):
    h = _modnorm(x_ref[0], g_ref[...], sh_ref[0], sc_ref[0]).astype(BF16)
    p = jnp.dot(h, w_ref[...], preferred_element_type=F32)
    n_chunks = p.shape[1] // LANES
    if rope_chunks:
        cs = cos_ref[...]
        sn = sin_ref[...]
    for j in range(n_chunks):
        pj = p[:, j * LANES:(j + 1) * LANES]
        if j in rope_chunks:
            pj = pj * cs + _swap16(pj) * sn
        if j in chunk_scale:
            pj = pj * chunk_scale[j]
        o_ref[0, :, j * LANES:(j + 1) * LANES] = pj.astype(BF16)
    vt_ref[0] = lax.dot_general(wvt_ref[...], h, (((1,), (1,)), ((), ())),
                                preferred_element_type=F32).astype(BF16)


def _inproj(x, shift, scale, gain, w, w_vt, cos_t, sin_t, rope, tm):
    b, s, d = x.shape
    n = w.shape[1]
    nv = w_vt.shape[0]
    assert n == QK_CHUNKS * LANES
    rope_chunks = frozenset(range(QK_CHUNKS)) if rope else frozenset()
    chunk_scale = {j: DH ** -0.5 * LOG2E for j in range(0, 8)}
    chunk_scale.update({j: DH ** -0.5 for j in range(16, 24)})
    kern = functools.partial(_inproj_kernel, rope_chunks=rope_chunks, chunk_scale=chunk_scale)
    return pl.pallas_call(
        kern,
        grid=(b, s // tm),
        in_specs=[pl.BlockSpec((1, tm, d), lambda i, j: (i, j, 0)),
                  pl.BlockSpec((1, 1, d), lambda i, j: (i, 0, 0)),
                  pl.BlockSpec((1, 1, d), lambda i, j: (i, 0, 0)),
                  pl.BlockSpec((1, d), lambda i, j: (0, 0)),
                  pl.BlockSpec((d, n), lambda i, j: (0, 0), pipeline_mode=pl.Buffered(1)),
                  pl.BlockSpec((nv, d), lambda i, j: (0, 0), pipeline_mode=pl.Buffered(1)),
                  pl.BlockSpec((tm, LANES), lambda i, j: (j, 0)),
                  pl.BlockSpec((tm, LANES), lambda i, j: (j, 0))],
        out_specs=[pl.BlockSpec((1, tm, n), lambda i, j: (i, j, 0)),
                   pl.BlockSpec((1, nv, tm), lambda i, j: (i, 0, j))],
        out_shape=[jax.ShapeDtypeStruct((b, s, n), BF16), jax.ShapeDtypeStruct((b, nv, s), BF16)],
        compiler_params=_cparams(("arbitrary", "arbitrary")),
        name="in_proj",
    )(x, shift, scale, gain, w, w_vt, cos_t, sin_t)


def _rope_tables(s):
    rows = s // GRID_W
    row = jnp.repeat(jnp.arange(rows, dtype=F32), GRID_W)
    col = jnp.tile(jnp.arange(GRID_W, dtype=F32), rows)
    freqs = ROPE_THETA ** (-jnp.arange(ROPE_PAIRS, dtype=F32) / ROPE_PAIRS)
    ar = row[:, None] * freqs
    ac = col[:, None] * freqs
    cos64 = jnp.concatenate([jnp.cos(ar), jnp.cos(ar), jnp.cos(ac), jnp.cos(ac)], axis=1)
    sin64 = jnp.concatenate([-jnp.sin(ar), jnp.sin(ar), -jnp.sin(ac), jnp.sin(ac)], axis=1)
    return jnp.tile(cos64, (1, 2)), jnp.tile(sin64, (1, 2))


def _head_rms(o, gain):
    return o * lax.rsqrt(jnp.mean(o * o, axis=-1, keepdims=True) + EPS) * gain


def _diff_attn_kernel(lq1_ref, lk1_ref, lq2_ref, lk2_ref, q_ref, k_ref, vt_ref, kc_ref, vtc_ref,
                      gain_ref, o_ref, s_ref, m_ref, *, tk, n_tiles):
    j = pl.program_id(2)
    tq = q_ref.shape[1]
    s_len = k_ref.shape[1]
    lc = kc_ref.shape[1]
    blocks = [(0, lc, lambda: kc_ref[0], lambda: vtc_ref[0])]
    blocks += [(lc + b * tk, tk, lambda b=b: k_ref[0, b * tk:(b + 1) * tk, :],
                lambda b=b: vt_ref[0, :, b * tk:(b + 1) * tk]) for b in range(s_len // tk)]
    m_init = jnp.full((1, tq), NEG_INF, F32)

    def query_maps():
        q = q_ref[0]
        lane = lax.broadcasted_iota(I32, q.shape, 1)
        zero = jnp.zeros_like(q)
        return jnp.where(lane < DH, q, zero), jnp.where(lane >= DH, q, zero)

    def score_block(slot, i, blk, qm, m):
        r0, rows, kblk, _ = blk
        s = lax.dot_general(kblk(), qm, (((1,), (1,)), ((), ())), preferred_element_type=F32)
        s_ref[slot, i, r0:r0 + rows, :] = s
        return jnp.maximum(m, jnp.max(s, axis=0, keepdims=True))

    def prob_block(slot, i, blk, m, l, acc):
        r0, rows, _, vtblk = blk
        p = jnp.exp2(s_ref[slot, i, r0:r0 + rows, :] - m)
        part = jnp.dot(vtblk(), p.astype(BF16), preferred_element_type=F32)
        psum = jnp.sum(p, axis=0, keepdims=True)
        return (psum if l is None else l + psum), (part if acc is None else acc + part)

    def finish(l, acc):
        lam = (jnp.exp(jnp.sum(lq1_ref[...] * lk1_ref[...], keepdims=True))
               - jnp.exp(jnp.sum(lq2_ref[...] * lk2_ref[...], keepdims=True)) + LAM_INIT)
        out = (acc[0] / l[0] - lam * (acc[1] / l[1])).T
        o_ref[0] = (_head_rms(out, gain_ref[...]) * (1.0 - LAM_INIT)).astype(BF16)

    def step(score_slot, prob_slot):
        if score_slot is not None:
            qm = query_maps()
            m_new = [m_init, m_init]
        if prob_slot is not None:
            m_old = [m_ref[prob_slot, 0], m_ref[prob_slot, 1]]
            l, acc = [None, None], [None, None]
        for blk in blocks:
            for i in range(2):
                if score_slot is not None:
                    m_new[i] = score_block(score_slot, i, blk, qm[i], m_new[i])
            for i in range(2):
                if prob_slot is not None:
                    l[i], acc[i] = prob_block(prob_slot, i, blk, m_old[i], l[i], acc[i])
        if score_slot is not None:
            m_ref[score_slot, 0] = m_new[0]
            m_ref[score_slot, 1] = m_new[1]
        if prob_slot is not None:
            finish(l, acc)

    @pl.when(j == 0)
    def _():
        step(0, None)

    for parity in range(2):
        @pl.when((j > 0) & (j < n_tiles) & ((j & 1) == parity))
        def _():
            step(parity, 1 - parity)

    @pl.when(j == n_tiles)
    def _():
        step(None, (n_tiles - 1) % 2)


def _diff_attn(p, vt, pc, vtc, lams, gain, tq, tk):
    b, s, _ = p.shape
    lc = pc.shape[1]
    n_tiles = s // tq
    lam_spec = pl.BlockSpec((1, DH), lambda i, h, j: (0, 0))
    return pl.pallas_call(
        functools.partial(_diff_attn_kernel, tk=tk, n_tiles=n_tiles),
        grid=(b, DIFF_HEADS, n_tiles + 1),
        in_specs=[lam_spec, lam_spec, lam_spec, lam_spec,
                  pl.BlockSpec((1, tq, LANES), lambda i, h, j: (i, jnp.minimum(j, n_tiles - 1), h)),
                  pl.BlockSpec((1, s, LANES), lambda i, h, j: (i, 0, 8 + h)),
                  pl.BlockSpec((1, LANES, s), lambda i, h, j: (i, h, 0)),
                  pl.BlockSpec((1, lc, LANES), lambda i, h, j: (i, 0, 8 + h)),
                  pl.BlockSpec((1, LANES, lc), lambda i, h, j: (i, h, 0)),
                  pl.BlockSpec((1, LANES), lambda i, h, j: (0, h))],
        out_specs=pl.BlockSpec((1, tq, LANES), lambda i, h, j: (i, jnp.maximum(j - 1, 0), h)),
        out_shape=jax.ShapeDtypeStruct((b, s, DIFF_HEADS * LANES), BF16),
        scratch_shapes=[pltpu.VMEM((2, 2, lc + s, tq), F32), pltpu.VMEM((2, 2, 1, tq), F32)],
        compiler_params=_cparams(("arbitrary", "arbitrary", "arbitrary")),
        name="diff_attn",
    )(*lams, p, p, vt, pc, vtc, gain)


def _swa_attn_kernel(sink_ref, q_ref, k_ref, vt_ref, kc_ref, vtc_ref, gain_ref, o_ref,
                     kcat_ref, vtcat_ref):
    n = pl.program_id(1)
    nb = pl.num_programs(1)
    s_len = k_ref.shape[1]
    lc = kc_ref.shape[1]
    n_band = 3 * BAND

    @pl.when(n == 0)
    def _():
        kcat_ref[n_band:, :] = kc_ref[0]
        vtcat_ref[:, n_band:] = vtc_ref[0]

    prev = pl.multiple_of(jnp.maximum(n - 1, 0) * BAND, BAND)
    cur = pl.multiple_of(n * BAND, BAND)
    nxt = pl.multiple_of(jnp.minimum(n + 1, nb - 1) * BAND, BAND)
    for t, off in enumerate((prev, cur, nxt)):
        kcat_ref[t * BAND:(t + 1) * BAND, :] = k_ref[0, pl.ds(off, BAND), :]
        vtcat_ref[:, t * BAND:(t + 1) * BAND] = vt_ref[0, :, pl.ds(off, BAND)]

    keys = n_band + lc
    cols = SWA_GROUP * BAND
    ki = lax.broadcasted_iota(I32, (keys, cols), 0)
    qi = lax.broadcasted_iota(I32, (keys, cols), 1) & (BAND - 1)
    kpos = (n - 1) * BAND + ki
    in_band = jnp.where(jnp.abs(qi + BAND - ki) <= BAND,
                        jnp.where(kpos >= 0, jnp.where(kpos < s_len, 1, 0), 0), 0)
    ok = jnp.where(ki >= n_band, 1, in_band) > 0

    q = q_ref[0]
    outs = []
    for g in range(SWA_KV):
        kg = kcat_ref[:, g * DH:(g + 1) * DH]
        vtg = vtcat_ref[g * DH:(g + 1) * DH, :]
        qg = jnp.concatenate(
            [q[:, (g * SWA_GROUP + j) * DH:(g * SWA_GROUP + j + 1) * DH] for j in range(SWA_GROUP)],
            axis=0)
        s = lax.dot_general(kg, qg, (((1,), (1,)), ((), ())), preferred_element_type=F32)
        s = jnp.where(ok, s, NEG_INF)
        sink = jnp.concatenate(
            [jnp.full((1, BAND), sink_ref[g * SWA_GROUP + j], F32) for j in range(SWA_GROUP)],
            axis=1)
        m = jnp.maximum(jnp.max(s, axis=0, keepdims=True), sink)
        e = jnp.exp(s - m)
        denom = jnp.sum(e, axis=0, keepdims=True) + jnp.exp(sink - m)
        o = jnp.dot(vtg, e.astype(BF16), preferred_element_type=F32) / denom
        outs.append(o * lax.rsqrt(jnp.mean(o * o, axis=0, keepdims=True) + EPS))
    o_all = jnp.concatenate(outs, axis=0).T
    for g in range(SWA_KV):
        for j in range(SWA_GROUP):
            c0 = (g * SWA_GROUP + j) * DH
            o_ref[0, :, c0:c0 + DH] = (o_all[j * BAND:(j + 1) * BAND, g * DH:(g + 1) * DH]
                                       * gain_ref[:, c0:c0 + DH]).astype(BF16)


def _swa_attn(p, vt, pc, vtc, sink, gain):
    b, s, _ = p.shape
    lc = pc.shape[1]
    kvw = SWA_KV * DH
    vblk = vt.shape[1] // kvw - 1
    return pl.pallas_call(
        _swa_attn_kernel,
        grid=(b, s // BAND),
        in_specs=[pl.BlockSpec(memory_space=pltpu.SMEM),
                  pl.BlockSpec((1, BAND, SWA_HEADS * DH), lambda i, j: (i, j, 2)),
                  pl.BlockSpec((1, s, kvw), lambda i, j: (i, 0, 12)),
                  pl.BlockSpec((1, kvw, s), lambda i, j: (i, vblk, 0)),
                  pl.BlockSpec((1, lc, kvw), lambda i, j: (i, 0, 12)),
                  pl.BlockSpec((1, kvw, lc), lambda i, j: (i, vblk, 0)),
                  pl.BlockSpec((1, SWA_HEADS * DH), lambda i, j: (0, 0))],
        out_specs=pl.BlockSpec((1, BAND, SWA_HEADS * DH), lambda i, j: (i, j, 0)),
        out_shape=jax.ShapeDtypeStruct((b, s, SWA_HEADS * DH), BF16),
        scratch_shapes=[pltpu.VMEM((3 * BAND + lc, kvw), BF16),
                        pltpu.VMEM((kvw, 3 * BAND + lc), BF16)],
        compiler_params=_cparams(("arbitrary", "arbitrary")),
        name="swa_attn",
    )(sink, p, p, vt, pc, vtc, gain)


def _outproj_kernel(ad_ref, as_ref, wd_ref, ws_ref, x_ref, g1_ref, sh_ref, sc_ref, gn_ref,
                    x1_ref, h2_ref):
    a = (jnp.dot(ad_ref[0], wd_ref[...], preferred_element_type=F32)
         + jnp.dot(as_ref[0], ws_ref[...], preferred_element_type=F32))
    x1 = x_ref[0] + g1_ref[0] * a
    x1_ref[0] = x1
    h2_ref[0] = _modnorm(x1, gn_ref[...], sh_ref[0], sc_ref[0]).astype(BF16)


def _outproj(a_diff, a_swa, w_d, w_s, x, g1, sh2, sc2, gain, tm):
    b, s, d = x.shape
    wd = a_diff.shape[2]
    vec = pl.BlockSpec((1, 1, d), lambda i, j: (i, 0, 0))
    return pl.pallas_call(
        _outproj_kernel,
        grid=(b, s // tm),
        in_specs=[pl.BlockSpec((1, tm, wd), lambda i, j: (i, j, 0)),
                  pl.BlockSpec((1, tm, wd), lambda i, j: (i, j, 0)),
                  pl.BlockSpec((wd, d), lambda i, j: (0, 0)),
                  pl.BlockSpec((wd, d), lambda i, j: (0, 0)),
                  pl.BlockSpec((1, tm, d), lambda i, j: (i, j, 0)),
                  vec, vec, vec,
                  pl.BlockSpec((1, d), lambda i, j: (0, 0))],
        out_specs=[pl.BlockSpec((1, tm, d), lambda i, j: (i, j, 0)),
                   pl.BlockSpec((1, tm, d), lambda i, j: (i, j, 0))],
        out_shape=[jax.ShapeDtypeStruct((b, s, d), F32), jax.ShapeDtypeStruct((b, s, d), BF16)],
        compiler_params=_cparams(("arbitrary", "arbitrary")),
        name="out_proj",
    )(a_diff, a_swa, w_d, w_s, x, g1, sh2, sc2, gain)


def _extract_topk(s, rank, payload, k):
    vals, pays = [], []
    for _ in range(k):
        m = jnp.max(s, axis=0, keepdims=True)
        first = jnp.min(jnp.where(s == m, rank, jnp.inf), axis=0, keepdims=True)
        hit = rank == first
        vals.append(m)
        pays.append(first if payload is None
                    else jnp.sum(jnp.where(hit, payload, 0.0), axis=0, keepdims=True))
        s = jnp.where(hit, -jnp.inf, s)
    return jnp.concatenate(vals, axis=0), jnp.concatenate(pays, axis=0)


def _extract_topk_paired(s, k):
    half = s.shape[0] // 2
    a, b = s[:half], s[half:]
    ra = lax.broadcasted_iota(I32, a.shape, 0).astype(F32)
    rb = ra + float(half)
    b_wins = b > a
    w, l = jnp.where(b_wins, b, a), jnp.where(b_wins, a, b)
    rw, rl = jnp.where(b_wins, rb, ra), jnp.where(b_wins, ra, rb)
    vals, ids = [], []
    for _ in range(k):
        m = jnp.max(w, axis=0, keepdims=True)
        first = jnp.min(jnp.where(w == m, rw, jnp.inf), axis=0, keepdims=True)
        hit = rw == first
        vals.append(m)
        ids.append(first)
        w = jnp.where(hit, l, w)
        rw = jnp.where(hit, rl, rw)
        l = jnp.where(hit, -jnp.inf, l)
    return jnp.concatenate(vals, axis=0), jnp.concatenate(ids, axis=0)


def _pair_candidates(s1, i1, s2, i2):
    k, tt = s1.shape
    assert k == 16, "block layout below is written for 16 x 16 pairs"
    sub = lax.broadcasted_iota(I32, (8, tt), 0)
    vals = [s1[0:1] + s2, s1[1:2] + s2[0:8]]
    pos = [lax.broadcasted_iota(I32, (k, tt), 0), k + sub]
    ids = [i1[0:1] * N_KEYS + i2, i1[1:2] * N_KEYS + i2[0:8]]
    for a in range(2, 8):
        keep = sub < k // (a + 1)
        vals.append(jnp.where(keep, s1[a:a + 1] + s2[0:8], -jnp.inf))
        pos.append(a * k + sub)
        ids.append(i1[a:a + 1] * N_KEYS + i2[0:8])
    vals.append(s1[8:16] + s2[0:1])
    pos.append((sub + 8) * k)
    ids.append(i1[8:16] * N_KEYS + i2[0:1])
    return (jnp.concatenate(vals, axis=0), jnp.concatenate(pos, axis=0).astype(F32),
            jnp.concatenate(ids, axis=0))


def _peer_route_kernel(h_ref, wq_ref, keys_ref, idx_ref, gate_ref):
    q = jnp.dot(h_ref[...], wq_ref[...], preferred_element_type=F32).astype(BF16)
    for h in range(PEER_HEADS):
        halves = []
        for i in range(2):
            c0 = (h * 2 + i) * N_KEYS
            s = lax.dot_general(keys_ref[h, i], q[:, c0:c0 + N_KEYS], (((1,), (1,)), ((), ())),
                                preferred_element_type=F32)
            halves.append(_extract_topk_paired(s, PEER_TOPK))
        (s1, i1), (s2, i2) = halves
        cand, pos, cidx = _pair_candidates(s1, i1, s2, i2)
        top_s, top_i = _extract_topk(cand, pos, cidx, PEER_TOPK)
        e = jnp.exp(top_s - jnp.max(top_s, axis=0, keepdims=True))
        r0 = h * PEER_TOPK
        idx_ref[r0:r0 + PEER_TOPK, :] = top_i.astype(I32)
        gate_ref[r0:r0 + PEER_TOPK, :] = e / jnp.sum(e, axis=0, keepdims=True)


def _peer_route(h2, wq, keys, tt):
    t, d = h2.shape
    nq = wq.shape[1]
    return pl.pallas_call(
        _peer_route_kernel,
        grid=(t // tt,),
        in_specs=[pl.BlockSpec((tt, d), lambda i: (i, 0)),
                  pl.BlockSpec((d, nq), lambda i: (0, 0)),
                  pl.BlockSpec(keys.shape, lambda i: (0, 0, 0, 0))],
        out_specs=[pl.BlockSpec((PEER_SEL, tt), lambda i: (0, i)),
                   pl.BlockSpec((PEER_SEL, tt), lambda i: (0, i))],
        out_shape=[jax.ShapeDtypeStruct((PEER_SEL, t), I32), jax.ShapeDtypeStruct((PEER_SEL, t), F32)],
        compiler_params=_cparams(("arbitrary",)),
        name="peer_route",
    )(h2, wq, keys)


def _split_bf16(x):
    hi = x.astype(BF16)
    return hi, (x - hi.astype(F32)).astype(BF16)


def _peer_expert_kernel(idx_cur_ref, idx_nxt_ref, uv_ref, h_ref, gate_ref, seg_ref, grpt_ref,
                        o_ref, buf_ref, sem_ref, *, tb, tg):
    i = pl.program_id(0)
    nsteps = pl.num_programs(0)

    def row_copy(idx_ref, slot, t, j):
        return pltpu.make_async_copy(uv_ref.at[idx_ref[t * PEER_SEL + j]],
                                     buf_ref.at[slot, t, j], sem_ref.at[slot, t])

    def wait_tokens(slot, toks):
        for t in toks:
            for j in range(PEER_SEL):
                row_copy(idx_cur_ref, slot, t, j).wait()

    @pl.when(i == 0)
    def _():
        for t in range(tb):
            for j in range(PEER_SEL):
                row_copy(idx_cur_ref, 0, t, j).start(priority=j % 2)

    diag = (lax.broadcasted_iota(I32, (SLAB, PEER_SEL * SLAB), 1) & (SLAB - 1)) == \
        lax.broadcasted_iota(I32, (SLAB, PEER_SEL * SLAB), 0)

    def consume(slot):
        for t in range(PRE_ISSUE):
            for j in range(PEER_SEL):
                row_copy(idx_nxt_ref, 1 - slot, t, j).start(priority=j % 2)
        wait_tokens(slot, range(tb))
        g_hi, g_lo = _split_bf16(gate_ref[...])
        gate_rep = (jnp.dot(g_hi, grpt_ref[...], preferred_element_type=F32)
                    + jnp.dot(g_lo, grpt_ref[...], preferred_element_type=F32))

        def token_pair(t_u, t_v, w_row):
            z_parts, r = [], None
            if t_v is not None:
                wt = jnp.where(diag, jnp.broadcast_to(w_row, diag.shape), 0.0)
                wt_hi, wt_lo = _split_bf16(wt)
                wt2 = jnp.concatenate([wt_hi, wt_lo], axis=0)
            for n in range(PEER_SEL // SLAB):
                e0, c0 = n * SLAB, n * SLAB * SLAB
                if t_u is not None:
                    for j in range(e0, e0 + SLAB // 2) if t_u >= PRE_ISSUE else ():
                        row_copy(idx_nxt_ref, 1 - slot, t_u, j).start(priority=j % 2)
                    u_tile = buf_ref[slot, t_u, e0:e0 + SLAB, 0:SLAB, :].reshape(SLAB * SLAB, LANES)
                    y = lax.dot_general(h_ref[t_u], u_tile, (((1,), (1,)), ((), ())),
                                        preferred_element_type=F32)
                    z_parts.append(jnp.sum(jnp.where(diag[:, :SLAB * SLAB], y, 0.0),
                                           axis=0, keepdims=True))
                    for j in range(e0 + SLAB // 2, e0 + SLAB) if t_u >= PRE_ISSUE else ():
                        row_copy(idx_nxt_ref, 1 - slot, t_u, j).start(priority=j % 2)
                if t_v is not None:
                    v_tile = buf_ref[slot, t_v, e0:e0 + SLAB, SLAB:2 * SLAB, :].reshape(
                        SLAB * SLAB, LANES)
                    part = jnp.dot(wt2[:, c0:c0 + SLAB * SLAB], v_tile, preferred_element_type=F32)
                    r = part if r is None else r + part
            z = jnp.concatenate(z_parts, axis=1) if z_parts else None
            out = None if r is None else r[:SLAB] + r[SLAB:]
            return z, out

        def gate_rows(g, z):
            tile = SLAB * SLAB
            z8 = jnp.concatenate([z, jnp.zeros((8 - tg, z.shape[1]), F32)], axis=0) if tg < 8 else z
            stacked = jnp.concatenate([z8[:, n * tile:(n + 1) * tile]
                                       for n in range(z.shape[1] // tile)], axis=0)
            s_hi, s_lo = _split_bf16(stacked)
            seg = jnp.dot(jnp.concatenate([s_hi, s_lo], axis=0), seg_ref[...],
                          preferred_element_type=F32)
            seg = seg[:stacked.shape[0]] + seg[stacked.shape[0]:]
            act = jnp.concatenate([seg[8 * n:8 * n + tg] for n in range(z.shape[1] // tile)], axis=1)
            gelu = 0.5 * act * (1.0 + jnp.tanh(math.sqrt(2.0 / math.pi)
                                               * (act + 0.044715 * act * act * act)))
            return gate_rep[g * tg:(g + 1) * tg, :] * gelu

        assert GATE_LAG >= tg + 2
        w_reps, outs, zs = {}, {}, []
        for n in range(tb + GATE_LAG):
            t_u = n if n < tb else None
            t_v = n - GATE_LAG if n >= GATE_LAG else None
            w_row = None if t_v is None else w_reps[t_v // tg][t_v % tg:t_v % tg + 1]
            z, out = token_pair(t_u, t_v, w_row)
            if t_u is not None:
                zs.append(z)
            if t_v is not None:
                outs[t_v] = out
            if n % tg == 0 and tg <= n <= tb:
                g = n // tg - 1
                w_reps[g] = gate_rows(g, jnp.concatenate(zs[g * tg:(g + 1) * tg], axis=0))
        for t in range(tb):
            for r in range(SLAB):
                o_ref[t:t + 1, r * LANES:(r + 1) * LANES] = outs[t][r:r + 1, :]

        @pl.when(i == nsteps - 1)
        def _():
            wait_tokens(1 - slot, range(tb))

    @pl.when((i & 1) == 0)
    def _():
        consume(0)

    @pl.when((i & 1) == 1)
    def _():
        consume(1)


def _peer_expert(idx_flat, uv, h_slab, gate, tb):
    t = h_slab.shape[0]
    nsteps = t // tb
    tile = SLAB * SLAB
    seg = (lax.broadcasted_iota(I32, (tile, tile), 0) // SLAB
           == lax.broadcasted_iota(I32, (tile, tile), 1) // SLAB).astype(BF16)
    grpt = (lax.broadcasted_iota(I32, (PEER_SEL, PEER_SEL * SLAB), 0)
            == lax.broadcasted_iota(I32, (PEER_SEL, PEER_SEL * SLAB), 1) // SLAB).astype(BF16)
    blk = tb * PEER_SEL
    return pl.pallas_call(
        functools.partial(_peer_expert_kernel, tb=tb, tg=min(4, tb)),
        grid=(nsteps,),
        in_specs=[pl.BlockSpec((blk,), lambda i: (i,), memory_space=pltpu.SMEM),
                  pl.BlockSpec((blk,), lambda i: (jnp.minimum(i + 1, nsteps - 1),),
                               memory_space=pltpu.SMEM),
                  pl.BlockSpec(memory_space=pl.ANY),
                  pl.BlockSpec((tb, SLAB, LANES), lambda i: (i, 0, 0)),
                  pl.BlockSpec((tb, PEER_SEL), lambda i: (i, 0)),
                  pl.BlockSpec(seg.shape, lambda i: (0, 0)),
                  pl.BlockSpec(grpt.shape, lambda i: (0, 0))],
        out_specs=pl.BlockSpec((tb, SLAB * LANES), lambda i: (i, 0)),
        out_shape=jax.ShapeDtypeStruct((t, SLAB * LANES), F32),
        scratch_shapes=[pltpu.VMEM((2, tb, PEER_SEL, 2 * SLAB, LANES), BF16),
                        pltpu.SemaphoreType.DMA((2, tb))],
        compiler_params=_cparams(("arbitrary",)),
        name="peer_expert",
    )(idx_flat, idx_flat, uv, h_slab, gate, seg, grpt)


def _final_kernel(x1_ref, pe_ref, g2_ref, gn_ref, o_ref):
    x2 = x1_ref[0] + g2_ref[0] * pe_ref[0]
    o_ref[0] = x2 * lax.rsqrt(jnp.mean(x2 * x2, axis=-1, keepdims=True) + EPS) * gn_ref[...]


def _final(x1, pe, g2, gain, tm):
    b, s, d = x1.shape
    tok = pl.BlockSpec((1, tm, d), lambda i, j: (i, j, 0))
    return pl.pallas_call(
        _final_kernel,
        grid=(b, s // tm),
        in_specs=[tok, tok, pl.BlockSpec((1, 1, d), lambda i, j: (i, 0, 0)),
                  pl.BlockSpec((1, d), lambda i, j: (0, 0))],
        out_specs=tok,
        out_shape=jax.ShapeDtypeStruct((b, s, d), F32),
        compiler_params=_cparams(("arbitrary", "arbitrary")),
        name="final_norm",
    )(x1, pe, g2, gain)


def kernel(x, c, ctx, c_ctx, w_ada, b_ada, norm_attn, w_in, diff_lambda_q1, diff_lambda_k1,
           diff_lambda_q2, diff_lambda_k2, diff_norm, swa_sink, swa_norm, w_out, norm_ffn,
           peer_w_q, peer_sub_keys, peer_u, peer_v, final_norm):
    b, s, d = x.shape
    assert w_ada.shape[0] == 1, "single layer only"
    t = b * s

    cc = jnp.zeros((8, d), F32).at[:b].set(c).at[b].set(c_ctx)
    mod = _adaln(cc, w_ada[0], b_ada[0])
    sh1, sc1, g1, sh2, sc2, g2 = [m[:, None, :] for m in jnp.split(mod, 6, axis=-1)]

    w_in_b = w_in[0].astype(BF16)
    gain_attn = norm_attn[0].reshape(1, d)
    cos_t, sin_t = _rope_tables(s)
    tm = min(512, s)
    dvw = DIFF_HEADS * LANES
    kvw = SWA_KV * DH
    w_qk = jnp.concatenate([w_in_b[:, :2 * dvw], w_in_b[:, 3 * dvw:-kvw]], axis=1)
    w_vt = jnp.concatenate([w_in_b[:, 2 * dvw:3 * dvw], w_in_b[:, -kvw:]], axis=1).T
    p, vt = _inproj(x, sh1[:b], sc1[:b], gain_attn, w_qk, w_vt, cos_t, sin_t, True, tm)
    lc = ctx.shape[1]
    ones = jnp.ones((lc, LANES), F32)
    ctx_sh = jnp.broadcast_to(sh1[b:b + 1], (b, 1, d))
    ctx_sc = jnp.broadcast_to(sc1[b:b + 1], (b, 1, d))
    pc, vtc = _inproj(ctx, ctx_sh, ctx_sc, gain_attn, w_qk, w_vt, ones, ones, False, lc)

    lams = [v[0].reshape(1, DH).astype(F32) for v in
            (diff_lambda_q1, diff_lambda_k1, diff_lambda_q2, diff_lambda_k2)]
    a_diff = _diff_attn(p, vt, pc, vtc, lams, diff_norm[0].reshape(1, -1), min(256, s), min(512, s))
    a_swa = _swa_attn(p, vt, pc, vtc, swa_sink[0].astype(F32), swa_norm[0].reshape(1, -1))

    w_out_b = w_out[0].astype(BF16)
    dw = a_diff.shape[2]
    x1, h2 = _outproj(a_diff, a_swa, w_out_b[:dw], w_out_b[dw:], x, g1[:b], sh2[:b], sc2[:b],
                      norm_ffn[0].reshape(1, d), tm)

    idx_t, gate_t = _peer_route(h2.reshape(t, d), peer_w_q[0].astype(BF16),
                                peer_sub_keys[0].astype(BF16), min(256, t))
    uv = jnp.concatenate([peer_u[0].reshape(-1, SLAB, LANES),
                          peer_v[0].reshape(-1, SLAB, LANES)], axis=1).astype(BF16)
    pe = _peer_expert(idx_t.T.reshape(-1), uv, h2.reshape(t, SLAB, LANES), gate_t.T, 16)

    return _final(x1, pe.reshape(b, s, d), g2[:b], final_norm.reshape(1, d), tm)
```

```python
import functools
import math

import jax
import jax.numpy as jnp
from jax import lax
from jax.experimental import pallas as pl
from jax.experimental.pallas import tpu as pltpu

F32 = jnp.float32
BF16 = jnp.bfloat16
I32 = jnp.int32

EPS = 1e-6
NEG_INF = -1e30
ROPE_THETA = 10000.0
GRID_W = 64
ROPE_PAIRS = 16

LANES = 128
DH = 64
DIFF_HEADS = 8
SWA_HEADS = 16
SWA_KV = 4
SWA_GROUP = SWA_HEADS // SWA_KV
BAND = 128
QK_CHUNKS = 26
LAM_INIT = 0.8 - 0.6 * math.exp(-0.3 * 0)
LOG2E = math.log2(math.e)

PEER_HEADS = 8
PEER_TOPK = 16
N_KEYS = 128
PEER_SEL = PEER_HEADS * PEER_TOPK
SLAB = 16
PRE_ISSUE = 2
GATE_LAG = 6

VMEM_LIMIT = 56 * 1024 * 1024


def _cparams(sem):
    return pltpu.CompilerParams(dimension_semantics=sem, vmem_limit_bytes=VMEM_LIMIT)


def _adaln_kernel(c_ref, w_ref, b_ref, o_ref):
    c = c_ref[...]
    s = c * (1.0 / (1.0 + jnp.exp(-c)))
    rows = s.shape[0]
    s_hi, s_lo = _split_bf16(s)
    w_hi, w_lo = _split_bf16(w_ref[...])
    both = jnp.dot(jnp.concatenate([s_hi, s_lo], axis=0), w_hi, preferred_element_type=F32)
    o_ref[...] = (both[:rows] + both[rows:] + jnp.dot(s_hi, w_lo, preferred_element_type=F32)
                  + b_ref[...])


def _adaln(cc, w, b):
    rows, d = cc.shape
    n = w.shape[1]
    tn = 1024
    return pl.pallas_call(
        _adaln_kernel,
        grid=(n // tn,),
        in_specs=[pl.BlockSpec((rows, d), lambda j: (0, 0)),
                  pl.BlockSpec((d, tn), lambda j: (0, j)),
                  pl.BlockSpec((1, tn), lambda j: (0, j))],
        out_specs=pl.BlockSpec((rows, tn), lambda j: (0, j)),
        out_shape=jax.ShapeDtypeStruct((rows, n), F32),
        compiler_params=_cparams(("arbitrary",)),
        name="adaln",
    )(cc, w, b.reshape(1, n))


def _modnorm(x, gain, shift, scale):
    y = x * lax.rsqrt(jnp.mean(x * x, axis=-1, keepdims=True) + EPS)
    return (y * gain) * (1.0 + scale) + shift


def _swap16(p):
    lane = lax.broadcasted_iota(I32, p.shape, 1)
    up = pltpu.roll(p, LANES - 16, 1)
    dn = pltpu.roll(p, 16, 1)
    return jnp.where((lane & 31) < 16, up, dn)


def _inproj_kernel(x_ref, sh_ref, sc_ref, g_ref, w_ref, wvt_ref, cos_ref, sin_ref, o_ref, vt_ref, *,
                   rope_chunks, chunk_scale):
    h = _modnorm(x_ref[0], g_ref[...], sh_ref[0], sc_ref[0]).astype(BF16)
    p = jnp.dot(h, w_ref[...], preferred_element_type=F32)
    n_chunks = p.shape[1] // LANES
    if rope_chunks:
        cs = cos_ref[...]
        sn = sin_ref[...]
    for j in range(n_chunks):
        pj = p[:, j * LANES:(j + 1) * LANES]
        if j in rope_chunks:
            pj = pj * cs + _swap16(pj) * sn
        if j in chunk_scale:
            pj = pj * chunk_scale[j]
        o_ref[0, :, j * LANES:(j + 1) * LANES] = pj.astype(BF16)
    vt_ref[0] = lax.dot_general(wvt_ref[...], h, (((1,), (1,)), ((), ())),
                                preferred_element_type=F32).astype(BF16)


def _inproj(x, shift, scale, gain, w, w_vt, cos_t, sin_t, rope, tm):
    b, s, d = x.shape
    n = w.shape[1]
    nv = w_vt.shape[0]
    assert n == QK_CHUNKS * LANES
    rope_chunks = frozenset(range(QK_CHUNKS)) if rope else frozenset()
    chunk_scale = {j: DH ** -0.5 * LOG2E for j in range(0, 8)}
    chunk_scale.update({j: DH ** -0.5 for j in range(16, 24)})
    kern = functools.partial(_inproj_kernel, rope_chunks=rope_chunks, chunk_scale=chunk_scale)
    return pl.pallas_call(
        kern,
        grid=(b, s // tm),
        in_specs=[pl.BlockSpec((1, tm, d), lambda i, j: (i, j, 0)),
                  pl.BlockSpec((1, 1, d), lambda i, j: (i, 0, 0)),
                  pl.BlockSpec((1, 1, d), lambda i, j: (i, 0, 0)),
                  pl.BlockSpec((1, d), lambda i, j: (0, 0)),
                  pl.BlockSpec((d, n), lambda i, j: (0, 0), pipeline_mode=pl.Buffered(1)),
                  pl.BlockSpec((nv, d), lambda i, j: (0, 0), pipeline_mode=pl.Buffered(1)),
                  pl.BlockSpec((tm, LANES), lambda i, j: (j, 0)),
                  pl.BlockSpec((tm, LANES), lambda i, j: (j, 0))],
        out_specs=[pl.BlockSpec((1, tm, n), lambda i, j: (i, j, 0)),
                   pl.BlockSpec((1, nv, tm), lambda i, j: (i, 0, j))],
        out_shape=[jax.ShapeDtypeStruct((b, s, n), BF16), jax.ShapeDtypeStruct((b, nv, s), BF16)],
        compiler_params=_cparams(("arbitrary", "arbitrary")),
        name="in_proj",
    )(x, shift, scale, gain, w, w_vt, cos_t, sin_t)


def _rope_tables(s):
    rows = s // GRID_W
    row = jnp.repeat(jnp.arange(rows, dtype=F32), GRID_W)
    col = jnp.tile(jnp.arange(GRID_W, dtype=F32), rows)
    freqs = ROPE_THETA ** (-jnp.arange(ROPE_PAIRS, dtype=F32) / ROPE_PAIRS)
    ar = row[:, None] * freqs
    ac = col[:, None] * freqs
    cos64 = jnp.concatenate([jnp.cos(ar), jnp.cos(ar), jnp.cos(ac), jnp.cos(ac)], axis=1)
    sin64 = jnp.concatenate([-jnp.sin(ar), jnp.sin(ar), -jnp.sin(ac), jnp.sin(ac)], axis=1)
    return jnp.tile(cos64, (1, 2)), jnp.tile(sin64, (1, 2))


def _head_rms(o, gain):
    return o * lax.rsqrt(jnp.mean(o * o, axis=-1, keepdims=True) + EPS) * gain


def _diff_attn_kernel(lq1_ref, lk1_ref, lq2_ref, lk2_ref, q_ref, k_ref, vt_ref, kc_ref, vtc_ref,
                      gain_ref, o_ref, s_ref, m_ref, *, tk, n_tiles):
    j = pl.program_id(2)
    tq = q_ref.shape[1]
    s_len = k_ref.shape[1]
    lc = kc_ref.shape[1]
    blocks = [(0, lc, lambda: kc_ref[0], lambda: vtc_ref[0])]
    blocks += [(lc + b * tk, tk, lambda b=b: k_ref[0, b * tk:(b + 1) * tk, :],
                lambda b=b: vt_ref[0, :, b * tk:(b + 1) * tk]) for b in range(s_len // tk)]
    m_init = jnp.full((1, tq), NEG_INF, F32)

    def query_maps():
        q = q_ref[0]
        lane = lax.broadcasted_iota(I32, q.shape, 1)
        zero = jnp.zeros_like(q)
        return jnp.where(lane < DH, q, zero), jnp.where(lane >= DH, q, zero)

    def score_block(slot, i, blk, qm, m):
        r0, rows, kblk, _ = blk
        s = lax.dot_general(kblk(), qm, (((1,), (1,)), ((), ())), preferred_element_type=F32)
        s_ref[slot, i, r0:r0 + rows, :] = s
        return jnp.maximum(m, jnp.max(s, axis=0, keepdims=True))

    def prob_block(slot, i, blk, m, l, acc):
        r0, rows, _, vtblk = blk
        p = jnp.exp2(s_ref[slot, i, r0:r0 + rows, :] - m)
        part = jnp.dot(vtblk(), p.astype(BF16), preferred_element_type=F32)
        psum = jnp.sum(p, axis=0, keepdims=True)
        return (psum if l is None else l + psum), (part if acc is None else acc + part)

    def finish(l, acc):
        lam = (jnp.exp(jnp.sum(lq1_ref[...] * lk1_ref[...], keepdims=True))
               - jnp.exp(jnp.sum(lq2_ref[...] * lk2_ref[...], keepdims=True)) + LAM_INIT)
        out = (acc[0] / l[0] - lam * (acc[1] / l[1])).T
        o_ref[0] = (_head_rms(out, gain_ref[...]) * (1.0 - LAM_INIT)).astype(BF16)

    def step(score_slot, prob_slot):
        if score_slot is not None:
            qm = query_maps()
            m_new = [m_init, m_init]
        if prob_slot is not None:
            m_old = [m_ref[prob_slot, 0], m_ref[prob_slot, 1]]
            l, acc = [None, None], [None, None]
        for blk in blocks:
            for i in range(2):
                if score_slot is not None:
                    m_new[i] = score_block(score_slot, i, blk, qm[i], m_new[i])
            for i in range(2):
                if prob_slot is not None:
                    l[i], acc[i] = prob_block(prob_slot, i, blk, m_old[i], l[i], acc[i])
        if score_slot is not None:
            m_ref[score_slot, 0] = m_new[0]
            m_ref[score_slot, 1] = m_new[1]
        if prob_slot is not None:
            finish(l, acc)

    @pl.when(j == 0)
    def _():
        step(0, None)

    for parity in range(2):
        @pl.when((j > 0) & (j < n_tiles) & ((j & 1) == parity))
        def _():
            step(parity, 1 - parity)

    @pl.when(j == n_tiles)
    def _():
        step(None, (n_tiles - 1) % 2)


def _diff_attn(p, vt, pc, vtc, lams, gain, tq, tk):
    b, s, _ = p.shape
    lc = pc.shape[1]
    n_tiles = s // tq
    lam_spec = pl.BlockSpec((1, DH), lambda i, h, j: (0, 0))
    return pl.pallas_call(
        functools.partial(_diff_attn_kernel, tk=tk, n_tiles=n_tiles),
        grid=(b, DIFF_HEADS, n_tiles + 1),
        in_specs=[lam_spec, lam_spec, lam_spec, lam_spec,
                  pl.BlockSpec((1, tq, LANES), lambda i, h, j: (i, jnp.minimum(j, n_tiles - 1), h)),
                  pl.BlockSpec((1, s, LANES), lambda i, h, j: (i, 0, 8 + h)),
                  pl.BlockSpec((1, LANES, s), lambda i, h, j: (i, h, 0)),
                  pl.BlockSpec((1, lc, LANES), lambda i, h, j: (i, 0, 8 + h)),
                  pl.BlockSpec((1, LANES, lc), lambda i, h, j: (i, h, 0)),
                  pl.BlockSpec((1, LANES), lambda i, h, j: (0, h))],
        out_specs=pl.BlockSpec((1, tq, LANES), lambda i, h, j: (i, jnp.maximum(j - 1, 0), h)),
        out_shape=jax.ShapeDtypeStruct((b, s, DIFF_HEADS * LANES), BF16),
        scratch_shapes=[pltpu.VMEM((2, 2, lc + s, tq), F32), pltpu.VMEM((2, 2, 1, tq), F32)],
        compiler_params=_cparams(("arbitrary", "arbitrary", "arbitrary")),
        name="diff_attn",
    )(*lams, p, p, vt, pc, vtc, gain)


def _swa_attn_kernel(sink_ref, q_ref, k_ref, vt_ref, kc_ref, vtc_ref, gain_ref, o_ref,
                     kcat_ref, vtcat_ref):
    n = pl.program_id(1)
    nb = pl.num_programs(1)
    s_len = k_ref.shape[1]
    lc = kc_ref.shape[1]
    n_band = 3 * BAND

    @pl.when(n == 0)
    def _():
        kcat_ref[n_band:, :] = kc_ref[0]
        vtcat_ref[:, n_band:] = vtc_ref[0]

    prev = pl.multiple_of(jnp.maximum(n - 1, 0) * BAND, BAND)
    cur = pl.multiple_of(n * BAND, BAND)
    nxt = pl.multiple_of(jnp.minimum(n + 1, nb - 1) * BAND, BAND)
    for t, off in enumerate((prev, cur, nxt)):
        kcat_ref[t * BAND:(t + 1) * BAND, :] = k_ref[0, pl.ds(off, BAND), :]
        vtcat_ref[:, t * BAND:(t + 1) * BAND] = vt_ref[0, :, pl.ds(off, BAND)]

    keys = n_band + lc
    cols = SWA_GROUP * BAND
    ki = lax.broadcasted_iota(I32, (keys, cols), 0)
    qi = lax.broadcasted_iota(I32, (keys, cols), 1) & (BAND - 1)
    kpos = (n - 1) * BAND + ki
    in_band = jnp.where(jnp.abs(qi + BAND - ki) <= BAND,
                        jnp.where(kpos >= 0, jnp.where(kpos < s_len, 1, 0), 0), 0)
    ok = jnp.where(ki >= n_band, 1, in_band) > 0

    q = q_ref[0]
    outs = []
    for g in range(SWA_KV):
        kg = kcat_ref[:, g * DH:(g + 1) * DH]
        vtg = vtcat_ref[g * DH:(g + 1) * DH, :]
        qg = jnp.concatenate(
            [q[:, (g * SWA_GROUP + j) * DH:(g * SWA_GROUP + j + 1) * DH] for j in range(SWA_GROUP)],
            axis=0)
        s = lax.dot_general(kg, qg, (((1,), (1,)), ((), ())), preferred_element_type=F32)
        s = jnp.where(ok, s, NEG_INF)
        sink = jnp.concatenate(
            [jnp.full((1, BAND), sink_ref[g * SWA_GROUP + j], F32) for j in range(SWA_GROUP)],
            axis=1)
        m = jnp.maximum(jnp.max(s, axis=0, keepdims=True), sink)
        e = jnp.exp(s - m)
        denom = jnp.sum(e, axis=0, keepdims=True) + jnp.exp(sink - m)
        o = jnp.dot(vtg, e.astype(BF16), preferred_element_type=F32) / denom
        outs.append(o * lax.rsqrt(jnp.mean(o * o, axis=0, keepdims=True) + EPS))
    o_all = jnp.concatenate(outs, axis=0).T
    for g in range(SWA_KV):
        for j in range(SWA_GROUP):
            c0 = (g * SWA_GROUP + j) * DH
            o_ref[0, :, c0:c0 + DH] = (o_all[j * BAND:(j + 1) * BAND, g * DH:(g + 1) * DH]
                                       * gain_ref[:, c0:c0 + DH]).astype(BF16)


def _swa_attn(p, vt, pc, vtc, sink, gain):
    b, s, _ = p.shape
    lc = pc.shape[1]
    kvw = SWA_KV * DH
    vblk = vt.shape[1] // kvw - 1
    return pl.pallas_call(
        _swa_attn_kernel,
        grid=(b, s // BAND),
        in_specs=[pl.BlockSpec(memory_space=pltpu.SMEM),
                  pl.BlockSpec((1, BAND, SWA_HEADS * DH), lambda i, j: (i, j, 2)),
                  pl.BlockSpec((1, s, kvw), lambda i, j: (i, 0, 12)),
                  pl.BlockSpec((1, kvw, s), lambda i, j: (i, vblk, 0)),
                  pl.BlockSpec((1, lc, kvw), lambda i, j: (i, 0, 12)),
                  pl.BlockSpec((1, kvw, lc), lambda i, j: (i, vblk, 0)),
                  pl.BlockSpec((1, SWA_HEADS * DH), lambda i, j: (0, 0))],
        out_specs=pl.BlockSpec((1, BAND, SWA_HEADS * DH), lambda i, j: (i, j, 0)),
        out_shape=jax.ShapeDtypeStruct((b, s, SWA_HEADS * DH), BF16),
        scratch_shapes=[pltpu.VMEM((3 * BAND + lc, kvw), BF16),
                        pltpu.VMEM((kvw, 3 * BAND + lc), BF16)],
        compiler_params=_cparams(("arbitrary", "arbitrary")),
        name="swa_attn",
    )(sink, p, p, vt, pc, vtc, gain)


def _outproj_kernel(ad_ref, as_ref, wd_ref, ws_ref, x_ref, g1_ref, sh_ref, sc_ref, gn_ref,
                    x1_ref, h2_ref):
    a = (jnp.dot(ad_ref[0], wd_ref[...], preferred_element_type=F32)
         + jnp.dot(as_ref[0], ws_ref[...], preferred_element_type=F32))
    x1 = x_ref[0] + g1_ref[0] * a
    x1_ref[0] = x1
    h2_ref[0] = _modnorm(x1, gn_ref[...], sh_ref[0], sc_ref[0]).astype(BF16)


def _outproj(a_diff, a_swa, w_d, w_s, x, g1, sh2, sc2, gain, tm):
    b, s, d = x.shape
    wd = a_diff.shape[2]
    vec = pl.BlockSpec((1, 1, d), lambda i, j: (i, 0, 0))
    return pl.pallas_call(
        _outproj_kernel,
        grid=(b, s // tm),
        in_specs=[pl.BlockSpec((1, tm, wd), lambda i, j: (i, j, 0)),
                  pl.BlockSpec((1, tm, wd), lambda i, j: (i, j, 0)),
                  pl.BlockSpec((wd, d), lambda i, j: (0, 0)),
                  pl.BlockSpec((wd, d), lambda i, j: (0, 0)),
                  pl.BlockSpec((1, tm, d), lambda i, j: (i, j, 0)),
                  vec, vec, vec,
                  pl.BlockSpec((1, d), lambda i, j: (0, 0))],
        out_specs=[pl.BlockSpec((1, tm, d), lambda i, j: (i, j, 0)),
                   pl.BlockSpec((1, tm, d), lambda i, j: (i, j, 0))],
        out_shape=[jax.ShapeDtypeStruct((b, s, d), F32), jax.ShapeDtypeStruct((b, s, d), BF16)],
        compiler_params=_cparams(("arbitrary", "arbitrary")),
        name="out_proj",
    )(a_diff, a_swa, w_d, w_s, x, g1, sh2, sc2, gain)


def _extract_topk(s, rank, payload, k):
    vals, pays = [], []
    for _ in range(k):
        m = jnp.max(s, axis=0, keepdims=True)
        first = jnp.min(jnp.where(s == m, rank, jnp.inf), axis=0, keepdims=True)
        hit = rank == first
        vals.append(m)
        pays.append(first if payload is None
                    else jnp.sum(jnp.where(hit, payload, 0.0), axis=0, keepdims=True))
        s = jnp.where(hit, -jnp.inf, s)
    return jnp.concatenate(vals, axis=0), jnp.concatenate(pays, axis=0)


def _extract_topk_paired(s, k):
    half = s.shape[0] // 2
    a, b = s[:half], s[half:]
    ra = lax.broadcasted_iota(I32, a.shape, 0).astype(F32)
    rb = ra + float(half)
    b_wins = b > a
    w, l = jnp.where(b_wins, b, a), jnp.where(b_wins, a, b)
    rw, rl = jnp.where(b_wins, rb, ra), jnp.where(b_wins, ra, rb)
    vals, ids = [], []
    for _ in range(k):
        m = jnp.max(w, axis=0, keepdims=True)
        first = jnp.min(jnp.where(w == m, rw, jnp.inf), axis=0, keepdims=True)
        hit = rw == first
        vals.append(m)
        ids.append(first)
        w = jnp.where(hit, l, w)
        rw = jnp.where(hit, rl, rw)
        l = jnp.where(hit, -jnp.inf, l)
    return jnp.concatenate(vals, axis=0), jnp.concatenate(ids, axis=0)


def _pair_candidates(s1, i1, s2, i2):
    k, tt = s1.shape
    assert k == 16, "block layout below is written for 16 x 16 pairs"
    sub = lax.broadcasted_iota(I32, (8, tt), 0)
    vals = [s1[0:1] + s2, s1[1:2] + s2[0:8]]
    pos = [lax.broadcasted_iota(I32, (k, tt), 0), k + sub]
    ids = [i1[0:1] * N_KEYS + i2, i1[1:2] * N_KEYS + i2[0:8]]
    for a in range(2, 8):
        keep = sub < k // (a + 1)
        vals.append(jnp.where(keep, s1[a:a + 1] + s2[0:8], -jnp.inf))
        pos.append(a * k + sub)
        ids.append(i1[a:a + 1] * N_KEYS + i2[0:8])
    vals.append(s1[8:16] + s2[0:1])
    pos.append((sub + 8) * k)
    ids.append(i1[8:16] * N_KEYS + i2[0:1])
    return (jnp.concatenate(vals, axis=0), jnp.concatenate(pos, axis=0).astype(F32),
            jnp.concatenate(ids, axis=0))


def _peer_route_kernel(h_ref, wq_ref, keys_ref, idx_ref, gate_ref):
    q = jnp.dot(h_ref[...], wq_ref[...], preferred_element_type=F32).astype(BF16)
    for h in range(PEER_HEADS):
        halves = []
        for i in range(2):
            c0 = (h * 2 + i) * N_KEYS
            s = lax.dot_general(keys_ref[h, i], q[:, c0:c0 + N_KEYS], (((1,), (1,)), ((), ())),
                                preferred_element_type=F32)
            halves.append(_extract_topk_paired(s, PEER_TOPK))
        (s1, i1), (s2, i2) = halves
        cand, pos, cidx = _pair_candidates(s1, i1, s2, i2)
        top_s, top_i = _extract_topk(cand, pos, cidx, PEER_TOPK)
        e = jnp.exp(top_s - jnp.max(top_s, axis=0, keepdims=True))
        r0 = h * PEER_TOPK
        idx_ref[r0:r0 + PEER_TOPK, :] = top_i.astype(I32)
        gate_ref[r0:r0 + PEER_TOPK, :] = e / jnp.sum(e, axis=0, keepdims=True)


def _peer_route(h2, wq, keys, tt):
    t, d = h2.shape
    nq = wq.shape[1]
    return pl.pallas_call(
        _peer_route_kernel,
        grid=(t // tt,),
        in_specs=[pl.BlockSpec((tt, d), lambda i: (i, 0)),
                  pl.BlockSpec((d, nq), lambda i: (0, 0)),
                  pl.BlockSpec(keys.shape, lambda i: (0, 0, 0, 0))],
        out_specs=[pl.BlockSpec((PEER_SEL, tt), lambda i: (0, i)),
                   pl.BlockSpec((PEER_SEL, tt), lambda i: (0, i))],
        out_shape=[jax.ShapeDtypeStruct((PEER_SEL, t), I32), jax.ShapeDtypeStruct((PEER_SEL, t), F32)],
        compiler_params=_cparams(("arbitrary",)),
        name="peer_route",
    )(h2, wq, keys)


def _split_bf16(x):
    hi = x.astype(BF16)
    return hi, (x - hi.astype(F32)).astype(BF16)


def _peer_expert_kernel(idx_cur_ref, idx_nxt_ref, uv_ref, h_ref, gate_ref, seg_ref, grpt_ref,
                        o_ref, buf_ref, sem_ref, *, tb, tg):
    i = pl.program_id(0)
    nsteps = pl.num_programs(0)

    def row_copy(idx_ref, slot, t, j):
        return pltpu.make_async_copy(uv_ref.at[idx_ref[t * PEER_SEL + j]],
                                     buf_ref.at[slot, t, j], sem_ref.at[slot, t])

    def wait_tokens(slot, toks):
        for t in toks:
            for j in range(PEER_SEL):
                row_copy(idx_cur_ref, slot, t, j).wait()

    @pl.when(i == 0)
    def _():
        for t in range(tb):
            for j in range(PEER_SEL):
                row_copy(idx_cur_ref, 0, t, j).start(priority=j % 2)

    diag = (lax.broadcasted_iota(I32, (SLAB, PEER_SEL * SLAB), 1) & (SLAB - 1)) == \
        lax.broadcasted_iota(I32, (SLAB, PEER_SEL * SLAB), 0)

    def consume(slot):
        copies = [(t, j) for t in range(tb) for j in range(PEER_SEL)]
        issue_points = (tb + GATE_LAG) * (PEER_SEL // SLAB) * 2
        progress = {"copies": 0, "points": 0}

        def start_copies(upto):
            while progress["copies"] < min(upto, len(copies)):
                t, j = copies[progress["copies"]]
                row_copy(idx_nxt_ref, 1 - slot, t, j).start(priority=j % 2)
                progress["copies"] += 1

        def issue_point():
            progress["points"] += 1
            rest = len(copies) - PRE_ISSUE * PEER_SEL
            start_copies(PRE_ISSUE * PEER_SEL + -(-rest * progress["points"] // issue_points))

        start_copies(PRE_ISSUE * PEER_SEL)
        wait_tokens(slot, range(tb // 2))
        g_hi, g_lo = _split_bf16(gate_ref[...])
        gate_rep = (jnp.dot(g_hi, grpt_ref[...], preferred_element_type=F32)
                    + jnp.dot(g_lo, grpt_ref[...], preferred_element_type=F32))

        def token_pair(t_u, t_v, w_row):
            z_parts, r = [], None
            if t_v is not None:
                wt = jnp.where(diag, jnp.broadcast_to(w_row, diag.shape), 0.0)
                wt_hi, wt_lo = _split_bf16(wt)
                wt2 = jnp.concatenate([wt_hi, wt_lo], axis=0)
            for n in range(PEER_SEL // SLAB):
                e0, c0 = n * SLAB, n * SLAB * SLAB
                issue_point()
                if t_u is not None:
                    u_tile = buf_ref[slot, t_u, e0:e0 + SLAB, 0:SLAB, :].reshape(SLAB * SLAB, LANES)
                    y = lax.dot_general(h_ref[t_u], u_tile, (((1,), (1,)), ((), ())),
                                        preferred_element_type=F32)
                    z_parts.append(jnp.sum(jnp.where(diag[:, :SLAB * SLAB], y, 0.0),
                                           axis=0, keepdims=True))
                issue_point()
                if t_v is not None:
                    v_tile = buf_ref[slot, t_v, e0:e0 + SLAB, SLAB:2 * SLAB, :].reshape(
                        SLAB * SLAB, LANES)
                    part = jnp.dot(wt2[:, c0:c0 + SLAB * SLAB], v_tile, preferred_element_type=F32)
                    r = part if r is None else r + part
            z = jnp.concatenate(z_parts, axis=1) if z_parts else None
            out = None if r is None else r[:SLAB] + r[SLAB:]
            return z, out

        def gate_rows(g, z):
            tile = SLAB * SLAB
            z8 = jnp.concatenate([z, jnp.zeros((8 - tg, z.shape[1]), F32)], axis=0) if tg < 8 else z
            stacked = jnp.concatenate([z8[:, n * tile:(n + 1) * tile]
                                       for n in range(z.shape[1] // tile)], axis=0)
            s_hi, s_lo = _split_bf16(stacked)
            seg = jnp.dot(jnp.concatenate([s_hi, s_lo], axis=0), seg_ref[...],
                          preferred_element_type=F32)
            seg = seg[:stacked.shape[0]] + seg[stacked.shape[0]:]
            act = jnp.concatenate([seg[8 * n:8 * n + tg] for n in range(z.shape[1] // tile)], axis=1)
            gelu = 0.5 * act * (1.0 + jnp.tanh(math.sqrt(2.0 / math.pi)
                                               * (act + 0.044715 * act * act * act)))
            return gate_rep[g * tg:(g + 1) * tg, :] * gelu

        assert GATE_LAG >= tg + 2
        w_reps, outs, zs = {}, {}, []
        for n in range(tb + GATE_LAG):
            t_u = n if n < tb else None
            t_v = n - GATE_LAG if n >= GATE_LAG else None
            w_row = None if t_v is None else w_reps[t_v // tg][t_v % tg:t_v % tg + 1]
            if n == tb // 2:
                wait_tokens(slot, range(tb // 2, tb))
            z, out = token_pair(t_u, t_v, w_row)
            if t_u is not None:
                zs.append(z)
            if t_v is not None:
                outs[t_v] = out
            if n % tg == 0 and tg <= n <= tb:
                g = n // tg - 1
                w_reps[g] = gate_rows(g, jnp.concatenate(zs[g * tg:(g + 1) * tg], axis=0))
        for t in range(tb):
            for r in range(SLAB):
                o_ref[t:t + 1, r * LANES:(r + 1) * LANES] = outs[t][r:r + 1, :]

        @pl.when(i == nsteps - 1)
        def _():
            wait_tokens(1 - slot, range(tb))

    @pl.when((i & 1) == 0)
    def _():
        consume(0)

    @pl.when((i & 1) == 1)
    def _():
        consume(1)


def _peer_expert(idx_flat, uv, h_slab, gate, tb):
    t = h_slab.shape[0]
    nsteps = t // tb
    tile = SLAB * SLAB
    seg = (lax.broadcasted_iota(I32, (tile, tile), 0) // SLAB
           == lax.broadcasted_iota(I32, (tile, tile), 1) // SLAB).astype(BF16)
    grpt = (lax.broadcasted_iota(I32, (PEER_SEL, PEER_SEL * SLAB), 0)
            == lax.broadcasted_iota(I32, (PEER_SEL, PEER_SEL * SLAB), 1) // SLAB).astype(BF16)
    blk = tb * PEER_SEL
    return pl.pallas_call(
        functools.partial(_peer_expert_kernel, tb=tb, tg=min(4, tb)),
        grid=(nsteps,),
        in_specs=[pl.BlockSpec((blk,), lambda i: (i,), memory_space=pltpu.SMEM),
                  pl.BlockSpec((blk,), lambda i: (jnp.minimum(i + 1, nsteps - 1),),
                               memory_space=pltpu.SMEM),
                  pl.BlockSpec(memory_space=pl.ANY),
                  pl.BlockSpec((tb, SLAB, LANES), lambda i: (i, 0, 0)),
                  pl.BlockSpec((tb, PEER_SEL), lambda i: (i, 0)),
                  pl.BlockSpec(seg.shape, lambda i: (0, 0)),
                  pl.BlockSpec(grpt.shape, lambda i: (0, 0))],
        out_specs=pl.BlockSpec((tb, SLAB * LANES), lambda i: (i, 0)),
        out_shape=jax.ShapeDtypeStruct((t, SLAB * LANES), F32),
        scratch_shapes=[pltpu.VMEM((2, tb, PEER_SEL, 2 * SLAB, LANES), BF16),
                        pltpu.SemaphoreType.DMA((2, tb))],
        compiler_params=_cparams(("arbitrary",)),
        name="peer_expert",
    )(idx_flat, idx_flat, uv, h_slab, gate, seg, grpt)


def _final_kernel(x1_ref, pe_ref, g2_ref, gn_ref, o_ref):
    x2 = x1_ref[0] + g2_ref[0] * pe_ref[0]
    o_ref[0] = x2 * lax.rsqrt(jnp.mean(x2 * x2, axis=-1, keepdims=True) + EPS) * gn_ref[...]


def _final(x1, pe, g2, gain, tm):
    b, s, d = x1.shape
    tok = pl.BlockSpec((1, tm, d), lambda i, j: (i, j, 0))
    return pl.pallas_call(
        _final_kernel,
        grid=(b, s // tm),
        in_specs=[tok, tok, pl.BlockSpec((1, 1, d), lambda i, j: (i, 0, 0)),
                  pl.BlockSpec((1, d), lambda i, j: (0, 0))],
        out_specs=tok,
        out_shape=jax.ShapeDtypeStruct((b, s, d), F32),
        compiler_params=_cparams(("arbitrary", "arbitrary")),
        name="final_norm",
    )(x1, pe, g2, gain)


def kernel(x, c, ctx, c_ctx, w_ada, b_ada, norm_attn, w_in, diff_lambda_q1, diff_lambda_k1,
           diff_lambda_q2, diff_lambda_k2, diff_norm, swa_sink, swa_norm, w_out, norm_ffn,
           peer_w_q, peer_sub_keys, peer_u, peer_v, final_norm):
    b, s, d = x.shape
    assert w_ada.shape[0] == 1, "single layer only"
    t = b * s

    cc = jnp.zeros((8, d), F32).at[:b].set(c).at[b].set(c_ctx)
    mod = _adaln(cc, w_ada[0], b_ada[0])
    sh1, sc1, g1, sh2, sc2, g2 = [m[:, None, :] for m in jnp.split(mod, 6, axis=-1)]

    w_in_b = w_in[0].astype(BF16)
    gain_attn = norm_attn[0].reshape(1, d)
    cos_t, sin_t = _rope_tables(s)
    tm = min(512, s)
    dvw = DIFF_HEADS * LANES
    kvw = SWA_KV * DH
    w_qk = jnp.concatenate([w_in_b[:, :2 * dvw], w_in_b[:, 3 * dvw:-kvw]], axis=1)
    w_vt = jnp.concatenate([w_in_b[:, 2 * dvw:3 * dvw], w_in_b[:, -kvw:]], axis=1).T
    p, vt = _inproj(x, sh1[:b], sc1[:b], gain_attn, w_qk, w_vt, cos_t, sin_t, True, tm)
    lc = ctx.shape[1]
    ones = jnp.ones((lc, LANES), F32)
    ctx_sh = jnp.broadcast_to(sh1[b:b + 1], (b, 1, d))
    ctx_sc = jnp.broadcast_to(sc1[b:b + 1], (b, 1, d))
    pc, vtc = _inproj(ctx, ctx_sh, ctx_sc, gain_attn, w_qk, w_vt, ones, ones, False, lc)

    lams = [v[0].reshape(1, DH).astype(F32) for v in
            (diff_lambda_q1, diff_lambda_k1, diff_lambda_q2, diff_lambda_k2)]
    a_diff = _diff_attn(p, vt, pc, vtc, lams, diff_norm[0].reshape(1, -1), min(256, s), min(512, s))
    a_swa = _swa_attn(p, vt, pc, vtc, swa_sink[0].astype(F32), swa_norm[0].reshape(1, -1))

    w_out_b = w_out[0].astype(BF16)
    dw = a_diff.shape[2]
    x1, h2 = _outproj(a_diff, a_swa, w_out_b[:dw], w_out_b[dw:], x, g1[:b], sh2[:b], sc2[:b],
                      norm_ffn[0].reshape(1, d), tm)

    idx_t, gate_t = _peer_route(h2.reshape(t, d), peer_w_q[0].astype(BF16),
                                peer_sub_keys[0].astype(BF16), min(256, t))
    uv = jnp.concatenate([peer_u[0].reshape(-1, SLAB, LANES),
                          peer_v[0].reshape(-1, SLAB, LANES)], axis=1).astype(BF16)
    pe = _peer_expert(idx_t.T.reshape(-1), uv, h2.reshape(t, SLAB, LANES), gate_t.T, 16)

    return _final(x1, pe.reshape(b, s, d), g2[:b], final_norm.reshape(1, d), tm)
```

```python
import functools
import math

import jax
import jax.numpy as jnp
from jax import lax
from jax.experimental import pallas as pl
from jax.experimental.pallas import tpu as pltpu

F32 = jnp.float32
BF16 = jnp.bfloat16
I32 = jnp.int32

EPS = 1e-6
NEG_INF = -1e30
ROPE_THETA = 10000.0
GRID_W = 64
ROPE_PAIRS = 16

LANES = 128
DH = 64
DIFF_HEADS = 8
SWA_HEADS = 16
SWA_KV = 4
SWA_GROUP = SWA_HEADS // SWA_KV
BAND = 128
QK_CHUNKS = 26
LAM_INIT = 0.8 - 0.6 * math.exp(-0.3 * 0)
LOG2E = math.log2(math.e)

PEER_HEADS = 8
PEER_TOPK = 16
N_KEYS = 128
PEER_SEL = PEER_HEADS * PEER_TOPK
SLAB = 16
GATE_LAG = 6

VMEM_LIMIT = 56 * 1024 * 1024


def _cparams(sem):
    return pltpu.CompilerParams(dimension_semantics=sem, vmem_limit_bytes=VMEM_LIMIT)


def _adaln_kernel(c_ref, w_ref, b_ref, o_ref):
    c = c_ref[...]
    s = c * (1.0 / (1.0 + jnp.exp(-c)))
    rows = s.shape[0]
    s_hi, s_lo = _split_bf16(s)
    w_hi, w_lo = _split_bf16(w_ref[...])
    both = jnp.dot(jnp.concatenate([s_hi, s_lo], axis=0), w_hi, preferred_element_type=F32)
    o_ref[...] = (both[:rows] + both[rows:] + jnp.dot(s_hi, w_lo, preferred_element_type=F32)
                  + b_ref[...])


def _adaln(cc, w, b):
    rows, d = cc.shape
    n = w.shape[1]
    tn = 1024
    return pl.pallas_call(
        _adaln_kernel,
        grid=(n // tn,),
        in_specs=[pl.BlockSpec((rows, d), lambda j: (0, 0)),
                  pl.BlockSpec((d, tn), lambda j: (0, j)),
                  pl.BlockSpec((1, tn), lambda j: (0, j))],
        out_specs=pl.BlockSpec((rows, tn), lambda j: (0, j)),
        out_shape=jax.ShapeDtypeStruct((rows, n), F32),
        compiler_params=_cparams(("arbitrary",)),
        name="adaln",
    )(cc, w, b.reshape(1, n))


def _modnorm(x, gain, shift, scale):
    y = x * lax.rsqrt(jnp.mean(x * x, axis=-1, keepdims=True) + EPS)
    return (y * gain) * (1.0 + scale) + shift


def _swap16(p):
    lane = lax.broadcasted_iota(I32, p.shape, 1)
    up = pltpu.roll(p, LANES - 16, 1)
    dn = pltpu.roll(p, 16, 1)
    return jnp.where((lane & 31) < 16, up, dn)


def _inproj_kernel(x_ref, sh_ref, sc_ref, g_ref, w_ref, wvt_ref, cos_ref, sin_ref, o_ref, vt_ref, *,
                   rope_chunks, chunk_scale):
    h = _modnorm(x_ref[0], g_ref[...], sh_ref[0], sc_ref[0]).astype(BF16)
    p = jnp.dot(h, w_ref[...], preferred_element_type=F32)
    n_chunks = p.shape[1] // LANES
    if rope_chunks:
        cs = cos_ref[...]
        sn = sin_ref[...]
    for j in range(n_chunks):
        pj = p[:, j * LANES:(j + 1) * LANES]
        if j in rope_chunks:
            pj = pj * cs + _swap16(pj) * sn
        if j in chunk_scale:
            pj = pj * chunk_scale[j]
        o_ref[0, :, j * LANES:(j + 1) * LANES] = pj.astype(BF16)
    vt_ref[0] = lax.dot_general(wvt_ref[...], h, (((1,), (1,)), ((), ())),
                                preferred_element_type=F32).astype(BF16)


def _inproj(x, shift, scale, gain, w, w_vt, cos_t, sin_t, rope, tm):
    b, s, d = x.shape
    n = w.shape[1]
    nv = w_vt.shape[0]
    assert n == QK_CHUNKS * LANES
    rope_chunks = frozenset(range(QK_CHUNKS)) if rope else frozenset()
    chunk_scale = {j: DH ** -0.5 * LOG2E for j in range(0, 8)}
    chunk_scale.update({j: DH ** -0.5 for j in range(16, 24)})
    kern = functools.partial(_inproj_kernel, rope_chunks=rope_chunks, chunk_scale=chunk_scale)
    return pl.pallas_call(
        kern,
        grid=(b, s // tm),
        in_specs=[pl.BlockSpec((1, tm, d), lambda i, j: (i, j, 0)),
                  pl.BlockSpec((1, 1, d), lambda i, j: (i, 0, 0)),
                  pl.BlockSpec((1, 1, d), lambda i, j: (i, 0, 0)),
                  pl.BlockSpec((1, d), lambda i, j: (0, 0)),
                  pl.BlockSpec((d, n), lambda i, j: (0, 0), pipeline_mode=pl.Buffered(1)),
                  pl.BlockSpec((nv, d), lambda i, j: (0, 0), pipeline_mode=pl.Buffered(1)),
                  pl.BlockSpec((tm, LANES), lambda i, j: (j, 0)),
                  pl.BlockSpec((tm, LANES), lambda i, j: (j, 0))],
        out_specs=[pl.BlockSpec((1, tm, n), lambda i, j: (i, j, 0)),
                   pl.BlockSpec((1, nv, tm), lambda i, j: (i, 0, j))],
        out_shape=[jax.ShapeDtypeStruct((b, s, n), BF16), jax.ShapeDtypeStruct((b, nv, s), BF16)],
        compiler_params=_cparams(("arbitrary", "arbitrary")),
        name="in_proj",
    )(x, shift, scale, gain, w, w_vt, cos_t, sin_t)


def _rope_tables(s):
    rows = s // GRID_W
    row = jnp.repeat(jnp.arange(rows, dtype=F32), GRID_W)
    col = jnp.tile(jnp.arange(GRID_W, dtype=F32), rows)
    freqs = ROPE_THETA ** (-jnp.arange(ROPE_PAIRS, dtype=F32) / ROPE_PAIRS)
    ar = row[:, None] * freqs
    ac = col[:, None] * freqs
    cos64 = jnp.concatenate([jnp.cos(ar), jnp.cos(ar), jnp.cos(ac), jnp.cos(ac)], axis=1)
    sin64 = jnp.concatenate([-jnp.sin(ar), jnp.sin(ar), -jnp.sin(ac), jnp.sin(ac)], axis=1)
    return jnp.tile(cos64, (1, 2)), jnp.tile(sin64, (1, 2))


def _head_rms(o, gain):
    return o * lax.rsqrt(jnp.mean(o * o, axis=-1, keepdims=True) + EPS) * gain


def _diff_attn_kernel(lq1_ref, lk1_ref, lq2_ref, lk2_ref, q_ref, k_ref, vt_ref, kc_ref, vtc_ref,
                      gain_ref, o_ref, s_ref, m_ref, *, tk, n_tiles):
    j = pl.program_id(2)
    tq = q_ref.shape[1]
    s_len = k_ref.shape[1]
    lc = kc_ref.shape[1]
    blocks = [(0, lc, lambda: kc_ref[0], lambda: vtc_ref[0])]
    blocks += [(lc + b * tk, tk, lambda b=b: k_ref[0, b * tk:(b + 1) * tk, :],
                lambda b=b: vt_ref[0, :, b * tk:(b + 1) * tk]) for b in range(s_len // tk)]
    m_init = jnp.full((1, tq), NEG_INF, F32)

    def query_maps():
        q = q_ref[0]
        lane = lax.broadcasted_iota(I32, q.shape, 1)
        zero = jnp.zeros_like(q)
        return jnp.where(lane < DH, q, zero), jnp.where(lane >= DH, q, zero)

    def score_block(slot, i, blk, qm, m):
        r0, rows, kblk, _ = blk
        s = lax.dot_general(kblk(), qm, (((1,), (1,)), ((), ())), preferred_element_type=F32)
        s_ref[slot, i, r0:r0 + rows, :] = s
        return jnp.maximum(m, jnp.max(s, axis=0, keepdims=True))

    def prob_block(slot, i, blk, m, l, acc):
        r0, rows, _, vtblk = blk
        p = jnp.exp2(s_ref[slot, i, r0:r0 + rows, :] - m)
        part = jnp.dot(vtblk(), p.astype(BF16), preferred_element_type=F32)
        psum = jnp.sum(p, axis=0, keepdims=True)
        return (psum if l is None else l + psum), (part if acc is None else acc + part)

    def finish(l, acc):
        lam = (jnp.exp(jnp.sum(lq1_ref[...] * lk1_ref[...], keepdims=True))
               - jnp.exp(jnp.sum(lq2_ref[...] * lk2_ref[...], keepdims=True)) + LAM_INIT)
        out = (acc[0] / l[0] - lam * (acc[1] / l[1])).T
        o_ref[0] = (_head_rms(out, gain_ref[...]) * (1.0 - LAM_INIT)).astype(BF16)

    def step(score_slot, prob_slot):
        if score_slot is not None:
            qm = query_maps()
            m_new = [m_init, m_init]
        if prob_slot is not None:
            m_old = [m_ref[prob_slot, 0], m_ref[prob_slot, 1]]
            l, acc = [None, None], [None, None]
        for blk in blocks:
            for i in range(2):
                if score_slot is not None:
                    m_new[i] = score_block(score_slot, i, blk, qm[i], m_new[i])
            for i in range(2):
                if prob_slot is not None:
                    l[i], acc[i] = prob_block(prob_slot, i, blk, m_old[i], l[i], acc[i])
        if score_slot is not None:
            m_ref[score_slot, 0] = m_new[0]
            m_ref[score_slot, 1] = m_new[1]
        if prob_slot is not None:
            finish(l, acc)

    @pl.when(j == 0)
    def _():
        step(0, None)

    for parity in range(2):
        @pl.when((j > 0) & (j < n_tiles) & ((j & 1) == parity))
        def _():
            step(parity, 1 - parity)

    @pl.when(j == n_tiles)
    def _():
        step(None, (n_tiles - 1) % 2)


def _diff_attn(p, vt, pc, vtc, lams, gain, tq, tk):
    b, s, _ = p.shape
    lc = pc.shape[1]
    n_tiles = s // tq
    lam_spec = pl.BlockSpec((1, DH), lambda i, h, j: (0, 0))
    return pl.pallas_call(
        functools.partial(_diff_attn_kernel, tk=tk, n_tiles=n_tiles),
        grid=(b, DIFF_HEADS, n_tiles + 1),
        in_specs=[lam_spec, lam_spec, lam_spec, lam_spec,
                  pl.BlockSpec((1, tq, LANES), lambda i, h, j: (i, jnp.minimum(j, n_tiles - 1), h)),
                  pl.BlockSpec((1, s, LANES), lambda i, h, j: (i, 0, 8 + h)),
                  pl.BlockSpec((1, LANES, s), lambda i, h, j: (i, h, 0)),
                  pl.BlockSpec((1, lc, LANES), lambda i, h, j: (i, 0, 8 + h)),
                  pl.BlockSpec((1, LANES, lc), lambda i, h, j: (i, h, 0)),
                  pl.BlockSpec((1, LANES), lambda i, h, j: (0, h))],
        out_specs=pl.BlockSpec((1, tq, LANES), lambda i, h, j: (i, jnp.maximum(j - 1, 0), h)),
        out_shape=jax.ShapeDtypeStruct((b, s, DIFF_HEADS * LANES), BF16),
        scratch_shapes=[pltpu.VMEM((2, 2, lc + s, tq), F32), pltpu.VMEM((2, 2, 1, tq), F32)],
        compiler_params=_cparams(("arbitrary", "arbitrary", "arbitrary")),
        name="diff_attn",
    )(*lams, p, p, vt, pc, vtc, gain)


def _swa_attn_kernel(sink_ref, q_ref, k_ref, vt_ref, kc_ref, vtc_ref, gain_ref, o_ref,
                     kcat_ref, vtcat_ref):
    n = pl.program_id(1)
    nb = pl.num_programs(1)
    s_len = k_ref.shape[1]
    lc = kc_ref.shape[1]
    n_band = 3 * BAND

    @pl.when(n == 0)
    def _():
        kcat_ref[n_band:, :] = kc_ref[0]
        vtcat_ref[:, n_band:] = vtc_ref[0]

    prev = pl.multiple_of(jnp.maximum(n - 1, 0) * BAND, BAND)
    cur = pl.multiple_of(n * BAND, BAND)
    nxt = pl.multiple_of(jnp.minimum(n + 1, nb - 1) * BAND, BAND)
    for t, off in enumerate((prev, cur, nxt)):
        kcat_ref[t * BAND:(t + 1) * BAND, :] = k_ref[0, pl.ds(off, BAND), :]
        vtcat_ref[:, t * BAND:(t + 1) * BAND] = vt_ref[0, :, pl.ds(off, BAND)]

    keys = n_band + lc
    cols = SWA_GROUP * BAND
    ki = lax.broadcasted_iota(I32, (keys, cols), 0)
    qi = lax.broadcasted_iota(I32, (keys, cols), 1) & (BAND - 1)
    kpos = (n - 1) * BAND + ki
    in_band = jnp.where(jnp.abs(qi + BAND - ki) <= BAND,
                        jnp.where(kpos >= 0, jnp.where(kpos < s_len, 1, 0), 0), 0)
    ok = jnp.where(ki >= n_band, 1, in_band) > 0

    q = q_ref[0]
    outs = []
    for g in range(SWA_KV):
        kg = kcat_ref[:, g * DH:(g + 1) * DH]
        vtg = vtcat_ref[g * DH:(g + 1) * DH, :]
        qg = jnp.concatenate(
            [q[:, (g * SWA_GROUP + j) * DH:(g * SWA_GROUP + j + 1) * DH] for j in range(SWA_GROUP)],
            axis=0)
        s = lax.dot_general(kg, qg, (((1,), (1,)), ((), ())), preferred_element_type=F32)
        s = jnp.where(ok, s, NEG_INF)
        sink = jnp.concatenate(
            [jnp.full((1, BAND), sink_ref[g * SWA_GROUP + j], F32) for j in range(SWA_GROUP)],
            axis=1)
        m = jnp.maximum(jnp.max(s, axis=0, keepdims=True), sink)
        e = jnp.exp(s - m)
        denom = jnp.sum(e, axis=0, keepdims=True) + jnp.exp(sink - m)
        o = jnp.dot(vtg, e.astype(BF16), preferred_element_type=F32) / denom
        outs.append(o * lax.rsqrt(jnp.mean(o * o, axis=0, keepdims=True) + EPS))
    o_all = jnp.concatenate(outs, axis=0).T
    for g in range(SWA_KV):
        for j in range(SWA_GROUP):
            c0 = (g * SWA_GROUP + j) * DH
            o_ref[0, :, c0:c0 + DH] = (o_all[j * BAND:(j + 1) * BAND, g * DH:(g + 1) * DH]
                                       * gain_ref[:, c0:c0 + DH]).astype(BF16)


def _swa_attn(p, vt, pc, vtc, sink, gain):
    b, s, _ = p.shape
    lc = pc.shape[1]
    kvw = SWA_KV * DH
    vblk = vt.shape[1] // kvw - 1
    return pl.pallas_call(
        _swa_attn_kernel,
        grid=(b, s // BAND),
        in_specs=[pl.BlockSpec(memory_space=pltpu.SMEM),
                  pl.BlockSpec((1, BAND, SWA_HEADS * DH), lambda i, j: (i, j, 2)),
                  pl.BlockSpec((1, s, kvw), lambda i, j: (i, 0, 12)),
                  pl.BlockSpec((1, kvw, s), lambda i, j: (i, vblk, 0)),
                  pl.BlockSpec((1, lc, kvw), lambda i, j: (i, 0, 12)),
                  pl.BlockSpec((1, kvw, lc), lambda i, j: (i, vblk, 0)),
                  pl.BlockSpec((1, SWA_HEADS * DH), lambda i, j: (0, 0))],
        out_specs=pl.BlockSpec((1, BAND, SWA_HEADS * DH), lambda i, j: (i, j, 0)),
        out_shape=jax.ShapeDtypeStruct((b, s, SWA_HEADS * DH), BF16),
        scratch_shapes=[pltpu.VMEM((3 * BAND + lc, kvw), BF16),
                        pltpu.VMEM((kvw, 3 * BAND + lc), BF16)],
        compiler_params=_cparams(("arbitrary", "arbitrary")),
        name="swa_attn",
    )(sink, p, p, vt, pc, vtc, gain)


def _outproj_kernel(ad_ref, as_ref, wd_ref, ws_ref, x_ref, g1_ref, sh_ref, sc_ref, gn_ref,
                    x1_ref, h2_ref):
    a = (jnp.dot(ad_ref[0], wd_ref[...], preferred_element_type=F32)
         + jnp.dot(as_ref[0], ws_ref[...], preferred_element_type=F32))
    x1 = x_ref[0] + g1_ref[0] * a
    x1_ref[0] = x1
    h2_ref[0] = _modnorm(x1, gn_ref[...], sh_ref[0], sc_ref[0]).astype(BF16)


def _outproj(a_diff, a_swa, w_d, w_s, x, g1, sh2, sc2, gain, tm):
    b, s, d = x.shape
    wd = a_diff.shape[2]
    vec = pl.BlockSpec((1, 1, d), lambda i, j: (i, 0, 0))
    return pl.pallas_call(
        _outproj_kernel,
        grid=(b, s // tm),
        in_specs=[pl.BlockSpec((1, tm, wd), lambda i, j: (i, j, 0)),
                  pl.BlockSpec((1, tm, wd), lambda i, j: (i, j, 0)),
                  pl.BlockSpec((wd, d), lambda i, j: (0, 0)),
                  pl.BlockSpec((wd, d), lambda i, j: (0, 0)),
                  pl.BlockSpec((1, tm, d), lambda i, j: (i, j, 0)),
                  vec, vec, vec,
                  pl.BlockSpec((1, d), lambda i, j: (0, 0))],
        out_specs=[pl.BlockSpec((1, tm, d), lambda i, j: (i, j, 0)),
                   pl.BlockSpec((1, tm, d), lambda i, j: (i, j, 0))],
        out_shape=[jax.ShapeDtypeStruct((b, s, d), F32), jax.ShapeDtypeStruct((b, s, d), BF16)],
        compiler_params=_cparams(("arbitrary", "arbitrary")),
        name="out_proj",
    )(a_diff, a_swa, w_d, w_s, x, g1, sh2, sc2, gain)


def _extract_topk(s, rank, payload, k):
    vals, pays = [], []
    for _ in range(k):
        m = jnp.max(s, axis=0, keepdims=True)
        first = jnp.min(jnp.where(s == m, rank, jnp.inf), axis=0, keepdims=True)
        hit = rank == first
        vals.append(m)
        pays.append(first if payload is None
                    else jnp.sum(jnp.where(hit, payload, 0.0), axis=0, keepdims=True))
        s = jnp.where(hit, -jnp.inf, s)
    return jnp.concatenate(vals, axis=0), jnp.concatenate(pays, axis=0)


def _extract_topk_paired(s, k):
    half = s.shape[0] // 2
    a, b = s[:half], s[half:]
    ra = lax.broadcasted_iota(I32, a.shape, 0).astype(F32)
    rb = ra + float(half)
    b_wins = b > a
    w, l = jnp.where(b_wins, b, a), jnp.where(b_wins, a, b)
    rw, rl = jnp.where(b_wins, rb, ra), jnp.where(b_wins, ra, rb)
    vals, ids = [], []
    for _ in range(k):
        m = jnp.max(w, axis=0, keepdims=True)
        first = jnp.min(jnp.where(w == m, rw, jnp.inf), axis=0, keepdims=True)
        hit = rw == first
        vals.append(m)
        ids.append(first)
        w = jnp.where(hit, l, w)
        rw = jnp.where(hit, rl, rw)
        l = jnp.where(hit, -jnp.inf, l)
    return jnp.concatenate(vals, axis=0), jnp.concatenate(ids, axis=0)


def _pair_candidates(s1, i1, s2, i2):
    k, tt = s1.shape
    assert k == 16, "block layout below is written for 16 x 16 pairs"
    sub = lax.broadcasted_iota(I32, (8, tt), 0)
    vals = [s1[0:1] + s2, s1[1:2] + s2[0:8]]
    pos = [lax.broadcasted_iota(I32, (k, tt), 0), k + sub]
    ids = [i1[0:1] * N_KEYS + i2, i1[1:2] * N_KEYS + i2[0:8]]
    for a in range(2, 8):
        keep = sub < k // (a + 1)
        vals.append(jnp.where(keep, s1[a:a + 1] + s2[0:8], -jnp.inf))
        pos.append(a * k + sub)
        ids.append(i1[a:a + 1] * N_KEYS + i2[0:8])
    vals.append(s1[8:16] + s2[0:1])
    pos.append((sub + 8) * k)
    ids.append(i1[8:16] * N_KEYS + i2[0:1])
    return (jnp.concatenate(vals, axis=0), jnp.concatenate(pos, axis=0).astype(F32),
            jnp.concatenate(ids, axis=0))


def _peer_route_kernel(h_ref, wq_ref, keys_ref, idx_ref, gate_ref):
    q = jnp.dot(h_ref[...], wq_ref[...], preferred_element_type=F32).astype(BF16)
    for h in range(PEER_HEADS):
        halves = []
        for i in range(2):
            c0 = (h * 2 + i) * N_KEYS
            s = lax.dot_general(keys_ref[h, i], q[:, c0:c0 + N_KEYS], (((1,), (1,)), ((), ())),
                                preferred_element_type=F32)
            halves.append(_extract_topk_paired(s, PEER_TOPK))
        (s1, i1), (s2, i2) = halves
        cand, pos, cidx = _pair_candidates(s1, i1, s2, i2)
        top_s, top_i = _extract_topk(cand, pos, cidx, PEER_TOPK)
        e = jnp.exp(top_s - jnp.max(top_s, axis=0, keepdims=True))
        r0 = h * PEER_TOPK
        idx_ref[r0:r0 + PEER_TOPK, :] = top_i.astype(I32)
        gate_ref[r0:r0 + PEER_TOPK, :] = e / jnp.sum(e, axis=0, keepdims=True)


def _peer_route(h2, wq, keys, tt):
    t, d = h2.shape
    nq = wq.shape[1]
    return pl.pallas_call(
        _peer_route_kernel,
        grid=(t // tt,),
        in_specs=[pl.BlockSpec((tt, d), lambda i: (i, 0)),
                  pl.BlockSpec((d, nq), lambda i: (0, 0)),
                  pl.BlockSpec(keys.shape, lambda i: (0, 0, 0, 0))],
        out_specs=[pl.BlockSpec((PEER_SEL, tt), lambda i: (0, i)),
                   pl.BlockSpec((PEER_SEL, tt), lambda i: (0, i))],
        out_shape=[jax.ShapeDtypeStruct((PEER_SEL, t), I32), jax.ShapeDtypeStruct((PEER_SEL, t), F32)],
        compiler_params=_cparams(("arbitrary",)),
        name="peer_route",
    )(h2, wq, keys)


def _split_bf16(x):
    hi = x.astype(BF16)
    return hi, (x - hi.astype(F32)).astype(BF16)


def _peer_expert_kernel(idx_cur_ref, idx_nxt_ref, uv_ref, h_ref, gate_ref, seg_ref, grpt_ref,
                        o_ref, buf_ref, sem_ref, *, tb, tg):
    i = pl.program_id(0)
    nsteps = pl.num_programs(0)

    def row_copy(idx_ref, slot, t, j):
        return pltpu.make_async_copy(uv_ref.at[idx_ref[t * PEER_SEL + j]],
                                     buf_ref.at[slot, t, j], sem_ref.at[slot, t])

    def wait_tokens(slot, toks):
        for t in toks:
            for j in range(PEER_SEL):
                row_copy(idx_cur_ref, slot, t, j).wait()

    @pl.when(i == 0)
    def _():
        for t in range(tb):
            for j in range(PEER_SEL):
                row_copy(idx_cur_ref, 0, t, j).start(priority=j % 2)

    diag = (lax.broadcasted_iota(I32, (SLAB, PEER_SEL * SLAB), 1) & (SLAB - 1)) == \
        lax.broadcasted_iota(I32, (SLAB, PEER_SEL * SLAB), 0)

    def consume(slot):
        copies = [(t, j) for t in range(tb) for j in range(PEER_SEL)]
        issue_points = (tb + GATE_LAG) * (PEER_SEL // SLAB) * 2
        progress = {"copies": 0, "points": 0}

        def issue_point():
            progress["points"] += 1
            upto = -(-len(copies) * progress["points"] // issue_points)
            while progress["copies"] < upto:
                t, j = copies[progress["copies"]]
                row_copy(idx_nxt_ref, 1 - slot, t, j).start(priority=j % 2)
                progress["copies"] += 1

        wait_tokens(slot, range(tb // 2))
        g_hi, g_lo = _split_bf16(gate_ref[...])
        gate_rep = (jnp.dot(g_hi, grpt_ref[...], preferred_element_type=F32)
                    + jnp.dot(g_lo, grpt_ref[...], preferred_element_type=F32))

        def token_pair(t_u, t_v, w_row):
            z_parts, r = [], None
            if t_v is not None:
                wt = jnp.where(diag, jnp.broadcast_to(w_row, diag.shape), 0.0)
                wt_hi, wt_lo = _split_bf16(wt)
                wt2 = jnp.concatenate([wt_hi, wt_lo], axis=0)
            for n in range(PEER_SEL // SLAB):
                e0, c0 = n * SLAB, n * SLAB * SLAB
                issue_point()
                if t_u is not None:
                    u_tile = buf_ref[slot, t_u, e0:e0 + SLAB, 0:SLAB, :].reshape(SLAB * SLAB, LANES)
                    y = lax.dot_general(h_ref[t_u], u_tile, (((1,), (1,)), ((), ())),
                                        preferred_element_type=F32)
                    z_parts.append(jnp.sum(jnp.where(diag[:, :SLAB * SLAB], y, 0.0),
                                           axis=0, keepdims=True))
                issue_point()
                if t_v is not None:
                    v_tile = buf_ref[slot, t_v, e0:e0 + SLAB, SLAB:2 * SLAB, :].reshape(
                        SLAB * SLAB, LANES)
                    part = jnp.dot(wt2[:, c0:c0 + SLAB * SLAB], v_tile, preferred_element_type=F32)
                    r = part if r is None else r + part
            z = jnp.concatenate(z_parts, axis=1) if z_parts else None
            out = None if r is None else r[:SLAB] + r[SLAB:]
            return z, out

        def gate_rows(g, z):
            tile = SLAB * SLAB
            z8 = jnp.concatenate([z, jnp.zeros((8 - tg, z.shape[1]), F32)], axis=0) if tg < 8 else z
            stacked = jnp.concatenate([z8[:, n * tile:(n + 1) * tile]
                                       for n in range(z.shape[1] // tile)], axis=0)
            s_hi, s_lo = _split_bf16(stacked)
            seg = jnp.dot(jnp.concatenate([s_hi, s_lo], axis=0), seg_ref[...],
                          preferred_element_type=F32)
            seg = seg[:stacked.shape[0]] + seg[stacked.shape[0]:]
            act = jnp.concatenate([seg[8 * n:8 * n + tg] for n in range(z.shape[1] // tile)], axis=1)
            gelu = 0.5 * act * (1.0 + jnp.tanh(math.sqrt(2.0 / math.pi)
                                               * (act + 0.044715 * act * act * act)))
            return gate_rep[g * tg:(g + 1) * tg, :] * gelu

        assert GATE_LAG >= tg + 2
        w_reps, outs, zs = {}, {}, []
        for n in range(tb + GATE_LAG):
            t_u = n if n < tb else None
            t_v = n - GATE_LAG if n >= GATE_LAG else None
            w_row = None if t_v is None else w_reps[t_v // tg][t_v % tg:t_v % tg + 1]
            if n == tb // 2:
                wait_tokens(slot, range(tb // 2, tb))
            z, out = token_pair(t_u, t_v, w_row)
            if t_u is not None:
                zs.append(z)
            if t_v is not None:
                outs[t_v] = out
            if n % tg == 0 and tg <= n <= tb:
                g = n // tg - 1
                w_reps[g] = gate_rows(g, jnp.concatenate(zs[g * tg:(g + 1) * tg], axis=0))
        for t in range(tb):
            for r in range(SLAB):
                o_ref[t:t + 1, r * LANES:(r + 1) * LANES] = outs[t][r:r + 1, :]

        @pl.when(i == nsteps - 1)
        def _():
            wait_tokens(1 - slot, range(tb))

    @pl.when((i & 1) == 0)
    def _():
        consume(0)

    @pl.when((i & 1) == 1)
    def _():
        consume(1)


def _peer_expert(idx_flat, uv, h_slab, gate, tb):
    t = h_slab.shape[0]
    nsteps = t // tb
    tile = SLAB * SLAB
    seg = (lax.broadcasted_iota(I32, (tile, tile), 0) // SLAB
           == lax.broadcasted_iota(I32, (tile, tile), 1) // SLAB).astype(BF16)
    grpt = (lax.broadcasted_iota(I32, (PEER_SEL, PEER_SEL * SLAB), 0)
            == lax.broadcasted_iota(I32, (PEER_SEL, PEER_SEL * SLAB), 1) // SLAB).astype(BF16)
    blk = tb * PEER_SEL
    return pl.pallas_call(
        functools.partial(_peer_expert_kernel, tb=tb, tg=min(4, tb)),
        grid=(nsteps,),
        in_specs=[pl.BlockSpec((blk,), lambda i: (i,), memory_space=pltpu.SMEM),
                  pl.BlockSpec((blk,), lambda i: (jnp.minimum(i + 1, nsteps - 1),),
                               memory_space=pltpu.SMEM),
                  pl.BlockSpec(memory_space=pl.ANY),
                  pl.BlockSpec((tb, SLAB, LANES), lambda i: (i, 0, 0)),
                  pl.BlockSpec((tb, PEER_SEL), lambda i: (i, 0)),
                  pl.BlockSpec(seg.shape, lambda i: (0, 0)),
                  pl.BlockSpec(grpt.shape, lambda i: (0, 0))],
        out_specs=pl.BlockSpec((tb, SLAB * LANES), lambda i: (i, 0)),
        out_shape=jax.ShapeDtypeStruct((t, SLAB * LANES), F32),
        scratch_shapes=[pltpu.VMEM((2, tb, PEER_SEL, 2 * SLAB, LANES), BF16),
                        pltpu.SemaphoreType.DMA((2, tb))],
        compiler_params=_cparams(("arbitrary",)),
        name="peer_expert",
    )(idx_flat, idx_flat, uv, h_slab, gate, seg, grpt)


def _final_kernel(x1_ref, pe_ref, g2_ref, gn_ref, o_ref):
    x2 = x1_ref[0] + g2_ref[0] * pe_ref[0]
    o_ref[0] = x2 * lax.rsqrt(jnp.mean(x2 * x2, axis=-1, keepdims=True) + EPS) * gn_ref[...]


def _final(x1, pe, g2, gain, tm):
    b, s, d = x1.shape
    tok = pl.BlockSpec((1, tm, d), lambda i, j: (i, j, 0))
    return pl.pallas_call(
        _final_kernel,
        grid=(b, s // tm),
        in_specs=[tok, tok, pl.BlockSpec((1, 1, d), lambda i, j: (i, 0, 0)),
                  pl.BlockSpec((1, d), lambda i, j: (0, 0))],
        out_specs=tok,
        out_shape=jax.ShapeDtypeStruct((b, s, d), F32),
        compiler_params=_cparams(("arbitrary", "arbitrary")),
        name="final_norm",
    )(x1, pe, g2, gain)


def kernel(x, c, ctx, c_ctx, w_ada, b_ada, norm_attn, w_in, diff_lambda_q1, diff_lambda_k1,
           diff_lambda_q2, diff_lambda_k2, diff_norm, swa_sink, swa_norm, w_out, norm_ffn,
           peer_w_q, peer_sub_keys, peer_u, peer_v, final_norm):
    b, s, d = x.shape
    assert w_ada.shape[0] == 1, "single layer only"
    t = b * s

    cc = jnp.zeros((8, d), F32).at[:b].set(c).at[b].set(c_ctx)
    mod = _adaln(cc, w_ada[0], b_ada[0])
    sh1, sc1, g1, sh2, sc2, g2 = [m[:, None, :] for m in jnp.split(mod, 6, axis=-1)]

    w_in_b = w_in[0].astype(BF16)
    gain_attn = norm_attn[0].reshape(1, d)
    cos_t, sin_t = _rope_tables(s)
    tm = min(512, s)
    dvw = DIFF_HEADS * LANES
    kvw = SWA_KV * DH
    w_qk = jnp.concatenate([w_in_b[:, :2 * dvw], w_in_b[:, 3 * dvw:-kvw]], axis=1)
    w_vt = jnp.concatenate([w_in_b[:, 2 * dvw:3 * dvw], w_in_b[:, -kvw:]], axis=1).T
    p, vt = _inproj(x, sh1[:b], sc1[:b], gain_attn, w_qk, w_vt, cos_t, sin_t, True, tm)
    lc = ctx.shape[1]
    ones = jnp.ones((lc, LANES), F32)
    ctx_sh = jnp.broadcast_to(sh1[b:b + 1], (b, 1, d))
    ctx_sc = jnp.broadcast_to(sc1[b:b + 1], (b, 1, d))
    pc, vtc = _inproj(ctx, ctx_sh, ctx_sc, gain_attn, w_qk, w_vt, ones, ones, False, lc)

    lams = [v[0].reshape(1, DH).astype(F32) for v in
            (diff_lambda_q1, diff_lambda_k1, diff_lambda_q2, diff_lambda_k2)]
    a_diff = _diff_attn(p, vt, pc, vtc, lams, diff_norm[0].reshape(1, -1), min(256, s), min(512, s))
    a_swa = _swa_attn(p, vt, pc, vtc, swa_sink[0].astype(F32), swa_norm[0].reshape(1, -1))

    w_out_b = w_out[0].astype(BF16)
    dw = a_diff.shape[2]
    x1, h2 = _outproj(a_diff, a_swa, w_out_b[:dw], w_out_b[dw:], x, g1[:b], sh2[:b], sc2[:b],
                      norm_ffn[0].reshape(1, d), tm)

    idx_t, gate_t = _peer_route(h2.reshape(t, d), peer_w_q[0].astype(BF16),
                                peer_sub_keys[0].astype(BF16), min(256, t))
    uv = jnp.concatenate([peer_u[0].reshape(-1, SLAB, LANES),
                          peer_v[0].reshape(-1, SLAB, LANES)], axis=1).astype(BF16)
    pe = _peer_expert(idx_t.T.reshape(-1), uv, h2.reshape(t, SLAB, LANES), gate_t.T, 16)

    return _final(x1, pe.reshape(b, s, d), g2[:b], final_norm.reshape(1, d), tm)
```

```python
import functools
import math

import jax
import jax.numpy as jnp
from jax import lax
from jax.experimental import pallas as pl
from jax.experimental.pallas import tpu as pltpu

F32 = jnp.float32
BF16 = jnp.bfloat16
I32 = jnp.int32

EPS = 1e-6
NEG_INF = -1e30
ROPE_THETA = 10000.0
GRID_W = 64
ROPE_PAIRS = 16

LANES = 128
DH = 64
DIFF_HEADS = 8
SWA_HEADS = 16
SWA_KV = 4
SWA_GROUP = SWA_HEADS // SWA_KV
BAND = 128
QK_CHUNKS = 26
LAM_INIT = 0.8 - 0.6 * math.exp(-0.3 * 0)
LOG2E = math.log2(math.e)

PEER_HEADS = 8
PEER_TOPK = 16
N_KEYS = 128
PEER_SEL = PEER_HEADS * PEER_TOPK
SLAB = 16
GATE_LAG = 6

VMEM_LIMIT = 56 * 1024 * 1024

TOKEN_TILE = 512
ADALN_COLS = 1024
DIFF_Q_TILE = 256
DIFF_K_TILE = 512
ROUTE_TILE = 256
EXPERT_BLOCK = 16
EXPERT_GROUP = 4


def _cparams(sem):
    return pltpu.CompilerParams(dimension_semantics=sem, vmem_limit_bytes=VMEM_LIMIT)


def _adaln_kernel(c_ref, w_ref, b_ref, o_ref):
    c = c_ref[...]
    s = c * (1.0 / (1.0 + jnp.exp(-c)))
    rows = s.shape[0]
    s_hi, s_lo = _split_bf16(s)
    w_hi, w_lo = _split_bf16(w_ref[...])
    both = jnp.dot(jnp.concatenate([s_hi, s_lo], axis=0), w_hi, preferred_element_type=F32)
    o_ref[...] = (both[:rows] + both[rows:] + jnp.dot(s_hi, w_lo, preferred_element_type=F32)
                  + b_ref[...])


def _adaln(cc, w, b):
    rows, d = cc.shape
    n = w.shape[1]
    tn = ADALN_COLS
    return pl.pallas_call(
        _adaln_kernel,
        grid=(n // tn,),
        in_specs=[pl.BlockSpec((rows, d), lambda j: (0, 0)),
                  pl.BlockSpec((d, tn), lambda j: (0, j)),
                  pl.BlockSpec((1, tn), lambda j: (0, j))],
        out_specs=pl.BlockSpec((rows, tn), lambda j: (0, j)),
        out_shape=jax.ShapeDtypeStruct((rows, n), F32),
        compiler_params=_cparams(("arbitrary",)),
        name="adaln",
    )(cc, w, b.reshape(1, n))


def _modnorm(x, gain, shift, scale):
    y = x * lax.rsqrt(jnp.mean(x * x, axis=-1, keepdims=True) + EPS)
    return (y * gain) * (1.0 + scale) + shift


def _swap16(p):
    lane = lax.broadcasted_iota(I32, p.shape, 1)
    up = pltpu.roll(p, LANES - 16, 1)
    dn = pltpu.roll(p, 16, 1)
    return jnp.where((lane & 31) < 16, up, dn)


def _inproj_kernel(x_ref, sh_ref, sc_ref, g_ref, w_ref, wvt_ref, cos_ref, sin_ref, o_ref, vt_ref, *,
                   rope_chunks, chunk_scale):
    h = _modnorm(x_ref[0], g_ref[...], sh_ref[0], sc_ref[0]).astype(BF16)
    p = jnp.dot(h, w_ref[...], preferred_element_type=F32)
    n_chunks = p.shape[1] // LANES
    if rope_chunks:
        cs = cos_ref[...]
        sn = sin_ref[...]
    for j in range(n_chunks):
        pj = p[:, j * LANES:(j + 1) * LANES]
        if j in rope_chunks:
            pj = pj * cs + _swap16(pj) * sn
        if j in chunk_scale:
            pj = pj * chunk_scale[j]
        o_ref[0, :, j * LANES:(j + 1) * LANES] = pj.astype(BF16)
    vt_ref[0] = lax.dot_general(wvt_ref[...], h, (((1,), (1,)), ((), ())),
                                preferred_element_type=F32).astype(BF16)


def _inproj(x, shift, scale, gain, w, w_vt, cos_t, sin_t, rope, tm):
    b, s, d = x.shape
    n = w.shape[1]
    nv = w_vt.shape[0]
    assert n == QK_CHUNKS * LANES
    rope_chunks = frozenset(range(QK_CHUNKS)) if rope else frozenset()
    chunk_scale = {j: DH ** -0.5 * LOG2E for j in range(0, 8)}
    chunk_scale.update({j: DH ** -0.5 for j in range(16, 24)})
    kern = functools.partial(_inproj_kernel, rope_chunks=rope_chunks, chunk_scale=chunk_scale)
    return pl.pallas_call(
        kern,
        grid=(b, s // tm),
        in_specs=[pl.BlockSpec((1, tm, d), lambda i, j: (i, j, 0)),
                  pl.BlockSpec((1, 1, d), lambda i, j: (i, 0, 0)),
                  pl.BlockSpec((1, 1, d), lambda i, j: (i, 0, 0)),
                  pl.BlockSpec((1, d), lambda i, j: (0, 0)),
                  pl.BlockSpec((d, n), lambda i, j: (0, 0), pipeline_mode=pl.Buffered(1)),
                  pl.BlockSpec((nv, d), lambda i, j: (0, 0), pipeline_mode=pl.Buffered(1)),
                  pl.BlockSpec((tm, LANES), lambda i, j: (j, 0)),
                  pl.BlockSpec((tm, LANES), lambda i, j: (j, 0))],
        out_specs=[pl.BlockSpec((1, tm, n), lambda i, j: (i, j, 0)),
                   pl.BlockSpec((1, nv, tm), lambda i, j: (i, 0, j))],
        out_shape=[jax.ShapeDtypeStruct((b, s, n), BF16), jax.ShapeDtypeStruct((b, nv, s), BF16)],
        compiler_params=_cparams(("arbitrary", "arbitrary")),
        name="in_proj",
    )(x, shift, scale, gain, w, w_vt, cos_t, sin_t)


def _rope_tables(s):
    rows = s // GRID_W
    row = jnp.repeat(jnp.arange(rows, dtype=F32), GRID_W)
    col = jnp.tile(jnp.arange(GRID_W, dtype=F32), rows)
    freqs = ROPE_THETA ** (-jnp.arange(ROPE_PAIRS, dtype=F32) / ROPE_PAIRS)
    ar = row[:, None] * freqs
    ac = col[:, None] * freqs
    cos64 = jnp.concatenate([jnp.cos(ar), jnp.cos(ar), jnp.cos(ac), jnp.cos(ac)], axis=1)
    sin64 = jnp.concatenate([-jnp.sin(ar), jnp.sin(ar), -jnp.sin(ac), jnp.sin(ac)], axis=1)
    return jnp.tile(cos64, (1, 2)), jnp.tile(sin64, (1, 2))


def _head_rms(o, gain):
    return o * lax.rsqrt(jnp.mean(o * o, axis=-1, keepdims=True) + EPS) * gain


def _diff_attn_kernel(lq1_ref, lk1_ref, lq2_ref, lk2_ref, q_ref, k_ref, vt_ref, kc_ref, vtc_ref,
                      gain_ref, o_ref, s_ref, m_ref, *, tk, n_tiles):
    j = pl.program_id(2)
    tq = q_ref.shape[1]
    s_len = k_ref.shape[1]
    lc = kc_ref.shape[1]
    blocks = [(0, lc, lambda: kc_ref[0], lambda: vtc_ref[0])]
    blocks += [(lc + b * tk, tk, lambda b=b: k_ref[0, b * tk:(b + 1) * tk, :],
                lambda b=b: vt_ref[0, :, b * tk:(b + 1) * tk]) for b in range(s_len // tk)]
    m_init = jnp.full((1, tq), NEG_INF, F32)

    def query_maps():
        q = q_ref[0]
        lane = lax.broadcasted_iota(I32, q.shape, 1)
        zero = jnp.zeros_like(q)
        return jnp.where(lane < DH, q, zero), jnp.where(lane >= DH, q, zero)

    def score_block(slot, i, blk, qm, m):
        r0, rows, kblk, _ = blk
        s = lax.dot_general(kblk(), qm, (((1,), (1,)), ((), ())), preferred_element_type=F32)
        s_ref[slot, i, r0:r0 + rows, :] = s
        return jnp.maximum(m, jnp.max(s, axis=0, keepdims=True))

    def prob_block(slot, i, blk, m, l, acc):
        r0, rows, _, vtblk = blk
        p = jnp.exp2(s_ref[slot, i, r0:r0 + rows, :] - m)
        part = jnp.dot(vtblk(), p.astype(BF16), preferred_element_type=F32)
        psum = jnp.sum(p, axis=0, keepdims=True)
        return (psum if l is None else l + psum), (part if acc is None else acc + part)

    def finish(l, acc):
        lam = (jnp.exp(jnp.sum(lq1_ref[...] * lk1_ref[...], keepdims=True))
               - jnp.exp(jnp.sum(lq2_ref[...] * lk2_ref[...], keepdims=True)) + LAM_INIT)
        out = (acc[0] / l[0] - lam * (acc[1] / l[1])).T
        o_ref[0] = (_head_rms(out, gain_ref[...]) * (1.0 - LAM_INIT)).astype(BF16)

    def step(score_slot, prob_slot):
        if score_slot is not None:
            qm = query_maps()
            m_new = [m_init, m_init]
        if prob_slot is not None:
            m_old = [m_ref[prob_slot, 0], m_ref[prob_slot, 1]]
            l, acc = [None, None], [None, None]
        for blk in blocks:
            for i in range(2):
                if score_slot is not None:
                    m_new[i] = score_block(score_slot, i, blk, qm[i], m_new[i])
            for i in range(2):
                if prob_slot is not None:
                    l[i], acc[i] = prob_block(prob_slot, i, blk, m_old[i], l[i], acc[i])
        if score_slot is not None:
            m_ref[score_slot, 0] = m_new[0]
            m_ref[score_slot, 1] = m_new[1]
        if prob_slot is not None:
            finish(l, acc)

    @pl.when(j == 0)
    def _():
        step(0, None)

    for parity in range(2):
        @pl.when((j > 0) & (j < n_tiles) & ((j & 1) == parity))
        def _():
            step(parity, 1 - parity)

    @pl.when(j == n_tiles)
    def _():
        step(None, (n_tiles - 1) % 2)


def _diff_attn(p, vt, pc, vtc, lams, gain, tq, tk):
    b, s, _ = p.shape
    lc = pc.shape[1]
    n_tiles = s // tq
    lam_spec = pl.BlockSpec((1, DH), lambda i, h, j: (0, 0))
    return pl.pallas_call(
        functools.partial(_diff_attn_kernel, tk=tk, n_tiles=n_tiles),
        grid=(b, DIFF_HEADS, n_tiles + 1),
        in_specs=[lam_spec, lam_spec, lam_spec, lam_spec,
                  pl.BlockSpec((1, tq, LANES), lambda i, h, j: (i, jnp.minimum(j, n_tiles - 1), h)),
                  pl.BlockSpec((1, s, LANES), lambda i, h, j: (i, 0, 8 + h)),
                  pl.BlockSpec((1, LANES, s), lambda i, h, j: (i, h, 0)),
                  pl.BlockSpec((1, lc, LANES), lambda i, h, j: (i, 0, 8 + h)),
                  pl.BlockSpec((1, LANES, lc), lambda i, h, j: (i, h, 0)),
                  pl.BlockSpec((1, LANES), lambda i, h, j: (0, h))],
        out_specs=pl.BlockSpec((1, tq, LANES), lambda i, h, j: (i, jnp.maximum(j - 1, 0), h)),
        out_shape=jax.ShapeDtypeStruct((b, s, DIFF_HEADS * LANES), BF16),
        scratch_shapes=[pltpu.VMEM((2, 2, lc + s, tq), F32), pltpu.VMEM((2, 2, 1, tq), F32)],
        compiler_params=_cparams(("arbitrary", "arbitrary", "arbitrary")),
        name="diff_attn",
    )(*lams, p, p, vt, pc, vtc, gain)


def _swa_attn_kernel(sink_ref, q_ref, k_ref, vt_ref, kc_ref, vtc_ref, gain_ref, o_ref,
                     kcat_ref, vtcat_ref):
    n = pl.program_id(1)
    nb = pl.num_programs(1)
    s_len = k_ref.shape[1]
    lc = kc_ref.shape[1]
    n_band = 3 * BAND

    @pl.when(n == 0)
    def _():
        kcat_ref[n_band:, :] = kc_ref[0]
        vtcat_ref[:, n_band:] = vtc_ref[0]

    prev = pl.multiple_of(jnp.maximum(n - 1, 0) * BAND, BAND)
    cur = pl.multiple_of(n * BAND, BAND)
    nxt = pl.multiple_of(jnp.minimum(n + 1, nb - 1) * BAND, BAND)
    for t, off in enumerate((prev, cur, nxt)):
        kcat_ref[t * BAND:(t + 1) * BAND, :] = k_ref[0, pl.ds(off, BAND), :]
        vtcat_ref[:, t * BAND:(t + 1) * BAND] = vt_ref[0, :, pl.ds(off, BAND)]

    keys = n_band + lc
    cols = SWA_GROUP * BAND
    ki = lax.broadcasted_iota(I32, (keys, cols), 0)
    qi = lax.broadcasted_iota(I32, (keys, cols), 1) & (BAND - 1)
    kpos = (n - 1) * BAND + ki
    in_band = jnp.where(jnp.abs(qi + BAND - ki) <= BAND,
                        jnp.where(kpos >= 0, jnp.where(kpos < s_len, 1, 0), 0), 0)
    ok = jnp.where(ki >= n_band, 1, in_band) > 0

    q = q_ref[0]
    outs = []
    for g in range(SWA_KV):
        kg = kcat_ref[:, g * DH:(g + 1) * DH]
        vtg = vtcat_ref[g * DH:(g + 1) * DH, :]
        qg = jnp.concatenate(
            [q[:, (g * SWA_GROUP + j) * DH:(g * SWA_GROUP + j + 1) * DH] for j in range(SWA_GROUP)],
            axis=0)
        s = lax.dot_general(kg, qg, (((1,), (1,)), ((), ())), preferred_element_type=F32)
        s = jnp.where(ok, s, NEG_INF)
        sink = jnp.concatenate(
            [jnp.full((1, BAND), sink_ref[g * SWA_GROUP + j], F32) for j in range(SWA_GROUP)],
            axis=1)
        m = jnp.maximum(jnp.max(s, axis=0, keepdims=True), sink)
        e = jnp.exp(s - m)
        denom = jnp.sum(e, axis=0, keepdims=True) + jnp.exp(sink - m)
        o = jnp.dot(vtg, e.astype(BF16), preferred_element_type=F32) / denom
        outs.append(o * lax.rsqrt(jnp.mean(o * o, axis=0, keepdims=True) + EPS))
    o_all = jnp.concatenate(outs, axis=0).T
    for g in range(SWA_KV):
        for j in range(SWA_GROUP):
            c0 = (g * SWA_GROUP + j) * DH
            o_ref[0, :, c0:c0 + DH] = (o_all[j * BAND:(j + 1) * BAND, g * DH:(g + 1) * DH]
                                       * gain_ref[:, c0:c0 + DH]).astype(BF16)


def _swa_attn(p, vt, pc, vtc, sink, gain):
    b, s, _ = p.shape
    lc = pc.shape[1]
    kvw = SWA_KV * DH
    vblk = vt.shape[1] // kvw - 1
    return pl.pallas_call(
        _swa_attn_kernel,
        grid=(b, s // BAND),
        in_specs=[pl.BlockSpec(memory_space=pltpu.SMEM),
                  pl.BlockSpec((1, BAND, SWA_HEADS * DH), lambda i, j: (i, j, 2)),
                  pl.BlockSpec((1, s, kvw), lambda i, j: (i, 0, 12)),
                  pl.BlockSpec((1, kvw, s), lambda i, j: (i, vblk, 0)),
                  pl.BlockSpec((1, lc, kvw), lambda i, j: (i, 0, 12)),
                  pl.BlockSpec((1, kvw, lc), lambda i, j: (i, vblk, 0)),
                  pl.BlockSpec((1, SWA_HEADS * DH), lambda i, j: (0, 0))],
        out_specs=pl.BlockSpec((1, BAND, SWA_HEADS * DH), lambda i, j: (i, j, 0)),
        out_shape=jax.ShapeDtypeStruct((b, s, SWA_HEADS * DH), BF16),
        scratch_shapes=[pltpu.VMEM((3 * BAND + lc, kvw), BF16),
                        pltpu.VMEM((kvw, 3 * BAND + lc), BF16)],
        compiler_params=_cparams(("arbitrary", "arbitrary")),
        name="swa_attn",
    )(sink, p, p, vt, pc, vtc, gain)


def _outproj_kernel(ad_ref, as_ref, wd_ref, ws_ref, x_ref, g1_ref, sh_ref, sc_ref, gn_ref,
                    x1_ref, h2_ref):
    a = (jnp.dot(ad_ref[0], wd_ref[...], preferred_element_type=F32)
         + jnp.dot(as_ref[0], ws_ref[...], preferred_element_type=F32))
    x1 = x_ref[0] + g1_ref[0] * a
    x1_ref[0] = x1
    h2_ref[0] = _modnorm(x1, gn_ref[...], sh_ref[0], sc_ref[0]).astype(BF16)


def _outproj(a_diff, a_swa, w_d, w_s, x, g1, sh2, sc2, gain, tm):
    b, s, d = x.shape
    wd = a_diff.shape[2]
    vec = pl.BlockSpec((1, 1, d), lambda i, j: (i, 0, 0))
    return pl.pallas_call(
        _outproj_kernel,
        grid=(b, s // tm),
        in_specs=[pl.BlockSpec((1, tm, wd), lambda i, j: (i, j, 0)),
                  pl.BlockSpec((1, tm, wd), lambda i, j: (i, j, 0)),
                  pl.BlockSpec((wd, d), lambda i, j: (0, 0)),
                  pl.BlockSpec((wd, d), lambda i, j: (0, 0)),
                  pl.BlockSpec((1, tm, d), lambda i, j: (i, j, 0)),
                  vec, vec, vec,
                  pl.BlockSpec((1, d), lambda i, j: (0, 0))],
        out_specs=[pl.BlockSpec((1, tm, d), lambda i, j: (i, j, 0)),
                   pl.BlockSpec((1, tm, d), lambda i, j: (i, j, 0))],
        out_shape=[jax.ShapeDtypeStruct((b, s, d), F32), jax.ShapeDtypeStruct((b, s, d), BF16)],
        compiler_params=_cparams(("arbitrary", "arbitrary")),
        name="out_proj",
    )(a_diff, a_swa, w_d, w_s, x, g1, sh2, sc2, gain)


def _extract_topk(s, rank, payload, k):
    vals, pays = [], []
    for _ in range(k):
        m = jnp.max(s, axis=0, keepdims=True)
        first = jnp.min(jnp.where(s == m, rank, jnp.inf), axis=0, keepdims=True)
        hit = rank == first
        vals.append(m)
        pays.append(first if payload is None
                    else jnp.sum(jnp.where(hit, payload, 0.0), axis=0, keepdims=True))
        s = jnp.where(hit, -jnp.inf, s)
    return jnp.concatenate(vals, axis=0), jnp.concatenate(pays, axis=0)


def _extract_topk_paired(s, k):
    half = s.shape[0] // 2
    a, b = s[:half], s[half:]
    ra = lax.broadcasted_iota(I32, a.shape, 0).astype(F32)
    rb = ra + float(half)
    b_wins = b > a
    w, l = jnp.where(b_wins, b, a), jnp.where(b_wins, a, b)
    rw, rl = jnp.where(b_wins, rb, ra), jnp.where(b_wins, ra, rb)
    vals, ids = [], []
    for _ in range(k):
        m = jnp.max(w, axis=0, keepdims=True)
        first = jnp.min(jnp.where(w == m, rw, jnp.inf), axis=0, keepdims=True)
        hit = rw == first
        vals.append(m)
        ids.append(first)
        w = jnp.where(hit, l, w)
        rw = jnp.where(hit, rl, rw)
        l = jnp.where(hit, -jnp.inf, l)
    return jnp.concatenate(vals, axis=0), jnp.concatenate(ids, axis=0)


def _pair_candidates(s1, i1, s2, i2):
    k, tt = s1.shape
    assert k == 16, "block layout below is written for 16 x 16 pairs"
    sub = lax.broadcasted_iota(I32, (8, tt), 0)
    vals = [s1[0:1] + s2, s1[1:2] + s2[0:8]]
    pos = [lax.broadcasted_iota(I32, (k, tt), 0), k + sub]
    ids = [i1[0:1] * N_KEYS + i2, i1[1:2] * N_KEYS + i2[0:8]]
    for a in range(2, 8):
        keep = sub < k // (a + 1)
        vals.append(jnp.where(keep, s1[a:a + 1] + s2[0:8], -jnp.inf))
        pos.append(a * k + sub)
        ids.append(i1[a:a + 1] * N_KEYS + i2[0:8])
    vals.append(s1[8:16] + s2[0:1])
    pos.append((sub + 8) * k)
    ids.append(i1[8:16] * N_KEYS + i2[0:1])
    return (jnp.concatenate(vals, axis=0), jnp.concatenate(pos, axis=0).astype(F32),
            jnp.concatenate(ids, axis=0))


def _peer_route_kernel(h_ref, wq_ref, keys_ref, idx_ref, gate_ref):
    q = jnp.dot(h_ref[...], wq_ref[...], preferred_element_type=F32).astype(BF16)
    for h in range(PEER_HEADS):
        halves = []
        for i in range(2):
            c0 = (h * 2 + i) * N_KEYS
            s = lax.dot_general(keys_ref[h, i], q[:, c0:c0 + N_KEYS], (((1,), (1,)), ((), ())),
                                preferred_element_type=F32)
            halves.append(_extract_topk_paired(s, PEER_TOPK))
        (s1, i1), (s2, i2) = halves
        cand, pos, cidx = _pair_candidates(s1, i1, s2, i2)
        top_s, top_i = _extract_topk(cand, pos, cidx, PEER_TOPK)
        e = jnp.exp(top_s - jnp.max(top_s, axis=0, keepdims=True))
        r0 = h * PEER_TOPK
        idx_ref[r0:r0 + PEER_TOPK, :] = top_i.astype(I32)
        gate_ref[r0:r0 + PEER_TOPK, :] = e / jnp.sum(e, axis=0, keepdims=True)


def _peer_route(h2, wq, keys, tt):
    t, d = h2.shape
    nq = wq.shape[1]
    return pl.pallas_call(
        _peer_route_kernel,
        grid=(t // tt,),
        in_specs=[pl.BlockSpec((tt, d), lambda i: (i, 0)),
                  pl.BlockSpec((d, nq), lambda i: (0, 0)),
                  pl.BlockSpec(keys.shape, lambda i: (0, 0, 0, 0))],
        out_specs=[pl.BlockSpec((PEER_SEL, tt), lambda i: (0, i)),
                   pl.BlockSpec((PEER_SEL, tt), lambda i: (0, i))],
        out_shape=[jax.ShapeDtypeStruct((PEER_SEL, t), I32), jax.ShapeDtypeStruct((PEER_SEL, t), F32)],
        compiler_params=_cparams(("arbitrary",)),
        name="peer_route",
    )(h2, wq, keys)


def _split_bf16(x):
    hi = x.astype(BF16)
    return hi, (x - hi.astype(F32)).astype(BF16)


def _peer_expert_kernel(idx_cur_ref, idx_nxt_ref, uv_ref, h_ref, gate_ref, seg_ref, grpt_ref,
                        o_ref, buf_ref, sem_ref, *, tb, tg):
    i = pl.program_id(0)
    nsteps = pl.num_programs(0)

    def row_copy(idx_ref, slot, t, j):
        return pltpu.make_async_copy(uv_ref.at[idx_ref[t * PEER_SEL + j]],
                                     buf_ref.at[slot, t, j], sem_ref.at[slot, t])

    def wait_tokens(slot, toks):
        for t in toks:
            for j in range(PEER_SEL):
                row_copy(idx_cur_ref, slot, t, j).wait()

    @pl.when(i == 0)
    def _():
        for t in range(tb):
            for j in range(PEER_SEL):
                row_copy(idx_cur_ref, 0, t, j).start(priority=j % 2)

    diag = (lax.broadcasted_iota(I32, (SLAB, PEER_SEL * SLAB), 1) & (SLAB - 1)) == \
        lax.broadcasted_iota(I32, (SLAB, PEER_SEL * SLAB), 0)

    def consume(slot):
        copies = [(t, j) for t in range(tb) for j in range(PEER_SEL)]
        issue_points = (tb + GATE_LAG) * (PEER_SEL // SLAB) * 2
        progress = {"copies": 0, "points": 0}

        def issue_point():
            progress["points"] += 1
            upto = -(-len(copies) * progress["points"] // issue_points)
            while progress["copies"] < upto:
                t, j = copies[progress["copies"]]
                row_copy(idx_nxt_ref, 1 - slot, t, j).start(priority=j % 2)
                progress["copies"] += 1

        wait_tokens(slot, range(tb // 2))
        g_hi, g_lo = _split_bf16(gate_ref[...])
        gate_rep = (jnp.dot(g_hi, grpt_ref[...], preferred_element_type=F32)
                    + jnp.dot(g_lo, grpt_ref[...], preferred_element_type=F32))

        def token_pair(t_u, t_v, w_row):
            z_parts, r = [], None
            if t_v is not None:
                wt = jnp.where(diag, jnp.broadcast_to(w_row, diag.shape), 0.0)
                wt_hi, wt_lo = _split_bf16(wt)
                wt2 = jnp.concatenate([wt_hi, wt_lo], axis=0)
            for n in range(PEER_SEL // SLAB):
                e0, c0 = n * SLAB, n * SLAB * SLAB
                issue_point()
                if t_u is not None:
                    u_tile = buf_ref[slot, t_u, e0:e0 + SLAB, 0:SLAB, :].reshape(SLAB * SLAB, LANES)
                    y = lax.dot_general(h_ref[t_u], u_tile, (((1,), (1,)), ((), ())),
                                        preferred_element_type=F32)
                    z_parts.append(jnp.sum(jnp.where(diag[:, :SLAB * SLAB], y, 0.0),
                                           axis=0, keepdims=True))
                issue_point()
                if t_v is not None:
                    v_tile = buf_ref[slot, t_v, e0:e0 + SLAB, SLAB:2 * SLAB, :].reshape(
                        SLAB * SLAB, LANES)
                    part = jnp.dot(wt2[:, c0:c0 + SLAB * SLAB], v_tile, preferred_element_type=F32)
                    r = part if r is None else r + part
            z = jnp.concatenate(z_parts, axis=1) if z_parts else None
            out = None if r is None else r[:SLAB] + r[SLAB:]
            return z, out

        def gate_rows(g, z):
            tile = SLAB * SLAB
            z8 = jnp.concatenate([z, jnp.zeros((8 - tg, z.shape[1]), F32)], axis=0) if tg < 8 else z
            stacked = jnp.concatenate([z8[:, n * tile:(n + 1) * tile]
                                       for n in range(z.shape[1] // tile)], axis=0)
            s_hi, s_lo = _split_bf16(stacked)
            seg = jnp.dot(jnp.concatenate([s_hi, s_lo], axis=0), seg_ref[...],
                          preferred_element_type=F32)
            seg = seg[:stacked.shape[0]] + seg[stacked.shape[0]:]
            act = jnp.concatenate([seg[8 * n:8 * n + tg] for n in range(z.shape[1] // tile)], axis=1)
            gelu = 0.5 * act * (1.0 + jnp.tanh(math.sqrt(2.0 / math.pi)
                                               * (act + 0.044715 * act * act * act)))
            return gate_rep[g * tg:(g + 1) * tg, :] * gelu

        assert GATE_LAG >= tg + 2
        w_reps, outs, zs = {}, {}, []
        for n in range(tb + GATE_LAG):
            t_u = n if n < tb else None
            t_v = n - GATE_LAG if n >= GATE_LAG else None
            w_row = None if t_v is None else w_reps[t_v // tg][t_v % tg:t_v % tg + 1]
            if n == tb // 2:
                wait_tokens(slot, range(tb // 2, tb))
            z, out = token_pair(t_u, t_v, w_row)
            if t_u is not None:
                zs.append(z)
            if t_v is not None:
                outs[t_v] = out
            if n % tg == 0 and tg <= n <= tb:
                g = n // tg - 1
                w_reps[g] = gate_rows(g, jnp.concatenate(zs[g * tg:(g + 1) * tg], axis=0))
        for t in range(tb):
            for r in range(SLAB):
                o_ref[t:t + 1, r * LANES:(r + 1) * LANES] = outs[t][r:r + 1, :]

        @pl.when(i == nsteps - 1)
        def _():
            wait_tokens(1 - slot, range(tb))

    @pl.when((i & 1) == 0)
    def _():
        consume(0)

    @pl.when((i & 1) == 1)
    def _():
        consume(1)


def _peer_expert(idx_flat, uv, h_slab, gate, tb):
    t = h_slab.shape[0]
    nsteps = t // tb
    tile = SLAB * SLAB
    seg = (lax.broadcasted_iota(I32, (tile, tile), 0) // SLAB
           == lax.broadcasted_iota(I32, (tile, tile), 1) // SLAB).astype(BF16)
    grpt = (lax.broadcasted_iota(I32, (PEER_SEL, PEER_SEL * SLAB), 0)
            == lax.broadcasted_iota(I32, (PEER_SEL, PEER_SEL * SLAB), 1) // SLAB).astype(BF16)
    blk = tb * PEER_SEL
    return pl.pallas_call(
        functools.partial(_peer_expert_kernel, tb=tb, tg=min(EXPERT_GROUP, tb)),
        grid=(nsteps,),
        in_specs=[pl.BlockSpec((blk,), lambda i: (i,), memory_space=pltpu.SMEM),
                  pl.BlockSpec((blk,), lambda i: (jnp.minimum(i + 1, nsteps - 1),),
                               memory_space=pltpu.SMEM),
                  pl.BlockSpec(memory_space=pl.ANY),
                  pl.BlockSpec((tb, SLAB, LANES), lambda i: (i, 0, 0)),
                  pl.BlockSpec((tb, PEER_SEL), lambda i: (i, 0)),
                  pl.BlockSpec(seg.shape, lambda i: (0, 0)),
                  pl.BlockSpec(grpt.shape, lambda i: (0, 0))],
        out_specs=pl.BlockSpec((tb, SLAB * LANES), lambda i: (i, 0)),
        out_shape=jax.ShapeDtypeStruct((t, SLAB * LANES), F32),
        scratch_shapes=[pltpu.VMEM((2, tb, PEER_SEL, 2 * SLAB, LANES), BF16),
                        pltpu.SemaphoreType.DMA((2, tb))],
        compiler_params=_cparams(("arbitrary",)),
        name="peer_expert",
    )(idx_flat, idx_flat, uv, h_slab, gate, seg, grpt)


def _final_kernel(x1_ref, pe_ref, g2_ref, gn_ref, o_ref):
    x2 = x1_ref[0] + g2_ref[0] * pe_ref[0]
    o_ref[0] = x2 * lax.rsqrt(jnp.mean(x2 * x2, axis=-1, keepdims=True) + EPS) * gn_ref[...]


def _final(x1, pe, g2, gain, tm):
    b, s, d = x1.shape
    tok = pl.BlockSpec((1, tm, d), lambda i, j: (i, j, 0))
    return pl.pallas_call(
        _final_kernel,
        grid=(b, s // tm),
        in_specs=[tok, tok, pl.BlockSpec((1, 1, d), lambda i, j: (i, 0, 0)),
                  pl.BlockSpec((1, d), lambda i, j: (0, 0))],
        out_specs=tok,
        out_shape=jax.ShapeDtypeStruct((b, s, d), F32),
        compiler_params=_cparams(("arbitrary", "arbitrary")),
        name="final_norm",
    )(x1, pe, g2, gain)


def kernel(x, c, ctx, c_ctx, w_ada, b_ada, norm_attn, w_in, diff_lambda_q1, diff_lambda_k1,
           diff_lambda_q2, diff_lambda_k2, diff_norm, swa_sink, swa_norm, w_out, norm_ffn,
           peer_w_q, peer_sub_keys, peer_u, peer_v, final_norm):
    b, s, d = x.shape
    assert w_ada.shape[0] == 1, "single layer only"
    t = b * s
    assert b < 8 and s % GRID_W == 0 and s % BAND == 0 and s % min(TOKEN_TILE, s) == 0
    assert s % min(DIFF_Q_TILE, s) == 0 and s % min(DIFF_K_TILE, s) == 0
    assert t % min(ROUTE_TILE, t) == 0 and t % EXPERT_BLOCK == 0

    cc = jnp.zeros((8, d), F32).at[:b].set(c).at[b].set(c_ctx)
    mod = _adaln(cc, w_ada[0], b_ada[0])
    sh1, sc1, g1, sh2, sc2, g2 = [m[:, None, :] for m in jnp.split(mod, 6, axis=-1)]

    w_in_b = w_in[0].astype(BF16)
    gain_attn = norm_attn[0].reshape(1, d)
    cos_t, sin_t = _rope_tables(s)
    tm = min(TOKEN_TILE, s)
    dvw = DIFF_HEADS * LANES
    kvw = SWA_KV * DH
    w_qk = jnp.concatenate([w_in_b[:, :2 * dvw], w_in_b[:, 3 * dvw:-kvw]], axis=1)
    w_vt = jnp.concatenate([w_in_b[:, 2 * dvw:3 * dvw], w_in_b[:, -kvw:]], axis=1).T
    p, vt = _inproj(x, sh1[:b], sc1[:b], gain_attn, w_qk, w_vt, cos_t, sin_t, True, tm)
    lc = ctx.shape[1]
    ones = jnp.ones((lc, LANES), F32)
    ctx_sh = jnp.broadcast_to(sh1[b:b + 1], (b, 1, d))
    ctx_sc = jnp.broadcast_to(sc1[b:b + 1], (b, 1, d))
    pc, vtc = _inproj(ctx, ctx_sh, ctx_sc, gain_attn, w_qk, w_vt, ones, ones, False, lc)

    lams = [v[0].reshape(1, DH).astype(F32) for v in
            (diff_lambda_q1, diff_lambda_k1, diff_lambda_q2, diff_lambda_k2)]
    a_diff = _diff_attn(p, vt, pc, vtc, lams, diff_norm[0].reshape(1, -1),
                        min(DIFF_Q_TILE, s), min(DIFF_K_TILE, s))
    a_swa = _swa_attn(p, vt, pc, vtc, swa_sink[0].astype(F32), swa_norm[0].reshape(1, -1))

    w_out_b = w_out[0].astype(BF16)
    dw = a_diff.shape[2]
    x1, h2 = _outproj(a_diff, a_swa, w_out_b[:dw], w_out_b[dw:], x, g1[:b], sh2[:b], sc2[:b],
                      norm_ffn[0].reshape(1, d), tm)

    idx_t, gate_t = _peer_route(h2.reshape(t, d), peer_w_q[0].astype(BF16),
                                peer_sub_keys[0].astype(BF16), min(ROUTE_TILE, t))
    uv = jnp.concatenate([peer_u[0].reshape(-1, SLAB, LANES),
                          peer_v[0].reshape(-1, SLAB, LANES)], axis=1).astype(BF16)
    pe = _peer_expert(idx_t.T.reshape(-1), uv, h2.reshape(t, SLAB, LANES), gate_t.T, EXPERT_BLOCK)

    return _final(x1, pe.reshape(b, s, d), g2[:b], final_norm.reshape(1, d), tm)
```

```python
import functools
import math

import jax
import jax.numpy as jnp
from jax import lax
from jax.experimental import pallas as pl
from jax.experimental.pallas import tpu as pltpu

F32 = jnp.float32
BF16 = jnp.bfloat16
I32 = jnp.int32

EPS = 1e-6
NEG_INF = -1e30
ROPE_THETA = 10000.0
GRID_W = 64
ROPE_PAIRS = 16

LANES = 128
DH = 64
DIFF_HEADS = 8
SWA_HEADS = 16
SWA_KV = 4
SWA_GROUP = SWA_HEADS // SWA_KV
BAND = 128
QK_CHUNKS = 26
LAM_INIT = 0.8 - 0.6 * math.exp(-0.3 * 0)
LOG2E = math.log2(math.e)

PEER_HEADS = 8
PEER_TOPK = 16
N_KEYS = 128
PEER_SEL = PEER_HEADS * PEER_TOPK
SLAB = 16
GATE_LAG = 6

VMEM_LIMIT = 56 * 1024 * 1024

TOKEN_TILE = 512
ADALN_COLS = 1024
DIFF_Q_TILE = 256
DIFF_K_TILE = 512
ROUTE_TILE = 256
EXPERT_BLOCK = 16
EXPERT_GROUP = 4


def _cparams(sem):
    return pltpu.CompilerParams(dimension_semantics=sem, vmem_limit_bytes=VMEM_LIMIT)


def _adaln_kernel(c_ref, w_ref, b_ref, o_ref):
    c = c_ref[...]
    s = c * (1.0 / (1.0 + jnp.exp(-c)))
    rows = s.shape[0]
    s_hi, s_lo = _split_bf16(s)
    w_hi, w_lo = _split_bf16(w_ref[...])
    both = jnp.dot(jnp.concatenate([s_hi, s_lo], axis=0), w_hi, preferred_element_type=F32)
    o_ref[...] = (both[:rows] + both[rows:] + jnp.dot(s_hi, w_lo, preferred_element_type=F32)
                  + b_ref[...])


def _adaln(cc, w, b):
    rows, d = cc.shape
    n = w.shape[1]
    tn = ADALN_COLS
    return pl.pallas_call(
        _adaln_kernel,
        grid=(n // tn,),
        in_specs=[pl.BlockSpec((rows, d), lambda j: (0, 0)),
                  pl.BlockSpec((d, tn), lambda j: (0, j)),
                  pl.BlockSpec((1, tn), lambda j: (0, j))],
        out_specs=pl.BlockSpec((rows, tn), lambda j: (0, j)),
        out_shape=jax.ShapeDtypeStruct((rows, n), F32),
        compiler_params=_cparams(("arbitrary",)),
        name="adaln",
    )(cc, w, b.reshape(1, n))


def _modnorm(x, gain, shift, scale):
    y = x * lax.rsqrt(jnp.mean(x * x, axis=-1, keepdims=True) + EPS)
    return (y * gain) * (1.0 + scale) + shift


def _swap16(p):
    lane = lax.broadcasted_iota(I32, p.shape, 1)
    up = pltpu.roll(p, LANES - 16, 1)
    dn = pltpu.roll(p, 16, 1)
    return jnp.where((lane & 31) < 16, up, dn)


def _inproj_kernel(x_ref, sh_ref, sc_ref, g_ref, w_ref, wvt_ref, cos_ref, sin_ref, o_ref, vt_ref, *,
                   rope_chunks, chunk_scale):
    h = _modnorm(x_ref[0], g_ref[...], sh_ref[0], sc_ref[0]).astype(BF16)
    p = jnp.dot(h, w_ref[...], preferred_element_type=F32)
    n_chunks = p.shape[1] // LANES
    if rope_chunks:
        cs = cos_ref[...]
        sn = sin_ref[...]
    for j in range(n_chunks):
        pj = p[:, j * LANES:(j + 1) * LANES]
        if j in rope_chunks:
            pj = pj * cs + _swap16(pj) * sn
        if j in chunk_scale:
            pj = pj * chunk_scale[j]
        o_ref[0, :, j * LANES:(j + 1) * LANES] = pj.astype(BF16)
    vt_ref[0] = lax.dot_general(wvt_ref[...], h, (((1,), (1,)), ((), ())),
                                preferred_element_type=F32).astype(BF16)


def _inproj(x, shift, scale, gain, w, w_vt, cos_t, sin_t, rope, tm):
    b, s, d = x.shape
    n = w.shape[1]
    nv = w_vt.shape[0]
    assert n == QK_CHUNKS * LANES
    rope_chunks = frozenset(range(QK_CHUNKS)) if rope else frozenset()
    chunk_scale = {j: DH ** -0.5 * LOG2E for j in range(0, 8)}
    chunk_scale.update({j: DH ** -0.5 for j in range(16, 24)})
    kern = functools.partial(_inproj_kernel, rope_chunks=rope_chunks, chunk_scale=chunk_scale)
    return pl.pallas_call(
        kern,
        grid=(b, s // tm),
        in_specs=[pl.BlockSpec((1, tm, d), lambda i, j: (i, j, 0)),
                  pl.BlockSpec((1, 1, d), lambda i, j: (i, 0, 0)),
                  pl.BlockSpec((1, 1, d), lambda i, j: (i, 0, 0)),
                  pl.BlockSpec((1, d), lambda i, j: (0, 0)),
                  pl.BlockSpec((d, n), lambda i, j: (0, 0), pipeline_mode=pl.Buffered(1)),
                  pl.BlockSpec((nv, d), lambda i, j: (0, 0), pipeline_mode=pl.Buffered(1)),
                  pl.BlockSpec((tm, LANES), lambda i, j: (j, 0)),
                  pl.BlockSpec((tm, LANES), lambda i, j: (j, 0))],
        out_specs=[pl.BlockSpec((1, tm, n), lambda i, j: (i, j, 0)),
                   pl.BlockSpec((1, nv, tm), lambda i, j: (i, 0, j))],
        out_shape=[jax.ShapeDtypeStruct((b, s, n), BF16), jax.ShapeDtypeStruct((b, nv, s), BF16)],
        compiler_params=_cparams(("arbitrary", "arbitrary")),
        name="in_proj",
    )(x, shift, scale, gain, w, w_vt, cos_t, sin_t)


def _rope_tables(s):
    rows = s // GRID_W
    row = jnp.repeat(jnp.arange(rows, dtype=F32), GRID_W)
    col = jnp.tile(jnp.arange(GRID_W, dtype=F32), rows)
    freqs = ROPE_THETA ** (-jnp.arange(ROPE_PAIRS, dtype=F32) / ROPE_PAIRS)
    ar = row[:, None] * freqs
    ac = col[:, None] * freqs
    cos64 = jnp.concatenate([jnp.cos(ar), jnp.cos(ar), jnp.cos(ac), jnp.cos(ac)], axis=1)
    sin64 = jnp.concatenate([-jnp.sin(ar), jnp.sin(ar), -jnp.sin(ac), jnp.sin(ac)], axis=1)
    return jnp.tile(cos64, (1, 2)), jnp.tile(sin64, (1, 2))


def _head_rms(o, gain):
    return o * lax.rsqrt(jnp.mean(o * o, axis=-1, keepdims=True) + EPS) * gain


def _diff_attn_kernel(lq1_ref, lk1_ref, lq2_ref, lk2_ref, q_ref, k_ref, vt_ref, kc_ref, vtc_ref,
                      gain_ref, o_ref, s_ref, m_ref, *, tk, n_tiles):
    j = pl.program_id(2)
    tq = q_ref.shape[1]
    s_len = k_ref.shape[1]
    lc = kc_ref.shape[1]
    blocks = [(0, lc, lambda: kc_ref[0], lambda: vtc_ref[0])]
    blocks += [(lc + b * tk, tk, lambda b=b: k_ref[0, b * tk:(b + 1) * tk, :],
                lambda b=b: vt_ref[0, :, b * tk:(b + 1) * tk]) for b in range(s_len // tk)]
    m_init = jnp.full((1, tq), NEG_INF, F32)

    def query_maps():
        q = q_ref[0]
        lane = lax.broadcasted_iota(I32, q.shape, 1)
        zero = jnp.zeros_like(q)
        return jnp.where(lane < DH, q, zero), jnp.where(lane >= DH, q, zero)

    def score_block(slot, i, blk, qm, m):
        r0, rows, kblk, _ = blk
        s = lax.dot_general(kblk(), qm, (((1,), (1,)), ((), ())), preferred_element_type=F32)
        s_ref[slot, i, r0:r0 + rows, :] = s
        return jnp.maximum(m, jnp.max(s, axis=0, keepdims=True))

    def prob_block(slot, i, blk, m, l, acc):
        r0, rows, _, vtblk = blk
        p = jnp.exp2(s_ref[slot, i, r0:r0 + rows, :] - m)
        part = jnp.dot(vtblk(), p.astype(BF16), preferred_element_type=F32)
        psum = jnp.sum(p, axis=0, keepdims=True)
        return (psum if l is None else l + psum), (part if acc is None else acc + part)

    def finish(l, acc):
        lam = (jnp.exp(jnp.sum(lq1_ref[...] * lk1_ref[...], keepdims=True))
               - jnp.exp(jnp.sum(lq2_ref[...] * lk2_ref[...], keepdims=True)) + LAM_INIT)
        out = (acc[0] / l[0] - lam * (acc[1] / l[1])).T
        o_ref[0] = (_head_rms(out, gain_ref[...]) * (1.0 - LAM_INIT)).astype(BF16)

    def step(score_slot, prob_slot):
        if score_slot is not None:
            qm = query_maps()
            m_new = [m_init, m_init]
        if prob_slot is not None:
            m_old = [m_ref[prob_slot, 0], m_ref[prob_slot, 1]]
            l, acc = [None, None], [None, None]
        for blk in blocks:
            for i in range(2):
                if score_slot is not None:
                    m_new[i] = score_block(score_slot, i, blk, qm[i], m_new[i])
            for i in range(2):
                if prob_slot is not None:
                    l[i], acc[i] = prob_block(prob_slot, i, blk, m_old[i], l[i], acc[i])
        if score_slot is not None:
            m_ref[score_slot, 0] = m_new[0]
            m_ref[score_slot, 1] = m_new[1]
        if prob_slot is not None:
            finish(l, acc)

    @pl.when(j == 0)
    def _():
        step(0, None)

    for parity in range(2):
        @pl.when((j > 0) & (j < n_tiles) & ((j & 1) == parity))
        def _():
            step(parity, 1 - parity)

    @pl.when(j == n_tiles)
    def _():
        step(None, (n_tiles - 1) % 2)


def _diff_attn(p, vt, pc, vtc, lams, gain, tq, tk):
    b, s, _ = p.shape
    lc = pc.shape[1]
    n_tiles = s // tq
    lam_spec = pl.BlockSpec((1, DH), lambda i, h, j: (0, 0))
    return pl.pallas_call(
        functools.partial(_diff_attn_kernel, tk=tk, n_tiles=n_tiles),
        grid=(b, DIFF_HEADS, n_tiles + 1),
        in_specs=[lam_spec, lam_spec, lam_spec, lam_spec,
                  pl.BlockSpec((1, tq, LANES), lambda i, h, j: (i, jnp.minimum(j, n_tiles - 1), h)),
                  pl.BlockSpec((1, s, LANES), lambda i, h, j: (i, 0, 8 + h)),
                  pl.BlockSpec((1, LANES, s), lambda i, h, j: (i, h, 0)),
                  pl.BlockSpec((1, lc, LANES), lambda i, h, j: (i, 0, 8 + h)),
                  pl.BlockSpec((1, LANES, lc), lambda i, h, j: (i, h, 0)),
                  pl.BlockSpec((1, LANES), lambda i, h, j: (0, h))],
        out_specs=pl.BlockSpec((1, tq, LANES), lambda i, h, j: (i, jnp.maximum(j - 1, 0), h)),
        out_shape=jax.ShapeDtypeStruct((b, s, DIFF_HEADS * LANES), BF16),
        scratch_shapes=[pltpu.VMEM((2, 2, lc + s, tq), F32), pltpu.VMEM((2, 2, 1, tq), F32)],
        compiler_params=_cparams(("arbitrary", "arbitrary", "arbitrary")),
        name="diff_attn",
    )(*lams, p, p, vt, pc, vtc, gain)


def _swa_attn_kernel(sink_ref, q_ref, k_ref, vt_ref, kc_ref, vtc_ref, gain_ref, o_ref,
                     kcat_ref, vtcat_ref):
    n = pl.program_id(1)
    nb = pl.num_programs(1)
    s_len = k_ref.shape[1]
    lc = kc_ref.shape[1]
    n_band = 3 * BAND

    @pl.when(n == 0)
    def _():
        kcat_ref[n_band:, :] = kc_ref[0]
        vtcat_ref[:, n_band:] = vtc_ref[0]

    prev = pl.multiple_of(jnp.maximum(n - 1, 0) * BAND, BAND)
    cur = pl.multiple_of(n * BAND, BAND)
    nxt = pl.multiple_of(jnp.minimum(n + 1, nb - 1) * BAND, BAND)
    for t, off in enumerate((prev, cur, nxt)):
        kcat_ref[t * BAND:(t + 1) * BAND, :] = k_ref[0, pl.ds(off, BAND), :]
        vtcat_ref[:, t * BAND:(t + 1) * BAND] = vt_ref[0, :, pl.ds(off, BAND)]

    keys = n_band + lc
    cols = SWA_GROUP * BAND
    ki = lax.broadcasted_iota(I32, (keys, cols), 0)
    qi = lax.broadcasted_iota(I32, (keys, cols), 1) & (BAND - 1)
    kpos = (n - 1) * BAND + ki
    in_band = jnp.where(jnp.abs(qi + BAND - ki) <= BAND,
                        jnp.where(kpos >= 0, jnp.where(kpos < s_len, 1, 0), 0), 0)
    ok = jnp.where(ki >= n_band, 1, in_band) > 0

    q = q_ref[0]
    outs = []
    for g in range(SWA_KV):
        kg = kcat_ref[:, g * DH:(g + 1) * DH]
        vtg = vtcat_ref[g * DH:(g + 1) * DH, :]
        qg = jnp.concatenate(
            [q[:, (g * SWA_GROUP + j) * DH:(g * SWA_GROUP + j + 1) * DH] for j in range(SWA_GROUP)],
            axis=0)
        s = lax.dot_general(kg, qg, (((1,), (1,)), ((), ())), preferred_element_type=F32)
        s = jnp.where(ok, s, NEG_INF)
        sink = jnp.concatenate(
            [jnp.full((1, BAND), sink_ref[g * SWA_GROUP + j], F32) for j in range(SWA_GROUP)],
            axis=1)
        m = jnp.maximum(jnp.max(s, axis=0, keepdims=True), sink)
        e = jnp.exp(s - m)
        denom = jnp.sum(e, axis=0, keepdims=True) + jnp.exp(sink - m)
        o = jnp.dot(vtg, e.astype(BF16), preferred_element_type=F32) / denom
        outs.append(o * lax.rsqrt(jnp.mean(o * o, axis=0, keepdims=True) + EPS))
    o_all = jnp.concatenate(outs, axis=0).T
    for g in range(SWA_KV):
        for j in range(SWA_GROUP):
            c0 = (g * SWA_GROUP + j) * DH
            o_ref[0, :, c0:c0 + DH] = (o_all[j * BAND:(j + 1) * BAND, g * DH:(g + 1) * DH]
                                       * gain_ref[:, c0:c0 + DH]).astype(BF16)


def _swa_attn(p, vt, pc, vtc, sink, gain):
    b, s, _ = p.shape
    lc = pc.shape[1]
    kvw = SWA_KV * DH
    vblk = vt.shape[1] // kvw - 1
    return pl.pallas_call(
        _swa_attn_kernel,
        grid=(b, s // BAND),
        in_specs=[pl.BlockSpec(memory_space=pltpu.SMEM),
                  pl.BlockSpec((1, BAND, SWA_HEADS * DH), lambda i, j: (i, j, 2)),
                  pl.BlockSpec((1, s, kvw), lambda i, j: (i, 0, 12)),
                  pl.BlockSpec((1, kvw, s), lambda i, j: (i, vblk, 0)),
                  pl.BlockSpec((1, lc, kvw), lambda i, j: (i, 0, 12)),
                  pl.BlockSpec((1, kvw, lc), lambda i, j: (i, vblk, 0)),
                  pl.BlockSpec((1, SWA_HEADS * DH), lambda i, j: (0, 0))],
        out_specs=pl.BlockSpec((1, BAND, SWA_HEADS * DH), lambda i, j: (i, j, 0)),
        out_shape=jax.ShapeDtypeStruct((b, s, SWA_HEADS * DH), BF16),
        scratch_shapes=[pltpu.VMEM((3 * BAND + lc, kvw), BF16),
                        pltpu.VMEM((kvw, 3 * BAND + lc), BF16)],
        compiler_params=_cparams(("arbitrary", "arbitrary")),
        name="swa_attn",
    )(sink, p, p, vt, pc, vtc, gain)


def _outproj_kernel(ad_ref, as_ref, wd_ref, ws_ref, x_ref, g1_ref, sh_ref, sc_ref, gn_ref,
                    x1_ref, h2_ref):
    a = (jnp.dot(ad_ref[0], wd_ref[...], preferred_element_type=F32)
         + jnp.dot(as_ref[0], ws_ref[...], preferred_element_type=F32))
    x1 = x_ref[0] + g1_ref[0] * a
    x1_ref[0] = x1
    h2_ref[0] = _modnorm(x1, gn_ref[...], sh_ref[0], sc_ref[0]).astype(BF16)


def _outproj(a_diff, a_swa, w_d, w_s, x, g1, sh2, sc2, gain, tm):
    b, s, d = x.shape
    wd = a_diff.shape[2]
    vec = pl.BlockSpec((1, 1, d), lambda i, j: (i, 0, 0))
    return pl.pallas_call(
        _outproj_kernel,
        grid=(b, s // tm),
        in_specs=[pl.BlockSpec((1, tm, wd), lambda i, j: (i, j, 0)),
                  pl.BlockSpec((1, tm, wd), lambda i, j: (i, j, 0)),
                  pl.BlockSpec((wd, d), lambda i, j: (0, 0)),
                  pl.BlockSpec((wd, d), lambda i, j: (0, 0)),
                  pl.BlockSpec((1, tm, d), lambda i, j: (i, j, 0)),
                  vec, vec, vec,
                  pl.BlockSpec((1, d), lambda i, j: (0, 0))],
        out_specs=[pl.BlockSpec((1, tm, d), lambda i, j: (i, j, 0)),
                   pl.BlockSpec((1, tm, d), lambda i, j: (i, j, 0))],
        out_shape=[jax.ShapeDtypeStruct((b, s, d), F32), jax.ShapeDtypeStruct((b, s, d), BF16)],
        compiler_params=_cparams(("arbitrary", "arbitrary")),
        name="out_proj",
    )(a_diff, a_swa, w_d, w_s, x, g1, sh2, sc2, gain)


def _extract_topk_paired(s, k):
    half = s.shape[0] // 2
    a, b = s[:half], s[half:]
    ra = lax.broadcasted_iota(I32, a.shape, 0).astype(F32)
    rb = ra + float(half)
    b_wins = b > a
    w, l = jnp.where(b_wins, b, a), jnp.where(b_wins, a, b)
    rw, rl = jnp.where(b_wins, rb, ra), jnp.where(b_wins, ra, rb)
    vals, ids = [], []
    for _ in range(k):
        m = jnp.max(w, axis=0, keepdims=True)
        first = jnp.min(jnp.where(w == m, rw, jnp.inf), axis=0, keepdims=True)
        hit = rw == first
        vals.append(m)
        ids.append(first)
        w = jnp.where(hit, l, w)
        rw = jnp.where(hit, rl, rw)
        l = jnp.where(hit, -jnp.inf, l)
    return jnp.concatenate(vals, axis=0), jnp.concatenate(ids, axis=0)


def _extract_topk_ranked_pairs(s, rank, payload, k):
    (a, b), (ra, rb), (pa, pb) = s, rank, payload
    b_wins = (b > a) | ((b == a) & (rb < ra))
    w, l = jnp.where(b_wins, b, a), jnp.where(b_wins, a, b)
    rw, rl = jnp.where(b_wins, rb, ra), jnp.where(b_wins, ra, rb)
    pw, pl_ = jnp.where(b_wins, pb, pa), jnp.where(b_wins, pa, pb)
    vals, pays = [], []
    for _ in range(k):
        m = jnp.max(w, axis=0, keepdims=True)
        first = jnp.min(jnp.where(w == m, rw, jnp.inf), axis=0, keepdims=True)
        hit = rw == first
        vals.append(m)
        pays.append(jnp.sum(jnp.where(hit, pw, 0.0), axis=0, keepdims=True))
        w, rw, pw = jnp.where(hit, l, w), jnp.where(hit, rl, rw), jnp.where(hit, pl_, pw)
        l = jnp.where(hit, -jnp.inf, l)
    return jnp.concatenate(vals, axis=0), jnp.concatenate(pays, axis=0)


def _pair_candidates(s1, i1, s2, i2):
    k, tt = s1.shape
    assert k == 16, "block layout below is written for 16 x 16 pairs"
    sub = lax.broadcasted_iota(I32, (8, tt), 0)
    vals = [s1[0:1] + s2, s1[1:2] + s2[0:8]]
    pos = [lax.broadcasted_iota(I32, (k, tt), 0), k + sub]
    ids = [i1[0:1] * N_KEYS + i2, i1[1:2] * N_KEYS + i2[0:8]]
    for a in range(2, 8):
        keep = sub < k // (a + 1)
        vals.append(jnp.where(keep, s1[a:a + 1] + s2[0:8], -jnp.inf))
        pos.append(a * k + sub)
        ids.append(i1[a:a + 1] * N_KEYS + i2[0:8])
    vals.append(s1[8:16] + s2[0:1])
    pos.append((sub + 8) * k)
    ids.append(i1[8:16] * N_KEYS + i2[0:1])
    def halves(parts, cast=False):
        lo, hi = jnp.concatenate(parts[:4], axis=0), jnp.concatenate(parts[4:], axis=0)
        return (lo.astype(F32), hi.astype(F32)) if cast else (lo, hi)

    return halves(vals), halves(pos, cast=True), halves(ids)


def _peer_route_kernel(h_ref, wq_ref, keys_ref, idx_ref, gate_ref):
    q = jnp.dot(h_ref[...], wq_ref[...], preferred_element_type=F32).astype(BF16)
    for h in range(PEER_HEADS):
        halves = []
        for i in range(2):
            c0 = (h * 2 + i) * N_KEYS
            s = lax.dot_general(keys_ref[h, i], q[:, c0:c0 + N_KEYS], (((1,), (1,)), ((), ())),
                                preferred_element_type=F32)
            halves.append(_extract_topk_paired(s, PEER_TOPK))
        (s1, i1), (s2, i2) = halves
        cand, pos, cidx = _pair_candidates(s1, i1, s2, i2)
        top_s, top_i = _extract_topk_ranked_pairs(cand, pos, cidx, PEER_TOPK)
        e = jnp.exp(top_s - jnp.max(top_s, axis=0, keepdims=True))
        r0 = h * PEER_TOPK
        idx_ref[r0:r0 + PEER_TOPK, :] = top_i.astype(I32)
        gate_ref[r0:r0 + PEER_TOPK, :] = e / jnp.sum(e, axis=0, keepdims=True)


def _peer_route(h2, wq, keys, tt):
    t, d = h2.shape
    nq = wq.shape[1]
    return pl.pallas_call(
        _peer_route_kernel,
        grid=(t // tt,),
        in_specs=[pl.BlockSpec((tt, d), lambda i: (i, 0)),
                  pl.BlockSpec((d, nq), lambda i: (0, 0)),
                  pl.BlockSpec(keys.shape, lambda i: (0, 0, 0, 0))],
        out_specs=[pl.BlockSpec((PEER_SEL, tt), lambda i: (0, i)),
                   pl.BlockSpec((PEER_SEL, tt), lambda i: (0, i))],
        out_shape=[jax.ShapeDtypeStruct((PEER_SEL, t), I32), jax.ShapeDtypeStruct((PEER_SEL, t), F32)],
        compiler_params=_cparams(("arbitrary",)),
        name="peer_route",
    )(h2, wq, keys)


def _split_bf16(x):
    hi = x.astype(BF16)
    return hi, (x - hi.astype(F32)).astype(BF16)


def _peer_expert_kernel(idx_cur_ref, idx_nxt_ref, uv_ref, h_ref, gate_ref, seg_ref, grpt_ref,
                        o_ref, buf_ref, sem_ref, *, tb, tg):
    i = pl.program_id(0)
    nsteps = pl.num_programs(0)

    def row_copy(idx_ref, slot, t, j):
        return pltpu.make_async_copy(uv_ref.at[idx_ref[t * PEER_SEL + j]],
                                     buf_ref.at[slot, t, j], sem_ref.at[slot, t])

    def wait_tokens(slot, toks):
        for t in toks:
            for j in range(PEER_SEL):
                row_copy(idx_cur_ref, slot, t, j).wait()

    @pl.when(i == 0)
    def _():
        for t in range(tb):
            for j in range(PEER_SEL):
                row_copy(idx_cur_ref, 0, t, j).start(priority=j % 2)

    diag = (lax.broadcasted_iota(I32, (SLAB, PEER_SEL * SLAB), 1) & (SLAB - 1)) == \
        lax.broadcasted_iota(I32, (SLAB, PEER_SEL * SLAB), 0)

    def consume(slot):
        copies = [(t, j) for t in range(tb) for j in range(PEER_SEL)]
        issue_points = (tb + GATE_LAG) * (PEER_SEL // SLAB) * 2
        progress = {"copies": 0, "points": 0}

        def issue_point():
            progress["points"] += 1
            upto = -(-len(copies) * progress["points"] // issue_points)
            while progress["copies"] < upto:
                t, j = copies[progress["copies"]]
                row_copy(idx_nxt_ref, 1 - slot, t, j).start(priority=j % 2)
                progress["copies"] += 1

        wait_tokens(slot, range(tb // 2))
        g_hi, g_lo = _split_bf16(gate_ref[...])
        gate_rep = (jnp.dot(g_hi, grpt_ref[...], preferred_element_type=F32)
                    + jnp.dot(g_lo, grpt_ref[...], preferred_element_type=F32))

        def token_pair(t_u, t_v, w_row):
            z_parts, r = [], None
            if t_v is not None:
                wt = jnp.where(diag, jnp.broadcast_to(w_row, diag.shape), 0.0)
                wt_hi, wt_lo = _split_bf16(wt)
                wt2 = jnp.concatenate([wt_hi, wt_lo], axis=0)
            for n in range(PEER_SEL // SLAB):
                e0, c0 = n * SLAB, n * SLAB * SLAB
                issue_point()
                if t_u is not None:
                    u_tile = buf_ref[slot, t_u, e0:e0 + SLAB, 0:SLAB, :].reshape(SLAB * SLAB, LANES)
                    y = lax.dot_general(h_ref[t_u], u_tile, (((1,), (1,)), ((), ())),
                                        preferred_element_type=F32)
                    z_parts.append(jnp.sum(jnp.where(diag[:, :SLAB * SLAB], y, 0.0),
                                           axis=0, keepdims=True))
                issue_point()
                if t_v is not None:
                    v_tile = buf_ref[slot, t_v, e0:e0 + SLAB, SLAB:2 * SLAB, :].reshape(
                        SLAB * SLAB, LANES)
                    part = jnp.dot(wt2[:, c0:c0 + SLAB * SLAB], v_tile, preferred_element_type=F32)
                    r = part if r is None else r + part
            z = jnp.concatenate(z_parts, axis=1) if z_parts else None
            out = None if r is None else r[:SLAB] + r[SLAB:]
            return z, out

        def gate_rows(g, z):
            tile = SLAB * SLAB
            z8 = jnp.concatenate([z, jnp.zeros((8 - tg, z.shape[1]), F32)], axis=0) if tg < 8 else z
            stacked = jnp.concatenate([z8[:, n * tile:(n + 1) * tile]
                                       for n in range(z.shape[1] // tile)], axis=0)
            s_hi, s_lo = _split_bf16(stacked)
            seg = jnp.dot(jnp.concatenate([s_hi, s_lo], axis=0), seg_ref[...],
                          preferred_element_type=F32)
            seg = seg[:stacked.shape[0]] + seg[stacked.shape[0]:]
            act = jnp.concatenate([seg[8 * n:8 * n + tg] for n in range(z.shape[1] // tile)], axis=1)
            gelu = 0.5 * act * (1.0 + jnp.tanh(math.sqrt(2.0 / math.pi)
                                               * (act + 0.044715 * act * act * act)))
            return gate_rep[g * tg:(g + 1) * tg, :] * gelu

        assert GATE_LAG >= tg + 2
        w_reps, outs, zs = {}, {}, []
        for n in range(tb + GATE_LAG):
            t_u = n if n < tb else None
            t_v = n - GATE_LAG if n >= GATE_LAG else None
            w_row = None if t_v is None else w_reps[t_v // tg][t_v % tg:t_v % tg + 1]
            if n == tb // 2:
                wait_tokens(slot, range(tb // 2, tb))
            z, out = token_pair(t_u, t_v, w_row)
            if t_u is not None:
                zs.append(z)
            if t_v is not None:
                outs[t_v] = out
            if n % tg == 0 and tg <= n <= tb:
                g = n // tg - 1
                w_reps[g] = gate_rows(g, jnp.concatenate(zs[g * tg:(g + 1) * tg], axis=0))
        for t in range(tb):
            for r in range(SLAB):
                o_ref[t:t + 1, r * LANES:(r + 1) * LANES] = outs[t][r:r + 1, :]

        @pl.when(i == nsteps - 1)
        def _():
            wait_tokens(1 - slot, range(tb))

    @pl.when((i & 1) == 0)
    def _():
        consume(0)

    @pl.when((i & 1) == 1)
    def _():
        consume(1)


def _peer_expert(idx_flat, uv, h_slab, gate, tb):
    t = h_slab.shape[0]
    nsteps = t // tb
    tile = SLAB * SLAB
    seg = (lax.broadcasted_iota(I32, (tile, tile), 0) // SLAB
           == lax.broadcasted_iota(I32, (tile, tile), 1) // SLAB).astype(BF16)
    grpt = (lax.broadcasted_iota(I32, (PEER_SEL, PEER_SEL * SLAB), 0)
            == lax.broadcasted_iota(I32, (PEER_SEL, PEER_SEL * SLAB), 1) // SLAB).astype(BF16)
    blk = tb * PEER_SEL
    return pl.pallas_call(
        functools.partial(_peer_expert_kernel, tb=tb, tg=min(EXPERT_GROUP, tb)),
        grid=(nsteps,),
        in_specs=[pl.BlockSpec((blk,), lambda i: (i,), memory_space=pltpu.SMEM),
                  pl.BlockSpec((blk,), lambda i: (jnp.minimum(i + 1, nsteps - 1),),
                               memory_space=pltpu.SMEM),
                  pl.BlockSpec(memory_space=pl.ANY),
                  pl.BlockSpec((tb, SLAB, LANES), lambda i: (i, 0, 0)),
                  pl.BlockSpec((tb, PEER_SEL), lambda i: (i, 0)),
                  pl.BlockSpec(seg.shape, lambda i: (0, 0)),
                  pl.BlockSpec(grpt.shape, lambda i: (0, 0))],
        out_specs=pl.BlockSpec((tb, SLAB * LANES), lambda i: (i, 0)),
        out_shape=jax.ShapeDtypeStruct((t, SLAB * LANES), F32),
        scratch_shapes=[pltpu.VMEM((2, tb, PEER_SEL, 2 * SLAB, LANES), BF16),
                        pltpu.SemaphoreType.DMA((2, tb))],
        compiler_params=_cparams(("arbitrary",)),
        name="peer_expert",
    )(idx_flat, idx_flat, uv, h_slab, gate, seg, grpt)


def _final_kernel(x1_ref, pe_ref, g2_ref, gn_ref, o_ref):
    x2 = x1_ref[0] + g2_ref[0] * pe_ref[0]
    o_ref[0] = x2 * lax.rsqrt(jnp.mean(x2 * x2, axis=-1, keepdims=True) + EPS) * gn_ref[...]


def _final(x1, pe, g2, gain, tm):
    b, s, d = x1.shape
    tok = pl.BlockSpec((1, tm, d), lambda i, j: (i, j, 0))
    return pl.pallas_call(
        _final_kernel,
        grid=(b, s // tm),
        in_specs=[tok, tok, pl.BlockSpec((1, 1, d), lambda i, j: (i, 0, 0)),
                  pl.BlockSpec((1, d), lambda i, j: (0, 0))],
        out_specs=tok,
        out_shape=jax.ShapeDtypeStruct((b, s, d), F32),
        compiler_params=_cparams(("arbitrary", "arbitrary")),
        name="final_norm",
    )(x1, pe, g2, gain)


def kernel(x, c, ctx, c_ctx, w_ada, b_ada, norm_attn, w_in, diff_lambda_q1, diff_lambda_k1,
           diff_lambda_q2, diff_lambda_k2, diff_norm, swa_sink, swa_norm, w_out, norm_ffn,
           peer_w_q, peer_sub_keys, peer_u, peer_v, final_norm):
    b, s, d = x.shape
    assert w_ada.shape[0] == 1, "single layer only"
    t = b * s
    assert b < 8 and s % GRID_W == 0 and s % BAND == 0 and s % min(TOKEN_TILE, s) == 0
    assert s % min(DIFF_Q_TILE, s) == 0 and s % min(DIFF_K_TILE, s) == 0
    assert t % min(ROUTE_TILE, t) == 0 and t % EXPERT_BLOCK == 0

    cc = jnp.zeros((8, d), F32).at[:b].set(c).at[b].set(c_ctx)
    mod = _adaln(cc, w_ada[0], b_ada[0])
    sh1, sc1, g1, sh2, sc2, g2 = [m[:, None, :] for m in jnp.split(mod, 6, axis=-1)]

    w_in_b = w_in[0].astype(BF16)
    gain_attn = norm_attn[0].reshape(1, d)
    cos_t, sin_t = _rope_tables(s)
    tm = min(TOKEN_TILE, s)
    dvw = DIFF_HEADS * LANES
    kvw = SWA_KV * DH
    w_qk = jnp.concatenate([w_in_b[:, :2 * dvw], w_in_b[:, 3 * dvw:-kvw]], axis=1)
    w_vt = jnp.concatenate([w_in_b[:, 2 * dvw:3 * dvw], w_in_b[:, -kvw:]], axis=1).T
    p, vt = _inproj(x, sh1[:b], sc1[:b], gain_attn, w_qk, w_vt, cos_t, sin_t, True, tm)
    lc = ctx.shape[1]
    ones = jnp.ones((lc, LANES), F32)
    ctx_sh = jnp.broadcast_to(sh1[b:b + 1], (b, 1, d))
    ctx_sc = jnp.broadcast_to(sc1[b:b + 1], (b, 1, d))
    pc, vtc = _inproj(ctx, ctx_sh, ctx_sc, gain_attn, w_qk, w_vt, ones, ones, False, lc)

    lams = [v[0].reshape(1, DH).astype(F32) for v in
            (diff_lambda_q1, diff_lambda_k1, diff_lambda_q2, diff_lambda_k2)]
    a_diff = _diff_attn(p, vt, pc, vtc, lams, diff_norm[0].reshape(1, -1),
                        min(DIFF_Q_TILE, s), min(DIFF_K_TILE, s))
    a_swa = _swa_attn(p, vt, pc, vtc, swa_sink[0].astype(F32), swa_norm[0].reshape(1, -1))

    w_out_b = w_out[0].astype(BF16)
    dw = a_diff.shape[2]
    x1, h2 = _outproj(a_diff, a_swa, w_out_b[:dw], w_out_b[dw:], x, g1[:b], sh2[:b], sc2[:b],
                      norm_ffn[0].reshape(1, d), tm)

    idx_t, gate_t = _peer_route(h2.reshape(t, d), peer_w_q[0].astype(BF16),
                                peer_sub_keys[0].astype(BF16), min(ROUTE_TILE, t))
    uv = jnp.concatenate([peer_u[0].reshape(-1, SLAB, LANES),
                          peer_v[0].reshape(-1, SLAB, LANES)], axis=1).astype(BF16)
    pe = _peer_expert(idx_t.T.reshape(-1), uv, h2.reshape(t, SLAB, LANES), gate_t.T, EXPERT_BLOCK)

    return _final(x1, pe.reshape(b, s, d), g2[:b], final_norm.reshape(1, d), tm)
```
